```python
import jax, jax.numpy as jnp
from jax import lax
import numpy as np

D_MODEL = 4096
BATCH = 2
SEQ = 4096
DEPTH = 1
DEC_BATCH = 2
DEC_SEQ = 8192
PAST_LEN = 128

HEAD_DIM = 128
N_HEADS = D_MODEL // HEAD_DIM
N_KV_HEADS = N_HEADS // 4
Q_BLOCK = 128
ROPE_THETA = 10000.0
GRID_W = 64
ATT_WIDTH = N_HEADS * HEAD_DIM
KV_WIDTH = N_KV_HEADS * HEAD_DIM
N_FGROUPS = 8
FGROUP_DIM = D_MODEL // 16
F_WIDTH = N_FGROUPS * FGROUP_DIM
SPLITS = [int(v) for v in np.cumsum([ATT_WIDTH, KV_WIDTH, KV_WIDTH, F_WIDTH, D_MODEL])]
IN_WIDTH = ATT_WIDTH + 2 * KV_WIDTH + F_WIDTH + 2 * D_MODEL
PEER_HEADS = 8
PEER_NKEYS = 128
PEER_N = PEER_NKEYS * PEER_NKEYS
PEER_DKEY = 256
PEER_HALF = PEER_DKEY // 2
PEER_TOPK = 16
TOKEN_BLOCK = 128
EPS = 1e-6

kernel_name = 'hybrid_gqa_fnet_peer_encoder'


def rmsnorm(x, g):
    xf = x.astype(jnp.float32)
    y = xf * lax.rsqrt(jnp.mean(xf * xf, axis=-1, keepdims=True) + EPS)
    return (y * g.astype(jnp.float32)).astype(x.dtype)


def axial_rope_tables(seq_len):
    rows = seq_len // GRID_W
    row_ids = jnp.repeat(jnp.arange(rows), GRID_W).astype(jnp.float32)
    col_ids = jnp.tile(jnp.arange(GRID_W), rows).astype(jnp.float32)
    axis_dim = HEAD_DIM // 2
    inv_freq = ROPE_THETA ** (-jnp.arange(0, axis_dim, 2, dtype=jnp.float32) / axis_dim)
    ang = jnp.concatenate([row_ids[:, None] * inv_freq, col_ids[:, None] * inv_freq], axis=-1)
    return jnp.cos(ang), jnp.sin(ang)


def apply_rope(x, cos, sin):
    xf = x.astype(jnp.float32).reshape(x.shape[:-1] + (HEAD_DIM // 2, 2))
    x1, x2 = xf[..., 0], xf[..., 1]
    c = cos[None, :, None, :]
    s = sin[None, :, None, :]
    out = jnp.stack([x1 * c - x2 * s, x1 * s + x2 * c], axis=-1).reshape(x.shape)
    return out.astype(x.dtype)


def block_attention(q, k, v):
    b, s = q.shape[0], q.shape[1]
    grp = N_HEADS // N_KV_HEADS
    nblk = s // Q_BLOCK
    qb = q.reshape(b, nblk, Q_BLOCK, N_KV_HEADS, grp, HEAD_DIM).transpose(1, 0, 2, 3, 4, 5)
    scale = HEAD_DIM ** -0.5

    def one_block(qblk):
        sc = jnp.einsum('bqhgd,bkhd->bhgqk', qblk, k, preferred_element_type=jnp.float32) * scale
        p = jax.nn.softmax(sc, axis=-1).astype(v.dtype)
        return jnp.einsum('bhgqk,bkhd->bqhgd', p, v)

    out = lax.map(one_block, qb)
    return out.transpose(1, 0, 2, 3, 4, 5).reshape(b, s, ATT_WIDTH)


def fourier_mix(u):
    b, s = u.shape[0], u.shape[1]
    ug = u.astype(jnp.float32).reshape(b, s, N_FGROUPS, FGROUP_DIM)
    y = jnp.fft.fftn(ug, axes=(1, 3), norm='ortho').real
    return y.reshape(b, s, F_WIDTH).astype(u.dtype)


def peer_ffn(h, w_q, keys, u_tab, v_tab):
    b, s, d = h.shape
    t = b * s
    hf = h.reshape(t, d)
    q = (hf @ w_q).reshape(t, PEER_HEADS, 2, PEER_HALF)
    sc = jnp.einsum('thpc,hpnc->thpn', q, keys).astype(jnp.float32)
    top_v, top_i = lax.top_k(sc, PEER_TOPK)
    cand_v = top_v[:, :, 0, :, None] + top_v[:, :, 1, None, :]
    cand_i = top_i[:, :, 0, :, None] * PEER_NKEYS + top_i[:, :, 1, None, :]
    kk = PEER_TOPK * PEER_TOPK
    best_v, best_pos = lax.top_k(cand_v.reshape(t, PEER_HEADS, kk), PEER_TOPK)
    best_i = jnp.take_along_axis(cand_i.reshape(t, PEER_HEADS, kk), best_pos, axis=-1)
    gates = jax.nn.softmax(best_v, axis=-1)
    n_sel = PEER_HEADS * PEER_TOPK
    idx = best_i.reshape(t, n_sel).astype(jnp.int32)
    gw = gates.reshape(t, n_sel).astype(h.dtype)
    nblk = t // TOKEN_BLOCK

    def one_block(args):
        xb, ib, gb = args
        ub = jnp.take(u_tab, ib, axis=0)
        act = jax.nn.gelu(jnp.einsum('td,ted->te', xb, ub), approximate=False)
        vb = jnp.take(v_tab, ib, axis=0)
        return jnp.einsum('te,ted->td', act * gb, vb)

    out = lax.map(one_block, (hf.reshape(nblk, TOKEN_BLOCK, d),
                              idx.reshape(nblk, TOKEN_BLOCK, n_sel),
                              gw.reshape(nblk, TOKEN_BLOCK, n_sel)))
    return out.reshape(b, s, d)


def encoder_layer(x, c, cos, sin, w_ada, b_ada, norm1_g, norm2_g, w_in, q_norm_g, k_norm_g,
                  w_attn_br, w_four_br, w_out, w_peer_q, peer_keys, peer_u, peer_v):
    b, s, _ = x.shape
    mod = (jax.nn.silu(c) @ w_ada + b_ada)[:, None, :]
    sh1, sc1, g1, sh2, sc2, g2 = jnp.split(mod, 6, axis=-1)

    h = rmsnorm(x, norm1_g) * (1.0 + sc1) + sh1
    proj = h @ w_in
    q, k, v, f, ga, gf = jnp.split(proj, SPLITS, axis=-1)
    q = apply_rope(rmsnorm(q.reshape(b, s, N_HEADS, HEAD_DIM), q_norm_g), cos, sin)
    k = apply_rope(rmsnorm(k.reshape(b, s, N_KV_HEADS, HEAD_DIM), k_norm_g), cos, sin)
    v = v.reshape(b, s, N_KV_HEADS, HEAD_DIM)
    a_br = block_attention(q, k, v) @ w_attn_br
    f_br = fourier_mix(f) @ w_four_br
    merged = jax.nn.sigmoid(ga) * a_br + jax.nn.sigmoid(gf) * f_br
    x = x + g1 * (merged @ w_out)

    h2 = rmsnorm(x, norm2_g) * (1.0 + sc2) + sh2
    x = x + g2 * peer_ffn(h2, w_peer_q, peer_keys, peer_u, peer_v)
    return x


def run_trunk(x, c, w_ada, b_ada, norm1_g, norm2_g, w_in, q_norm_g, k_norm_g, w_attn_br,
              w_four_br, w_out, w_peer_q, peer_keys, peer_u, peer_v, final_g):
    cos, sin = axial_rope_tables(x.shape[1])
    for l in range(DEPTH):
        x = encoder_layer(x, c, cos, sin, w_ada[l], b_ada[l], norm1_g[l], norm2_g[l], w_in[l],
                          q_norm_g[l], k_norm_g[l], w_attn_br[l], w_four_br[l], w_out[l],
                          w_peer_q[l], peer_keys[l], peer_u[l], peer_v[l])
    return rmsnorm(x, final_g)


def setup_inputs(seed: int = 0) -> dict:
    key = jax.random.key(seed)
    ks = jax.random.split(key, 20)
    f32 = jnp.float32
    nrm = lambda k, shape, sc: jax.random.normal(k, shape, f32) * sc
    gain = lambda k, shape: 1.0 + 0.02 * jax.random.normal(k, shape, f32)
    L, D = DEPTH, D_MODEL
    return {
        'x_prompt': nrm(ks[0], (BATCH, SEQ, D), 1.0),
        'x_sample': nrm(ks[1], (DEC_BATCH, DEC_SEQ, D), 1.0),
        'c_prompt': nrm(ks[2], (BATCH, D), 1.0),
        'c_sample': nrm(ks[3], (DEC_BATCH, D), 1.0),
        'w_ada': nrm(ks[4], (L, D, 6 * D), 0.5 * D ** -0.5),
        'b_ada': nrm(ks[5], (L, 6 * D), 0.01),
        'norm1_g': gain(ks[6], (L, D)),
        'norm2_g': gain(ks[7], (L, D)),
        'w_in': nrm(ks[8], (L, D, IN_WIDTH), D ** -0.5),
        'q_norm_g': gain(ks[9], (L, HEAD_DIM)),
        'k_norm_g': gain(ks[10], (L, HEAD_DIM)),
        'w_attn_br': nrm(ks[11], (L, ATT_WIDTH, D), ATT_WIDTH ** -0.5),
        'w_four_br': nrm(ks[12], (L, F_WIDTH, D), F_WIDTH ** -0.5),
        'w_out': nrm(ks[13], (L, D, D), D ** -0.5),
        'w_peer_q': nrm(ks[14], (L, D, PEER_HEADS * PEER_DKEY), D ** -0.5),
        'peer_keys': nrm(ks[15], (L, PEER_HEADS, 2, PEER_NKEYS, PEER_HALF), PEER_HALF ** -0.5),
        'peer_u': nrm(ks[16], (L, PEER_N, D), D ** -0.5),
        'peer_v': nrm(ks[17], (L, PEER_N, D), PEER_HEADS ** -0.5),
        'final_g': gain(ks[18], (D,)),
    }


def reference(x_prompt, x_sample, c_prompt, c_sample, w_ada, b_ada, norm1_g, norm2_g, w_in,
              q_norm_g, k_norm_g, w_attn_br, w_four_br, w_out, w_peer_q, peer_keys, peer_u,
              peer_v, final_g):
    y_prompt = run_trunk(x_prompt, c_prompt, w_ada, b_ada, norm1_g, norm2_g, w_in, q_norm_g,
                         k_norm_g, w_attn_br, w_four_br, w_out, w_peer_q, peer_keys, peer_u,
                         peer_v, final_g)
    y_sample = run_trunk(x_sample, c_sample, w_ada, b_ada, norm1_g, norm2_g, w_in, q_norm_g,
                         k_norm_g, w_attn_br, w_four_br, w_out, w_peer_q, peer_keys, peer_u,
                         peer_v, final_g)
    return (y_prompt, y_sample)
```

```python
import functools
import math

import jax
import jax.numpy as jnp
import numpy as np
from jax import lax
from jax.experimental import pallas as pl
from jax.experimental.pallas import tpu as pltpu

F32 = jnp.float32
BF16 = jnp.bfloat16

HEAD_DIM = 128
GQA_GROUP = 4
Q_PER_TILE = GQA_GROUP * HEAD_DIM
ROPE_THETA = 10000.0
GRID_W = 64
N_FGROUPS = 8
PEER_HEADS = 8
PEER_NKEYS = 128
PEER_HALF = 128
PEER_TOPK = 16
EPS = 1e-6
LANES = 128
NEG_INF = float("-inf")
MIB = 1024 * 1024


def _params(sem, vmem_mib):
    return pltpu.CompilerParams(dimension_semantics=sem, vmem_limit_bytes=vmem_mib * MIB)


def _mod_kernel(c_ref, w_ref, b_ref, o_ref):
    c = c_ref[...]
    a = c * jax.nn.sigmoid(c)
    o_ref[...] = jnp.dot(a, w_ref[...], preferred_element_type=F32,
                         precision=lax.Precision.HIGHEST) + b_ref[...]


def _modulation(c_pad, w_ada, b_ada):
    rows, d = c_pad.shape
    n = w_ada.shape[1]
    tn = 512
    return pl.pallas_call(
        _mod_kernel,
        grid=(n // tn,),
        in_specs=[pl.BlockSpec((rows, d), lambda j: (0, 0)),
                  pl.BlockSpec((d, tn), lambda j: (0, j)),
                  pl.BlockSpec((1, tn), lambda j: (0, j))],
        out_specs=pl.BlockSpec((rows, tn), lambda j: (0, j)),
        out_shape=jax.ShapeDtypeStruct((rows, n), F32),
        compiler_params=_params(("arbitrary",), 40),
        name="mod",
    )(c_pad, w_ada, b_ada.reshape(1, n))


def _ln_kernel(x_ref, g_ref, sc_ref, sh_ref, o_ref, *, transpose, chunk):
    x = x_ref[...]
    ms = jnp.mean(x * x, axis=-1, keepdims=True)
    y = x * lax.rsqrt(ms + EPS) * g_ref[...]
    h = y * (1.0 + sc_ref[0]) + sh_ref[0]
    if transpose:
        d = h.shape[1]
        for c in range(d // chunk):
            o_ref[c * chunk:(c + 1) * chunk, :] = h[:, c * chunk:(c + 1) * chunk].T.astype(BF16)
    else:
        o_ref[...] = h.astype(BF16)


def _ln_mod(x2d, g, sc, sh, seq, *, transpose):
    t, d = x2d.shape
    tm = 256
    per_b = seq // tm
    if transpose:
        out_spec = pl.BlockSpec((d, tm), lambda i: (0, i))
        out_shape = jax.ShapeDtypeStruct((d, t), BF16)
    else:
        out_spec = pl.BlockSpec((tm, d), lambda i: (i, 0))
        out_shape = jax.ShapeDtypeStruct((t, d), BF16)
    return pl.pallas_call(
        functools.partial(_ln_kernel, transpose=transpose, chunk=min(512, d)),
        grid=(t // tm,),
        in_specs=[pl.BlockSpec((tm, d), lambda i: (i, 0)),
                  pl.BlockSpec((1, d), lambda i: (0, 0)),
                  pl.BlockSpec((1, 1, d), lambda i: (i // per_b, 0, 0)),
                  pl.BlockSpec((1, 1, d), lambda i: (i // per_b, 0, 0))],
        out_specs=out_spec,
        out_shape=out_shape,
        compiler_params=_params(("arbitrary",), 40),
        name="ln_t" if transpose else "ln",
    )(x2d, g.reshape(1, d), sc, sh)


def _norm_rope(acc, g, cc, ss, scale, o_ref):
    for hd in range(acc.shape[1] // HEAD_DIM):
        xh = acc[:, hd * HEAD_DIM:(hd + 1) * HEAD_DIM]
        ms = jnp.mean(xh * xh, axis=-1, keepdims=True)
        y = xh * lax.rsqrt(ms + EPS) * g
        y = y * cc + pltpu.roll(y, HEAD_DIM // 2, axis=1) * ss
        if scale != 1.0:
            y = y * scale
        o_ref[:, hd * HEAD_DIM:(hd + 1) * HEAD_DIM] = y.astype(BF16)


def _inproj_kernel(h_ref, w_ref, qg_ref, kg_ref, cc_ref, ss_ref, o_ref, *, nq, nk, nv, q_scale):
    j = pl.program_id(1)
    acc = jnp.dot(h_ref[...], w_ref[...], preferred_element_type=F32)

    @pl.when(j < nq)
    def _():
        _norm_rope(acc, qg_ref[...], cc_ref[...], ss_ref[...], q_scale, o_ref)

    @pl.when(jnp.logical_and(j >= nq, j < nq + nk))
    def _():
        _norm_rope(acc, kg_ref[...], cc_ref[...], ss_ref[...], 1.0, o_ref)

    @pl.when(jnp.logical_and(j >= nq + nk, j < nq + nk + nv))
    def _():
        o_ref[...] = acc.astype(BF16)

    @pl.when(j >= nq + nk + nv)
    def _():
        o_ref[...] = jax.nn.sigmoid(acc).astype(BF16)


def _inproj(h, w_main, qg, kg, cc, ss, seq, att_w, kv_w):
    t, d = h.shape
    wtot = w_main.shape[1]
    tm = 512
    tn = min(1024, kv_w)
    per_b = seq // tm
    kern = functools.partial(_inproj_kernel, nq=att_w // tn, nk=kv_w // tn, nv=kv_w // tn,
                             q_scale=HEAD_DIM ** -0.5)
    return pl.pallas_call(
        kern,
        grid=(t // tm, wtot // tn),
        in_specs=[pl.BlockSpec((tm, d), lambda i, j: (i, 0)),
                  pl.BlockSpec((d, tn), lambda i, j: (0, j)),
                  pl.BlockSpec((1, HEAD_DIM), lambda i, j: (0, 0)),
                  pl.BlockSpec((1, HEAD_DIM), lambda i, j: (0, 0)),
                  pl.BlockSpec((tm, HEAD_DIM), lambda i, j: (i % per_b, 0)),
                  pl.BlockSpec((tm, HEAD_DIM), lambda i, j: (i % per_b, 0))],
        out_specs=pl.BlockSpec((tm, tn), lambda i, j: (i, j)),
        out_shape=jax.ShapeDtypeStruct((t, wtot), BF16),
        compiler_params=_params(("arbitrary", "arbitrary"), 48),
        name="inproj",
    )(h, w_main, qg, kg, cc, ss)


def _fproj_kernel(h_ref, w_ref, cs_ref, p_ref, q_ref, *, gd):
    acc = jnp.dot(h_ref[...], w_ref[...], preferred_element_type=F32)
    for g in range(acc.shape[1] // gd):
        u = acc[:, g * gd:(g + 1) * gd].astype(BF16)
        pq = jnp.dot(u, cs_ref[...], preferred_element_type=F32)
        p_ref[:, g * gd:(g + 1) * gd] = pq[:, :gd].astype(BF16)
        q_ref[:, g * gd:(g + 1) * gd] = pq[:, gd:].astype(BF16)


def _fproj(h, w_f, cs, gd):
    t, d = h.shape
    fw = w_f.shape[1]
    tm = 512
    tn = min(1024, fw)
    out = jax.ShapeDtypeStruct((t, fw), BF16)
    return pl.pallas_call(
        functools.partial(_fproj_kernel, gd=gd),
        grid=(t // tm, fw // tn),
        in_specs=[pl.BlockSpec((tm, d), lambda i, j: (i, 0)),
                  pl.BlockSpec((d, tn), lambda i, j: (0, j)),
                  pl.BlockSpec((gd, 2 * gd), lambda i, j: (0, 0))],
        out_specs=[pl.BlockSpec((tm, tn), lambda i, j: (i, j)),
                   pl.BlockSpec((tm, tn), lambda i, j: (i, j))],
        out_shape=[out, out],
        compiler_params=_params(("arbitrary", "arbitrary"), 48),
        name="fproj",
    )(h, w_f, cs)


def _attn_kernel(q_ref, k_ref, v_ref, o_ref, m_scr, l_scr, acc_scr, *, tk):
    tq = q_ref.shape[0]
    seq = k_ref.shape[0]
    q = jnp.concatenate([q_ref[:, g * HEAD_DIM:(g + 1) * HEAD_DIM] for g in range(GQA_GROUP)], axis=0)
    m_scr[...] = jnp.full(m_scr.shape, NEG_INF, F32)
    l_scr[...] = jnp.zeros(l_scr.shape, F32)
    acc_scr[...] = jnp.zeros(acc_scr.shape, F32)

    def body(c, carry):
        start = pl.multiple_of(c * tk, tk)
        k = k_ref[pl.ds(start, tk), :]
        v = v_ref[pl.ds(start, tk), :]
        s = lax.dot_general(q, k, (((1,), (1,)), ((), ())), preferred_element_type=F32)
        m_prev = m_scr[...]
        m_next = jnp.maximum(m_prev, jnp.max(s, axis=1, keepdims=True))
        p = jnp.exp(s - jnp.tile(m_next, (1, tk // LANES)))
        alpha = jnp.exp(m_prev - m_next)
        l_scr[...] = alpha * l_scr[...] + jnp.sum(p, axis=1, keepdims=True)
        acc_scr[...] = alpha * acc_scr[...] + jnp.dot(p.astype(BF16), v, preferred_element_type=F32)
        m_scr[...] = m_next
        return carry

    lax.fori_loop(0, seq // tk, body, 0)
    out = acc_scr[...] / l_scr[...]
    for g in range(GQA_GROUP):
        o_ref[:, g * HEAD_DIM:(g + 1) * HEAD_DIM] = out[g * tq:(g + 1) * tq].astype(BF16)


def _attention(proj, nb, seq, att_w, kv_w):
    t = proj.shape[0]
    tq = 256
    tk = min(512, seq)
    n_kv = kv_w // HEAD_DIM
    per_b = seq // tq
    k_off = att_w // HEAD_DIM
    v_off = (att_w + kv_w) // HEAD_DIM
    rows = GQA_GROUP * tq
    return pl.pallas_call(
        functools.partial(_attn_kernel, tk=tk),
        grid=(nb, n_kv, per_b),
        in_specs=[pl.BlockSpec((tq, Q_PER_TILE), lambda b, g, i: (b * per_b + i, g)),
                  pl.BlockSpec((seq, HEAD_DIM), lambda b, g, i: (b, k_off + g)),
                  pl.BlockSpec((seq, HEAD_DIM), lambda b, g, i: (b, v_off + g))],
        out_specs=pl.BlockSpec((tq, Q_PER_TILE), lambda b, g, i: (b * per_b + i, g)),
        out_shape=jax.ShapeDtypeStruct((t, att_w), BF16),
        scratch_shapes=[pltpu.VMEM((rows, LANES), F32),
                        pltpu.VMEM((rows, LANES), F32),
                        pltpu.VMEM((rows, HEAD_DIM), F32)],
        compiler_params=_params(("arbitrary", "arbitrary", "arbitrary"), 48),
        name="attn",
    )(proj, proj, proj)


def _fft_a_kernel(m_ref, p_ref, q_ref, twc_ref, tws_ref, ar_ref, ai_ref, *, fw):
    n1 = p_ref.shape[1]
    rhs = jnp.concatenate([p_ref[0], q_ref[0]], axis=0)
    a = jnp.dot(m_ref[...], rhs, preferred_element_type=F32)
    rep = fw // LANES
    for u in range(p_ref.shape[2] // fw):
        ar = a[:n1, u * fw:(u + 1) * fw]
        ai = a[n1:, u * fw:(u + 1) * fw]
        c = jnp.tile(twc_ref[u], (1, rep))
        s = jnp.tile(tws_ref[u], (1, rep))
        ar_ref[0, :, u * fw:(u + 1) * fw] = (ar * c + ai * s).astype(BF16)
        ai_ref[0, :, u * fw:(u + 1) * fw] = (ai * c - ar * s).astype(BF16)


def _fft_c_kernel(cs_ref, ar_ref, ai_ref, o_ref, *, scale):
    rhs = jnp.concatenate([ar_ref[0, 0], ai_ref[0, 0]], axis=0)
    o_ref[0] = (jnp.dot(cs_ref[...], rhs, preferred_element_type=F32) * scale).astype(BF16)


def _dft_tables(n):
    idx = (np.arange(n)[:, None] * np.arange(n)[None, :]) % n
    ang = 2.0 * np.pi * idx / n
    return np.cos(ang), np.sin(ang)


def _seq_dft(p, q, nb, seq, gd):
    fw = p.shape[1]
    n1 = 64 if seq >= 4096 else 16
    n2 = seq // n1
    c1, s1 = _dft_tables(n1)
    c2, s2 = _dft_tables(n2)
    m_a = jnp.asarray(np.block([[c1, -s1], [-s1, -c1]]), BF16)
    cs_c = jnp.asarray(np.concatenate([c2, s2], axis=1), BF16)
    tw = 2.0 * np.pi * (np.arange(n2)[:, None] * np.arange(n1)[None, :]) / seq
    twc = jnp.asarray(np.broadcast_to(np.cos(tw)[:, :, None], (n2, n1, LANES)), F32)
    tws = jnp.asarray(np.broadcast_to(np.sin(tw)[:, :, None], (n2, n1, LANES)), F32)

    nt2 = 4
    tc = nt2 * fw
    p3 = p.reshape(nb, n1, n2 * fw)
    q3 = q.reshape(nb, n1, n2 * fw)
    a_shape = jax.ShapeDtypeStruct((nb, n1, n2 * fw), BF16)
    blk = pl.BlockSpec((1, n1, tc), lambda b, j: (b, 0, j))
    tw_blk = pl.BlockSpec((nt2, n1, LANES), lambda b, j: (j, 0, 0))
    ar, ai = pl.pallas_call(
        functools.partial(_fft_a_kernel, fw=fw),
        grid=(nb, n2 // nt2),
        in_specs=[pl.BlockSpec((2 * n1, 2 * n1), lambda b, j: (0, 0)), blk, blk, tw_blk, tw_blk],
        out_specs=[blk, blk],
        out_shape=[a_shape, a_shape],
        compiler_params=_params(("arbitrary", "arbitrary"), 40),
        name="fft_a",
    )(m_a, p3, q3, twc, tws)

    in_blk = pl.BlockSpec((1, 1, n2, fw), lambda b, k: (b, k, 0, 0))
    out = pl.pallas_call(
        functools.partial(_fft_c_kernel, scale=1.0 / math.sqrt(seq * gd)),
        grid=(nb, n1),
        in_specs=[pl.BlockSpec((n2, 2 * n2), lambda b, k: (0, 0)), in_blk, in_blk],
        out_specs=pl.BlockSpec((1, n2, fw), lambda b, k: (b, 0, k)),
        out_shape=jax.ShapeDtypeStruct((nb, n2, n1 * fw), BF16),
        compiler_params=_params(("arbitrary", "arbitrary"), 40),
        name="fft_c",
    )(cs_c, ar.reshape(nb, n1, n2, fw), ai.reshape(nb, n1, n2, fw))
    return out.reshape(nb * seq, fw)


def _merge_kernel(a_ref, f_ref, wa_ref, wf_ref, ga_ref, gf_ref, o_ref):
    a_br = jnp.dot(a_ref[...], wa_ref[...], preferred_element_type=F32)
    f_br = jnp.dot(f_ref[...], wf_ref[...], preferred_element_type=F32)
    o_ref[...] = (ga_ref[...].astype(F32) * a_br + gf_ref[...].astype(F32) * f_br).astype(BF16)


def _merge(att, four, wa, wf, proj, gate_off):
    t, aw = att.shape
    fw = four.shape[1]
    d = wa.shape[1]
    tm, tn = 512, 512
    ga0 = gate_off // tn
    gf0 = (gate_off + d) // tn
    return pl.pallas_call(
        _merge_kernel,
        grid=(t // tm, d // tn),
        in_specs=[pl.BlockSpec((tm, aw), lambda i, j: (i, 0)),
                  pl.BlockSpec((tm, fw), lambda i, j: (i, 0)),
                  pl.BlockSpec((aw, tn), lambda i, j: (0, j)),
                  pl.BlockSpec((fw, tn), lambda i, j: (0, j)),
                  pl.BlockSpec((tm, tn), lambda i, j: (i, ga0 + j)),
                  pl.BlockSpec((tm, tn), lambda i, j: (i, gf0 + j))],
        out_specs=pl.BlockSpec((tm, tn), lambda i, j: (i, j)),
        out_shape=jax.ShapeDtypeStruct((t, d), BF16),
        compiler_params=_params(("arbitrary", "arbitrary"), 48),
        name="merge",
    )(att, four, wa, wf, proj, proj)


def _outproj_kernel(m_ref, w_ref, x_ref, g_ref, o_ref):
    acc = jnp.dot(m_ref[...], w_ref[...], preferred_element_type=F32)
    o_ref[...] = x_ref[...] + g_ref[0] * acc


def _outproj(merged, w_out, x2d, g1, seq):
    t, d = merged.shape
    tm, tn = 512, 1024
    per_b = seq // tm
    return pl.pallas_call(
        _outproj_kernel,
        grid=(t // tm, d // tn),
        in_specs=[pl.BlockSpec((tm, d), lambda i, j: (i, 0)),
                  pl.BlockSpec((d, tn), lambda i, j: (0, j)),
                  pl.BlockSpec((tm, tn), lambda i, j: (i, j)),
                  pl.BlockSpec((1, 1, tn), lambda i, j: (i // per_b, 0, j))],
        out_specs=pl.BlockSpec((tm, tn), lambda i, j: (i, j)),
        out_shape=jax.ShapeDtypeStruct((t, d), F32),
        compiler_params=_params(("arbitrary", "arbitrary"), 48),
        name="outproj",
    )(merged, w_out, x2d, g1)


def _top_values(s, k, store):
    n = s.shape[0]
    iota = lax.broadcasted_iota(jnp.int32, s.shape, 0)
    cur = s
    for r in range(k):
        m = jnp.max(cur, axis=0, keepdims=True)
        store(r, m)
        first = jnp.min(jnp.where(cur == m, iota, n), axis=0, keepdims=True)
        cur = jnp.where(iota == first, NEG_INF, cur)


def _peer_score_kernel(h_ref, wq_ref, keys_ref, s0_ref, s1_ref, aux_ref, top_scr, best_scr):
    qt = jnp.dot(wq_ref[...], h_ref[...], preferred_element_type=F32).astype(BF16)
    for p, s_ref in enumerate((s0_ref, s1_ref)):
        s = jnp.dot(keys_ref[0, p], qt[p * PEER_HALF:(p + 1) * PEER_HALF], preferred_element_type=F32)
        s_ref[0] = s

        def store(r, row, p=p):
            top_scr[p, r:r + 1, :] = row
        _top_values(s, PEER_TOPK, store)

    t1 = top_scr[1]
    cand = jnp.concatenate([top_scr[0, i:i + 1, :] + t1 for i in range(PEER_TOPK)], axis=0)

    def store_best(r, row):
        best_scr[r:r + 1, :] = row
    _top_values(cand, PEER_TOPK, store_best)

    best = best_scr[...]
    top = best[0:1]
    z = jnp.sum(jnp.exp(best - top), axis=0, keepdims=True)
    aux_ref[0] = jnp.zeros(aux_ref.shape[1:], F32)
    aux_ref[0, 0:1, :] = best[PEER_TOPK - 1:PEER_TOPK]
    aux_ref[0, 1:2, :] = top + jnp.log(z)


def _peer_scores(h2t, wq_t, keys):
    d, t = h2t.shape
    tm = 512
    s_shape = jax.ShapeDtypeStruct((PEER_HEADS, PEER_NKEYS, t), F32)
    s_blk = pl.BlockSpec((1, PEER_NKEYS, tm), lambda i, h: (h, 0, i))
    return pl.pallas_call(
        _peer_score_kernel,
        grid=(t // tm, PEER_HEADS),
        in_specs=[pl.BlockSpec((d, tm), lambda i, h: (0, i)),
                  pl.BlockSpec((2 * PEER_HALF, d), lambda i, h: (h, 0)),
                  pl.BlockSpec((1, 2, PEER_NKEYS, PEER_HALF), lambda i, h: (h, 0, 0, 0))],
        out_specs=[s_blk, s_blk, pl.BlockSpec((1, 8, tm), lambda i, h: (h, 0, i))],
        out_shape=[s_shape, s_shape, jax.ShapeDtypeStruct((PEER_HEADS, 8, t), F32)],
        scratch_shapes=[pltpu.VMEM((2, PEER_TOPK, tm), F32), pltpu.VMEM((PEER_TOPK, tm), F32)],
        compiler_params=_params(("arbitrary", "arbitrary"), 40),
        name="peer_scores",
    )(h2t, wq_t, keys)


def _peer_kernel(h_ref, u_ref, v_ref, s0_ref, s1_ref, aux_ref, o_ref):
    j = pl.program_id(1)
    te = u_ref.shape[0]
    rows = te // PEER_NKEYS
    st = jnp.dot(u_ref[...], h_ref[...], preferred_element_type=F32)
    act = 0.5 * st * (1.0 + lax.erf(st * math.sqrt(0.5)))
    parts = []
    for r in range(rows):
        n0 = j * rows + r
        gate = None
        for hd in range(PEER_HEADS):
            sm = s0_ref[hd, pl.ds(n0, 1), :] + s1_ref[hd]
            g = jnp.where(sm >= aux_ref[hd, 0:1, :], jnp.exp(sm - aux_ref[hd, 1:2, :]), 0.0)
            gate = g if gate is None else gate + g
        parts.append(act[r * PEER_NKEYS:(r + 1) * PEER_NKEYS] * gate)
    w = jnp.concatenate(parts, axis=0).T.astype(BF16)
    contrib = jnp.dot(w, v_ref[...], preferred_element_type=F32)

    @pl.when(j == 0)
    def _():
        o_ref[...] = contrib

    @pl.when(j > 0)
    def _():
        o_ref[...] += contrib


def _peer(h2t, u_bf, v_bf, s0, s1, aux):
    d, t = h2t.shape
    n_exp = u_bf.shape[0]
    tm, te = 512, 512
    once = pl.Buffered(1)
    sc_blk = pl.BlockSpec((PEER_HEADS, PEER_NKEYS, tm), lambda i, j: (0, 0, i), pipeline_mode=once)
    return pl.pallas_call(
        _peer_kernel,
        grid=(t // tm, n_exp // te),
        in_specs=[pl.BlockSpec((d, tm), lambda i, j: (0, i), pipeline_mode=once),
                  pl.BlockSpec((te, d), lambda i, j: (j, 0)),
                  pl.BlockSpec((te, d), lambda i, j: (j, 0)),
                  sc_blk, sc_blk,
                  pl.BlockSpec((PEER_HEADS, 8, tm), lambda i, j: (0, 0, i))],
        out_specs=pl.BlockSpec((tm, d), lambda i, j: (i, 0), pipeline_mode=once),
        out_shape=jax.ShapeDtypeStruct((t, d), F32),
        compiler_params=_params(("arbitrary", "arbitrary"), 56),
        name="peer",
    )(h2t, u_bf, v_bf, s0, s1, aux)


def _final_kernel(x_ref, p_ref, g2_ref, fg_ref, o_ref):
    x = x_ref[...] + g2_ref[0] * p_ref[...]
    ms = jnp.mean(x * x, axis=-1, keepdims=True)
    o_ref[...] = x * lax.rsqrt(ms + EPS) * fg_ref[...]


def _final(x1, peer, g2, final_g, seq):
    t, d = x1.shape
    tm = 256
    per_b = seq // tm
    row = pl.BlockSpec((tm, d), lambda i: (i, 0))
    return pl.pallas_call(
        _final_kernel,
        grid=(t // tm,),
        in_specs=[row, row,
                  pl.BlockSpec((1, 1, d), lambda i: (i // per_b, 0, 0)),
                  pl.BlockSpec((1, d), lambda i: (0, 0))],
        out_specs=row,
        out_shape=jax.ShapeDtypeStruct((t, d), F32),
        compiler_params=_params(("arbitrary",), 48),
        name="final",
    )(x1, peer, g2, final_g.reshape(1, d))


def _rope_tables(seq):
    rows = seq // GRID_W
    row_ids = jnp.repeat(jnp.arange(rows), GRID_W).astype(F32)
    col_ids = jnp.tile(jnp.arange(GRID_W), rows).astype(F32)
    axis_dim = HEAD_DIM // 2
    inv_freq = ROPE_THETA ** (-jnp.arange(0, axis_dim, 2, dtype=F32) / axis_dim)
    ang = jnp.concatenate([row_ids[:, None] * inv_freq, col_ids[:, None] * inv_freq], axis=-1)
    cos, sin = jnp.cos(ang), jnp.sin(ang)
    return jnp.concatenate([cos, cos], axis=-1), jnp.concatenate([-sin, sin], axis=-1)


def _trunk(x, mod, wts, dims):
    nb, seq, d = x.shape
    att_w, kv_w, f_w, gd = dims
    x2d = x.reshape(nb * seq, d)
    sh1, sc1, g1, sh2, sc2, g2 = [mod[:, k].reshape(nb, 1, d) for k in range(6)]
    cc, ss = _rope_tables(seq)

    h = _ln_mod(x2d, wts["norm1_g"], sc1, sh1, seq, transpose=False)
    proj = _inproj(h, wts["w_main"], wts["qg"], wts["kg"], cc, ss, seq, att_w, kv_w)
    p, q = _fproj(h, wts["w_f"], wts["cs_ch"], gd)
    att = _attention(proj, nb, seq, att_w, kv_w)
    four = _seq_dft(p, q, nb, seq, gd)
    merged = _merge(att, four, wts["wa"], wts["wf"], proj, att_w + 2 * kv_w)
    x1 = _outproj(merged, wts["w_out"], x2d, g1, seq)

    h2t = _ln_mod(x1, wts["norm2_g"], sc2, sh2, seq, transpose=True)
    s0, s1, aux = _peer_scores(h2t, wts["wq_t"], wts["keys"])
    peer = _peer(h2t, wts["u"], wts["v"], s0, s1, aux)
    y = _final(x1, peer, g2, wts["final_g"], seq)
    return y.reshape(nb, seq, d)


def kernel(x_prompt, x_sample, c_prompt, c_sample, w_ada, b_ada, norm1_g, norm2_g, w_in, q_norm_g, k_norm_g, w_attn_br, w_four_br, w_out, w_peer_q, peer_keys, peer_u, peer_v, final_g):
    d = x_prompt.shape[-1]
    assert w_ada.shape[0] == 1, "single-layer trunk"
    att_w = w_attn_br.shape[1]
    f_w = w_four_br.shape[1]
    kv_w = (w_in.shape[2] - att_w - f_w - 2 * d) // 2
    gd = f_w // N_FGROUPS
    dims = (att_w, kv_w, f_w, gd)

    perm = np.concatenate([np.arange(0, HEAD_DIM, 2), np.arange(1, HEAD_DIM, 2)])
    w = w_in[0]

    def deinterleave(cols):
        return cols.reshape(d, -1, HEAD_DIM)[:, :, perm].reshape(d, -1)

    s_q, s_k, s_v, s_f = att_w, att_w + kv_w, att_w + 2 * kv_w, att_w + 2 * kv_w + f_w
    w_main = jnp.concatenate([deinterleave(w[:, :s_q]), deinterleave(w[:, s_q:s_k]), w[:, s_k:s_v],
                              w[:, s_f:]], axis=1).astype(BF16)
    cch, sch = _dft_tables(gd)
    wts = {
        "norm1_g": norm1_g[0], "norm2_g": norm2_g[0], "final_g": final_g,
        "w_main": w_main,
        "w_f": w[:, s_v:s_f].astype(BF16),
        "cs_ch": jnp.asarray(np.concatenate([cch, sch], axis=1), BF16),
        "qg": q_norm_g[0][perm].reshape(1, HEAD_DIM),
        "kg": k_norm_g[0][perm].reshape(1, HEAD_DIM),
        "wa": w_attn_br[0].astype(BF16), "wf": w_four_br[0].astype(BF16),
        "w_out": w_out[0].astype(BF16),
        "wq_t": w_peer_q[0].T.astype(BF16),
        "keys": peer_keys[0].astype(BF16),
        "u": peer_u[0].astype(BF16), "v": peer_v[0].astype(BF16),
    }

    nbp, nbs = c_prompt.shape[0], c_sample.shape[0]
    c_all = jnp.concatenate([c_prompt, c_sample], axis=0)
    c_pad = jnp.pad(c_all, ((0, (-c_all.shape[0]) % 8), (0, 0)))
    mod = _modulation(c_pad, w_ada[0], b_ada[0]).reshape(c_pad.shape[0], 6, d)

    y_prompt = _trunk(x_prompt, mod[:nbp], wts, dims)
    y_sample = _trunk(x_sample, mod[nbp:nbp + nbs], wts, dims)
    return (y_prompt, y_sample)
```

```python
import functools
import math

import jax
import jax.numpy as jnp
import numpy as np
from jax import lax
from jax.experimental import pallas as pl
from jax.experimental.pallas import tpu as pltpu

F32 = jnp.float32
BF16 = jnp.bfloat16

HEAD_DIM = 128
GQA_GROUP = 4
Q_PER_TILE = GQA_GROUP * HEAD_DIM
ROPE_THETA = 10000.0
GRID_W = 64
N_FGROUPS = 8
PEER_HEADS = 8
PEER_NKEYS = 128
PEER_HALF = 128
PEER_TOPK = 16
PEER_TRIPS = 2
EPS = 1e-6
LANES = 128
NEG_INF = float("-inf")
ONES_ROWS = 16
LOG2E = math.log2(math.e)
MIB = 1024 * 1024


def _params(sem, vmem_mib):
    return pltpu.CompilerParams(dimension_semantics=sem, vmem_limit_bytes=vmem_mib * MIB)


def _mod_kernel(c_ref, w_ref, b_ref, o_ref):
    c = c_ref[...]
    a = c * jax.nn.sigmoid(c)
    o_ref[...] = jnp.dot(a, w_ref[...], preferred_element_type=F32,
                         precision=lax.Precision.HIGHEST) + b_ref[...]


def _modulation(c_pad, w_ada, b_ada):
    rows, d = c_pad.shape
    n = w_ada.shape[1]
    tn = 512
    return pl.pallas_call(
        _mod_kernel,
        grid=(n // tn,),
        in_specs=[pl.BlockSpec((rows, d), lambda j: (0, 0)),
                  pl.BlockSpec((d, tn), lambda j: (0, j)),
                  pl.BlockSpec((1, tn), lambda j: (0, j))],
        out_specs=pl.BlockSpec((rows, tn), lambda j: (0, j)),
        out_shape=jax.ShapeDtypeStruct((rows, n), F32),
        compiler_params=_params(("arbitrary",), 40),
        name="mod",
    )(c_pad, w_ada, b_ada.reshape(1, n))


def _ln_kernel(x_ref, g_ref, sc_ref, sh_ref, o_ref, *, transpose, chunk):
    x = x_ref[...]
    ms = jnp.mean(x * x, axis=-1, keepdims=True)
    y = x * lax.rsqrt(ms + EPS) * g_ref[...]
    h = y * (1.0 + sc_ref[0]) + sh_ref[0]
    if transpose:
        d = h.shape[1]
        for c in range(d // chunk):
            o_ref[c * chunk:(c + 1) * chunk, :] = h[:, c * chunk:(c + 1) * chunk].T.astype(BF16)
    else:
        o_ref[...] = h.astype(BF16)


def _ln_mod(x2d, g, sc, sh, seq, *, transpose):
    t, d = x2d.shape
    tm = 256
    per_b = seq // tm
    if transpose:
        out_spec = pl.BlockSpec((d, tm), lambda i: (0, i))
        out_shape = jax.ShapeDtypeStruct((d, t), BF16)
    else:
        out_spec = pl.BlockSpec((tm, d), lambda i: (i, 0))
        out_shape = jax.ShapeDtypeStruct((t, d), BF16)
    return pl.pallas_call(
        functools.partial(_ln_kernel, transpose=transpose, chunk=min(512, d)),
        grid=(t // tm,),
        in_specs=[pl.BlockSpec((tm, d), lambda i: (i, 0)),
                  pl.BlockSpec((1, d), lambda i: (0, 0)),
                  pl.BlockSpec((1, 1, d), lambda i: (i // per_b, 0, 0)),
                  pl.BlockSpec((1, 1, d), lambda i: (i // per_b, 0, 0))],
        out_specs=out_spec,
        out_shape=out_shape,
        compiler_params=_params(("arbitrary",), 40),
        name="ln_t" if transpose else "ln",
    )(x2d, g.reshape(1, d), sc, sh)


def _norm_rope(acc, g, cc, ss, scale, o_ref):
    for hd in range(acc.shape[1] // HEAD_DIM):
        xh = acc[:, hd * HEAD_DIM:(hd + 1) * HEAD_DIM]
        ms = jnp.mean(xh * xh, axis=-1, keepdims=True)
        y = xh * lax.rsqrt(ms + EPS) * g
        y = y * cc + pltpu.roll(y, HEAD_DIM // 2, axis=1) * ss
        if scale != 1.0:
            y = y * scale
        o_ref[:, hd * HEAD_DIM:(hd + 1) * HEAD_DIM] = y.astype(BF16)


def _inproj_kernel(h_ref, w_ref, qg_ref, kg_ref, cc_ref, ss_ref, o_ref, *, nq, nk, nv, q_scale):
    j = pl.program_id(1)
    acc = jnp.dot(h_ref[...], w_ref[...], preferred_element_type=F32)

    @pl.when(j < nq)
    def _():
        _norm_rope(acc, qg_ref[...], cc_ref[...], ss_ref[...], q_scale, o_ref)

    @pl.when(jnp.logical_and(j >= nq, j < nq + nk))
    def _():
        _norm_rope(acc, kg_ref[...], cc_ref[...], ss_ref[...], 1.0, o_ref)

    @pl.when(jnp.logical_and(j >= nq + nk, j < nq + nk + nv))
    def _():
        o_ref[...] = acc.astype(BF16)

    @pl.when(j >= nq + nk + nv)
    def _():
        o_ref[...] = jax.nn.sigmoid(acc).astype(BF16)


def _inproj(h, w_main, qg, kg, cc, ss, seq, att_w, kv_w):
    t, d = h.shape
    wtot = w_main.shape[1]
    tm = 512
    tn = min(1024, kv_w)
    per_b = seq // tm
    kern = functools.partial(_inproj_kernel, nq=att_w // tn, nk=kv_w // tn, nv=kv_w // tn,
                             q_scale=HEAD_DIM ** -0.5 * LOG2E)
    return pl.pallas_call(
        kern,
        grid=(t // tm, wtot // tn),
        in_specs=[pl.BlockSpec((tm, d), lambda i, j: (i, 0)),
                  pl.BlockSpec((d, tn), lambda i, j: (0, j)),
                  pl.BlockSpec((1, HEAD_DIM), lambda i, j: (0, 0)),
                  pl.BlockSpec((1, HEAD_DIM), lambda i, j: (0, 0)),
                  pl.BlockSpec((tm, HEAD_DIM), lambda i, j: (i % per_b, 0)),
                  pl.BlockSpec((tm, HEAD_DIM), lambda i, j: (i % per_b, 0))],
        out_specs=pl.BlockSpec((tm, tn), lambda i, j: (i, j)),
        out_shape=jax.ShapeDtypeStruct((t, wtot), BF16),
        compiler_params=_params(("arbitrary", "arbitrary"), 48),
        name="inproj",
    )(h, w_main, qg, kg, cc, ss)


def _fproj_kernel(h_ref, w_ref, cs_ref, p_ref, q_ref, *, gd):
    acc = jnp.dot(h_ref[...], w_ref[...], preferred_element_type=F32)
    for g in range(acc.shape[1] // gd):
        u = acc[:, g * gd:(g + 1) * gd].astype(BF16)
        pq = jnp.dot(u, cs_ref[...], preferred_element_type=F32)
        p_ref[:, g * gd:(g + 1) * gd] = pq[:, :gd].astype(BF16)
        q_ref[:, g * gd:(g + 1) * gd] = pq[:, gd:].astype(BF16)


def _fproj(h, w_f, cs, gd):
    t, d = h.shape
    fw = w_f.shape[1]
    tm = 512
    tn = min(1024, fw)
    out = jax.ShapeDtypeStruct((t, fw), BF16)
    return pl.pallas_call(
        functools.partial(_fproj_kernel, gd=gd),
        grid=(t // tm, fw // tn),
        in_specs=[pl.BlockSpec((tm, d), lambda i, j: (i, 0)),
                  pl.BlockSpec((d, tn), lambda i, j: (0, j)),
                  pl.BlockSpec((gd, 2 * gd), lambda i, j: (0, 0))],
        out_specs=[pl.BlockSpec((tm, tn), lambda i, j: (i, j)),
                   pl.BlockSpec((tm, tn), lambda i, j: (i, j))],
        out_shape=[out, out],
        compiler_params=_params(("arbitrary", "arbitrary"), 48),
        name="fproj",
    )(h, w_f, cs)


def _attn_kernel(q_ref, k_ref, v_ref, o_ref, m_scr, acc_scr, qt_scr, sa_scr, sb_scr, *, tk):
    tq = q_ref.shape[0]
    seq = k_ref.shape[0]
    n_chunks = seq // tk
    for g in range(GQA_GROUP):
        qt_scr[:, g * tq:(g + 1) * tq] = q_ref[:, g * HEAD_DIM:(g + 1) * HEAD_DIM].astype(F32).T.astype(BF16)
    m_scr[...] = jnp.full(m_scr.shape, NEG_INF, F32)
    acc_scr[...] = jnp.zeros(acc_scr.shape, F32)
    ones = jnp.ones((ONES_ROWS, tk), BF16)

    def scores(c, st_ref):
        start = pl.multiple_of(c * tk, tk)
        st_ref[...] = jnp.dot(k_ref[pl.ds(start, tk), :], qt_scr[...], preferred_element_type=F32)

    def consume(c, st_ref):
        start = pl.multiple_of(c * tk, tk)
        v = v_ref[pl.ds(start, tk), :]
        st = st_ref[...]
        m_prev = m_scr[...]
        m_next = jnp.maximum(m_prev, jnp.max(st, axis=0, keepdims=True))
        p = jnp.exp2(st - m_next).astype(BF16)
        alpha = jnp.exp2(m_prev - m_next)
        vt = jnp.concatenate([v.astype(F32).T.astype(BF16), ones], axis=0)
        acc_scr[...] = alpha * acc_scr[...] + jnp.dot(vt, p, preferred_element_type=F32)
        m_scr[...] = m_next

    scores(0, sa_scr)

    def body(i, carry):
        c = 2 * i
        scores(c + 1, sb_scr)
        consume(c, sa_scr)
        scores(c + 2, sa_scr)
        consume(c + 1, sb_scr)
        return carry

    lax.fori_loop(0, n_chunks // 2 - 1, body, 0)
    scores(n_chunks - 1, sb_scr)
    consume(n_chunks - 2, sa_scr)
    consume(n_chunks - 1, sb_scr)
    acc = acc_scr[...]
    out_t = acc[:HEAD_DIM] / acc[HEAD_DIM:HEAD_DIM + 1]
    for g in range(GQA_GROUP):
        o_ref[:, g * HEAD_DIM:(g + 1) * HEAD_DIM] = out_t[:, g * tq:(g + 1) * tq].T.astype(BF16)


def _attention(proj, nb, seq, att_w, kv_w):
    t = proj.shape[0]
    tq = 256
    tk = min(512, seq)
    n_kv = kv_w // HEAD_DIM
    per_b = seq // tq
    k_off = att_w // HEAD_DIM
    v_off = (att_w + kv_w) // HEAD_DIM
    cols = GQA_GROUP * tq
    return pl.pallas_call(
        functools.partial(_attn_kernel, tk=tk),
        grid=(nb, n_kv, per_b),
        in_specs=[pl.BlockSpec((tq, Q_PER_TILE), lambda b, g, i: (b * per_b + i, g)),
                  pl.BlockSpec((seq, HEAD_DIM), lambda b, g, i: (b, k_off + g)),
                  pl.BlockSpec((seq, HEAD_DIM), lambda b, g, i: (b, v_off + g))],
        out_specs=pl.BlockSpec((tq, Q_PER_TILE), lambda b, g, i: (b * per_b + i, g)),
        out_shape=jax.ShapeDtypeStruct((t, att_w), BF16),
        scratch_shapes=[pltpu.VMEM((1, cols), F32),
                        pltpu.VMEM((HEAD_DIM + ONES_ROWS, cols), F32),
                        pltpu.VMEM((HEAD_DIM, cols), BF16),
                        pltpu.VMEM((tk, cols), F32),
                        pltpu.VMEM((tk, cols), F32)],
        compiler_params=_params(("arbitrary", "arbitrary", "arbitrary"), 48),
        name="attn",
    )(proj, proj, proj)


def _fft_a_kernel(m_ref, p_ref, q_ref, twc_ref, tws_ref, ar_ref, ai_ref, *, fw):
    n1 = p_ref.shape[1]
    rhs = jnp.concatenate([p_ref[0], q_ref[0]], axis=0)
    a = jnp.dot(m_ref[...], rhs, preferred_element_type=F32)
    rep = fw // LANES
    for u in range(p_ref.shape[2] // fw):
        ar = a[:n1, u * fw:(u + 1) * fw]
        ai = a[n1:, u * fw:(u + 1) * fw]
        c = jnp.tile(twc_ref[u], (1, rep))
        s = jnp.tile(tws_ref[u], (1, rep))
        ar_ref[0, :, u * fw:(u + 1) * fw] = (ar * c + ai * s).astype(BF16)
        ai_ref[0, :, u * fw:(u + 1) * fw] = (ai * c - ar * s).astype(BF16)


def _fft_c_kernel(cs_ref, ar_ref, ai_ref, o_ref, *, scale):
    rhs = jnp.concatenate([ar_ref[0, 0], ai_ref[0, 0]], axis=0)
    o_ref[0] = (jnp.dot(cs_ref[...], rhs, preferred_element_type=F32) * scale).astype(BF16)


def _dft_tables(n):
    idx = (np.arange(n)[:, None] * np.arange(n)[None, :]) % n
    ang = 2.0 * np.pi * idx / n
    return np.cos(ang), np.sin(ang)


def _seq_dft(p, q, nb, seq, gd):
    fw = p.shape[1]
    n1 = 64 if seq >= 4096 else 16
    n2 = seq // n1
    c1, s1 = _dft_tables(n1)
    c2, s2 = _dft_tables(n2)
    m_a = jnp.asarray(np.block([[c1, -s1], [-s1, -c1]]), BF16)
    cs_c = jnp.asarray(np.concatenate([c2, s2], axis=1), BF16)
    tw = 2.0 * np.pi * (np.arange(n2)[:, None] * np.arange(n1)[None, :]) / seq
    twc = jnp.asarray(np.broadcast_to(np.cos(tw)[:, :, None], (n2, n1, LANES)), F32)
    tws = jnp.asarray(np.broadcast_to(np.sin(tw)[:, :, None], (n2, n1, LANES)), F32)

    nt2 = 4
    tc = nt2 * fw
    p3 = p.reshape(nb, n1, n2 * fw)
    q3 = q.reshape(nb, n1, n2 * fw)
    a_shape = jax.ShapeDtypeStruct((nb, n1, n2 * fw), BF16)
    blk = pl.BlockSpec((1, n1, tc), lambda b, j: (b, 0, j))
    tw_blk = pl.BlockSpec((nt2, n1, LANES), lambda b, j: (j, 0, 0))
    ar, ai = pl.pallas_call(
        functools.partial(_fft_a_kernel, fw=fw),
        grid=(nb, n2 // nt2),
        in_specs=[pl.BlockSpec((2 * n1, 2 * n1), lambda b, j: (0, 0)), blk, blk, tw_blk, tw_blk],
        out_specs=[blk, blk],
        out_shape=[a_shape, a_shape],
        compiler_params=_params(("arbitrary", "arbitrary"), 40),
        name="fft_a",
    )(m_a, p3, q3, twc, tws)

    in_blk = pl.BlockSpec((1, 1, n2, fw), lambda b, k: (b, k, 0, 0))
    out = pl.pallas_call(
        functools.partial(_fft_c_kernel, scale=1.0 / math.sqrt(seq * gd)),
        grid=(nb, n1),
        in_specs=[pl.BlockSpec((n2, 2 * n2), lambda b, k: (0, 0)), in_blk, in_blk],
        out_specs=pl.BlockSpec((1, n2, fw), lambda b, k: (b, 0, k)),
        out_shape=jax.ShapeDtypeStruct((nb, n2, n1 * fw), BF16),
        compiler_params=_params(("arbitrary", "arbitrary"), 40),
        name="fft_c",
    )(cs_c, ar.reshape(nb, n1, n2, fw), ai.reshape(nb, n1, n2, fw))
    return out.reshape(nb * seq, fw)


def _merge_kernel(a_ref, f_ref, wa_ref, wf_ref, ga_ref, gf_ref, o_ref):
    a_br = jnp.dot(a_ref[...], wa_ref[...], preferred_element_type=F32)
    f_br = jnp.dot(f_ref[...], wf_ref[...], preferred_element_type=F32)
    o_ref[...] = (ga_ref[...].astype(F32) * a_br + gf_ref[...].astype(F32) * f_br).astype(BF16)


def _merge(att, four, wa, wf, proj, gate_off):
    t, aw = att.shape
    fw = four.shape[1]
    d = wa.shape[1]
    tm, tn = 512, 512
    ga0 = gate_off // tn
    gf0 = (gate_off + d) // tn
    return pl.pallas_call(
        _merge_kernel,
        grid=(t // tm, d // tn),
        in_specs=[pl.BlockSpec((tm, aw), lambda i, j: (i, 0)),
                  pl.BlockSpec((tm, fw), lambda i, j: (i, 0)),
                  pl.BlockSpec((aw, tn), lambda i, j: (0, j)),
                  pl.BlockSpec((fw, tn), lambda i, j: (0, j)),
                  pl.BlockSpec((tm, tn), lambda i, j: (i, ga0 + j)),
                  pl.BlockSpec((tm, tn), lambda i, j: (i, gf0 + j))],
        out_specs=pl.BlockSpec((tm, tn), lambda i, j: (i, j)),
        out_shape=jax.ShapeDtypeStruct((t, d), BF16),
        compiler_params=_params(("arbitrary", "arbitrary"), 48),
        name="merge",
    )(att, four, wa, wf, proj, proj)


def _outproj_kernel(m_ref, w_ref, x_ref, g_ref, o_ref):
    acc = jnp.dot(m_ref[...], w_ref[...], preferred_element_type=F32)
    o_ref[...] = x_ref[...] + g_ref[0] * acc


def _outproj(merged, w_out, x2d, g1, seq):
    t, d = merged.shape
    tm, tn = 512, 1024
    per_b = seq // tm
    return pl.pallas_call(
        _outproj_kernel,
        grid=(t // tm, d // tn),
        in_specs=[pl.BlockSpec((tm, d), lambda i, j: (i, 0)),
                  pl.BlockSpec((d, tn), lambda i, j: (0, j)),
                  pl.BlockSpec((tm, tn), lambda i, j: (i, j)),
                  pl.BlockSpec((1, 1, tn), lambda i, j: (i // per_b, 0, j))],
        out_specs=pl.BlockSpec((tm, tn), lambda i, j: (i, j)),
        out_shape=jax.ShapeDtypeStruct((t, d), F32),
        compiler_params=_params(("arbitrary", "arbitrary"), 48),
        name="outproj",
    )(merged, w_out, x2d, g1)


def _top_values(s, k, store):
    n = s.shape[0]
    iota = lax.broadcasted_iota(jnp.int32, s.shape, 0)
    cur = s
    for r in range(k):
        m = jnp.max(cur, axis=0, keepdims=True)
        store(r, m)
        first = jnp.min(jnp.where(cur == m, iota, n), axis=0, keepdims=True)
        cur = jnp.where(iota == first, NEG_INF, cur)


def _peer_score_kernel(h_ref, wq_ref, keys_ref, g0_ref, phi_ref, e1_ref, top_scr, best_scr):
    k1 = PEER_TOPK + 1
    qt = jnp.dot(wq_ref[...], h_ref[...], preferred_element_type=F32).astype(BF16)
    top_scr[...] = jnp.full(top_scr.shape, NEG_INF, F32)
    s = []
    for p in range(2):
        s.append(jnp.dot(keys_ref[0, p], qt[p * PEER_HALF:(p + 1) * PEER_HALF], preferred_element_type=F32))

        def store(r, row, p=p):
            top_scr[p, r:r + 1, :] = row
        _top_values(s[p], k1, store)

    cand = jnp.concatenate([top_scr[0, 0:1, :] + top_scr[1]]
                           + [top_scr[0, i:i + 1, :] + top_scr[1, 0:8, :] for i in range(1, k1)], axis=0)

    def store_best(r, row):
        best_scr[r:r + 1, :] = row
    _top_values(cand, k1, store_best)

    best = best_scr[0:PEER_TOPK]
    z = jnp.sum(jnp.exp(best - best[0:1]), axis=0, keepdims=True)
    tau = 0.5 * (best_scr[PEER_TOPK - 1:PEER_TOPK] + best_scr[PEER_TOPK:k1])
    max0 = top_scr[0, 0:1, :]
    max1 = top_scr[1, 0:1, :]
    g0_ref[0] = jnp.exp(s[0] - max0) / z
    phi_ref[0] = jnp.exp((tau - max1) - s[0])
    e1_ref[0] = jnp.exp(s[1] - max1)


def _peer_scores(h2t, wq_t, keys):
    d, t = h2t.shape
    tm = 512
    top_rows = 24
    s_shape = jax.ShapeDtypeStruct((PEER_HEADS, PEER_NKEYS, t), F32)
    s_blk = pl.BlockSpec((1, PEER_NKEYS, tm), lambda i, h: (h, 0, i))
    return pl.pallas_call(
        _peer_score_kernel,
        grid=(t // tm, PEER_HEADS),
        in_specs=[pl.BlockSpec((d, tm), lambda i, h: (0, i)),
                  pl.BlockSpec((2 * PEER_HALF, d), lambda i, h: (h, 0)),
                  pl.BlockSpec((1, 2, PEER_NKEYS, PEER_HALF), lambda i, h: (h, 0, 0, 0))],
        out_specs=[s_blk, s_blk, s_blk],
        out_shape=[s_shape, s_shape, s_shape],
        scratch_shapes=[pltpu.VMEM((2, top_rows, tm), F32), pltpu.VMEM((top_rows, tm), F32)],
        compiler_params=_params(("arbitrary", "arbitrary"), 40),
        name="peer_scores",
    )(h2t, wq_t, keys)


def _peer_kernel(h_ref, u_ref, v_ref, g0_ref, phi_ref, e1_ref, o_ref, st_scr, w_scr, *, n_tiles):
    j = pl.program_id(1)
    te = u_ref.shape[0]
    rows = te // PEER_NKEYS

    @pl.when(j == 0)
    def _():
        o_ref[...] = jnp.zeros(o_ref.shape, F32)
        st_scr[1] = jnp.zeros(st_scr.shape[1:], F32)
        w_scr[0] = jnp.zeros(w_scr.shape[1:], BF16)

    gate_tile = jnp.clip(j - 1, 0, n_tiles - 1)

    d = o_ref.shape[1]
    chunk = d // PEER_TRIPS
    rows_per_trip = rows // PEER_TRIPS

    def step(par):
        st_scr[par] = jnp.zeros(st_scr.shape[1:], F32)

        def body(trip, carry):
            cs = pl.multiple_of(trip * chunk, chunk)
            for q in range(rows_per_trip):
                r = trip * rows_per_trip + q
                rs = pl.multiple_of(r * PEER_NKEYS, PEER_NKEYS)
                st = st_scr[1 - par, pl.ds(rs, PEER_NKEYS), :]
                act = 0.5 * st * (1.0 + lax.erf(st * math.sqrt(0.5)))
                n0 = gate_tile * rows + r
                gate = None
                for hd in range(PEER_HEADS):
                    e1 = e1_ref[hd]
                    g = jnp.where(e1 >= phi_ref[hd, pl.ds(n0, 1), :], e1 * g0_ref[hd, pl.ds(n0, 1), :], 0.0)
                    gate = g if gate is None else gate + g
                w_scr[1 - par, :, pl.ds(rs, PEER_NKEYS)] = (act * gate).T.astype(BF16)
            st_scr[par] += jnp.dot(u_ref[:, pl.ds(cs, chunk)], h_ref[pl.ds(cs, chunk), :],
                                   preferred_element_type=F32)
            o_ref[:, pl.ds(cs, chunk)] += jnp.dot(w_scr[par], v_ref[:, pl.ds(cs, chunk)],
                                                  preferred_element_type=F32)
            return carry

        lax.fori_loop(0, PEER_TRIPS, body, 0)

    @pl.when(j % 2 == 0)
    def _():
        step(0)

    @pl.when(j % 2 == 1)
    def _():
        step(1)


def _peer(h2t, u_bf, v_bf, g0, phi, e1):
    d, t = h2t.shape
    n_exp = u_bf.shape[0]
    tm, te = 512, 512
    n_tiles = n_exp // te
    once = pl.Buffered(1)
    sc_blk = pl.BlockSpec((PEER_HEADS, PEER_NKEYS, tm), lambda i, j: (0, 0, i), pipeline_mode=once)
    return pl.pallas_call(
        functools.partial(_peer_kernel, n_tiles=n_tiles),
        grid=(t // tm, n_tiles + 2),
        in_specs=[pl.BlockSpec((d, tm), lambda i, j: (0, i), pipeline_mode=once),
                  pl.BlockSpec((te, d), lambda i, j: (jnp.minimum(j, n_tiles - 1), 0)),
                  pl.BlockSpec((te, d), lambda i, j: (jnp.clip(j - 2, 0, n_tiles - 1), 0)),
                  sc_blk, sc_blk, sc_blk],
        out_specs=pl.BlockSpec((tm, d), lambda i, j: (i, 0), pipeline_mode=once),
        out_shape=jax.ShapeDtypeStruct((t, d), F32),
        scratch_shapes=[pltpu.VMEM((2, te, tm), F32), pltpu.VMEM((2, tm, te), BF16)],
        compiler_params=_params(("arbitrary", "arbitrary"), 56),
        name="peer",
    )(h2t, u_bf, v_bf, g0, phi, e1)


def _final_kernel(x_ref, p_ref, g2_ref, fg_ref, o_ref):
    x = x_ref[...] + g2_ref[0] * p_ref[...]
    ms = jnp.mean(x * x, axis=-1, keepdims=True)
    o_ref[...] = x * lax.rsqrt(ms + EPS) * fg_ref[...]


def _final(x1, peer, g2, final_g, seq):
    t, d = x1.shape
    tm = 256
    per_b = seq // tm
    row = pl.BlockSpec((tm, d), lambda i: (i, 0))
    return pl.pallas_call(
        _final_kernel,
        grid=(t // tm,),
        in_specs=[row, row,
                  pl.BlockSpec((1, 1, d), lambda i: (i // per_b, 0, 0)),
                  pl.BlockSpec((1, d), lambda i: (0, 0))],
        out_specs=row,
        out_shape=jax.ShapeDtypeStruct((t, d), F32),
        compiler_params=_params(("arbitrary",), 48),
        name="final",
    )(x1, peer, g2, final_g.reshape(1, d))


def _rope_tables(seq):
    rows = seq // GRID_W
    row_ids = jnp.repeat(jnp.arange(rows), GRID_W).astype(F32)
    col_ids = jnp.tile(jnp.arange(GRID_W), rows).astype(F32)
    axis_dim = HEAD_DIM // 2
    inv_freq = ROPE_THETA ** (-jnp.arange(0, axis_dim, 2, dtype=F32) / axis_dim)
    ang = jnp.concatenate([row_ids[:, None] * inv_freq, col_ids[:, None] * inv_freq], axis=-1)
    cos, sin = jnp.cos(ang), jnp.sin(ang)
    return jnp.concatenate([cos, cos], axis=-1), jnp.concatenate([-sin, sin], axis=-1)


def _trunk(x, mod, wts, dims):
    nb, seq, d = x.shape
    att_w, kv_w, f_w, gd = dims
    x2d = x.reshape(nb * seq, d)
    sh1, sc1, g1, sh2, sc2, g2 = [mod[:, k].reshape(nb, 1, d) for k in range(6)]
    cc, ss = _rope_tables(seq)

    h = _ln_mod(x2d, wts["norm1_g"], sc1, sh1, seq, transpose=False)
    proj = _inproj(h, wts["w_main"], wts["qg"], wts["kg"], cc, ss, seq, att_w, kv_w)
    p, q = _fproj(h, wts["w_f"], wts["cs_ch"], gd)
    att = _attention(proj, nb, seq, att_w, kv_w)
    four = _seq_dft(p, q, nb, seq, gd)
    merged = _merge(att, four, wts["wa"], wts["wf"], proj, att_w + 2 * kv_w)
    x1 = _outproj(merged, wts["w_out"], x2d, g1, seq)

    h2t = _ln_mod(x1, wts["norm2_g"], sc2, sh2, seq, transpose=True)
    g0, phi, e1 = _peer_scores(h2t, wts["wq_t"], wts["keys"])
    peer = _peer(h2t, wts["u"], wts["v"], g0, phi, e1)
    y = _final(x1, peer, g2, wts["final_g"], seq)
    return y.reshape(nb, seq, d)


def kernel(x_prompt, x_sample, c_prompt, c_sample, w_ada, b_ada, norm1_g, norm2_g, w_in, q_norm_g, k_norm_g, w_attn_br, w_four_br, w_out, w_peer_q, peer_keys, peer_u, peer_v, final_g):
    d = x_prompt.shape[-1]
    assert w_ada.shape[0] == 1, "single-layer trunk"
    att_w = w_attn_br.shape[1]
    f_w = w_four_br.shape[1]
    kv_w = (w_in.shape[2] - att_w - f_w - 2 * d) // 2
    gd = f_w // N_FGROUPS
    dims = (att_w, kv_w, f_w, gd)

    perm = np.concatenate([np.arange(0, HEAD_DIM, 2), np.arange(1, HEAD_DIM, 2)])
    w = w_in[0]

    def deinterleave(cols):
        return cols.reshape(d, -1, HEAD_DIM)[:, :, perm].reshape(d, -1)

    s_q, s_k, s_v, s_f = att_w, att_w + kv_w, att_w + 2 * kv_w, att_w + 2 * kv_w + f_w
    w_main = jnp.concatenate([deinterleave(w[:, :s_q]), deinterleave(w[:, s_q:s_k]), w[:, s_k:s_v],
                              w[:, s_f:]], axis=1).astype(BF16)
    cch, sch = _dft_tables(gd)
    wts = {
        "norm1_g": norm1_g[0], "norm2_g": norm2_g[0], "final_g": final_g,
        "w_main": w_main,
        "w_f": w[:, s_v:s_f].astype(BF16),
        "cs_ch": jnp.asarray(np.concatenate([cch, sch], axis=1), BF16),
        "qg": q_norm_g[0][perm].reshape(1, HEAD_DIM),
        "kg": k_norm_g[0][perm].reshape(1, HEAD_DIM),
        "wa": w_attn_br[0].astype(BF16), "wf": w_four_br[0].astype(BF16),
        "w_out": w_out[0].astype(BF16),
        "wq_t": w_peer_q[0].T.astype(BF16),
        "keys": peer_keys[0].astype(BF16),
        "u": peer_u[0].astype(BF16), "v": peer_v[0].astype(BF16),
    }

    nbp, nbs = c_prompt.shape[0], c_sample.shape[0]
    c_all = jnp.concatenate([c_prompt, c_sample], axis=0)
    c_pad = jnp.pad(c_all, ((0, (-c_all.shape[0]) % 8), (0, 0)))
    mod = _modulation(c_pad, w_ada[0], b_ada[0]).reshape(c_pad.shape[0], 6, d)

    y_prompt = _trunk(x_prompt, mod[:nbp], wts, dims)
    y_sample = _trunk(x_sample, mod[nbp:nbp + nbs], wts, dims)
    return (y_prompt, y_sample)
```

```python
import functools
import math

import jax
import jax.numpy as jnp
import numpy as np
from jax import lax
from jax.experimental import pallas as pl
from jax.experimental.pallas import tpu as pltpu

F32 = jnp.float32
BF16 = jnp.bfloat16

HEAD_DIM = 128
GQA_GROUP = 4
Q_PER_TILE = GQA_GROUP * HEAD_DIM
ROPE_THETA = 10000.0
GRID_W = 64
N_FGROUPS = 8
PEER_HEADS = 8
PEER_NKEYS = 128
PEER_HALF = 128
PEER_TOPK = 16
EPS = 1e-6
LANES = 128
NEG_INF = float("-inf")
ONES_ROWS = 16
LOG2E = math.log2(math.e)
MIB = 1024 * 1024


def _params(sem, vmem_mib):
    return pltpu.CompilerParams(dimension_semantics=sem, vmem_limit_bytes=vmem_mib * MIB)


def _mod_kernel(c_ref, w_ref, b_ref, o_ref):
    c = c_ref[...]
    a = c * jax.nn.sigmoid(c)
    o_ref[...] = jnp.dot(a, w_ref[...], preferred_element_type=F32,
                         precision=lax.Precision.HIGHEST) + b_ref[...]


def _modulation(c_pad, w_ada, b_ada):
    rows, d = c_pad.shape
    n = w_ada.shape[1]
    tn = 512
    return pl.pallas_call(
        _mod_kernel,
        grid=(n // tn,),
        in_specs=[pl.BlockSpec((rows, d), lambda j: (0, 0)),
                  pl.BlockSpec((d, tn), lambda j: (0, j)),
                  pl.BlockSpec((1, tn), lambda j: (0, j))],
        out_specs=pl.BlockSpec((rows, tn), lambda j: (0, j)),
        out_shape=jax.ShapeDtypeStruct((rows, n), F32),
        compiler_params=_params(("arbitrary",), 40),
        name="mod",
    )(c_pad, w_ada, b_ada.reshape(1, n))


def _ln_kernel(x_ref, g_ref, sc_ref, sh_ref, o_ref, *, transpose, chunk):
    x = x_ref[...]
    ms = jnp.mean(x * x, axis=-1, keepdims=True)
    y = x * lax.rsqrt(ms + EPS) * g_ref[...]
    h = y * (1.0 + sc_ref[0]) + sh_ref[0]
    if transpose:
        d = h.shape[1]
        for c in range(d // chunk):
            o_ref[c * chunk:(c + 1) * chunk, :] = h[:, c * chunk:(c + 1) * chunk].T.astype(BF16)
    else:
        o_ref[...] = h.astype(BF16)


def _ln_mod(x2d, g, sc, sh, seq, *, transpose):
    t, d = x2d.shape
    tm = 256
    per_b = seq // tm
    if transpose:
        out_spec = pl.BlockSpec((d, tm), lambda i: (0, i))
        out_shape = jax.ShapeDtypeStruct((d, t), BF16)
    else:
        out_spec = pl.BlockSpec((tm, d), lambda i: (i, 0))
        out_shape = jax.ShapeDtypeStruct((t, d), BF16)
    return pl.pallas_call(
        functools.partial(_ln_kernel, transpose=transpose, chunk=min(512, d)),
        grid=(t // tm,),
        in_specs=[pl.BlockSpec((tm, d), lambda i: (i, 0)),
                  pl.BlockSpec((1, d), lambda i: (0, 0)),
                  pl.BlockSpec((1, 1, d), lambda i: (i // per_b, 0, 0)),
                  pl.BlockSpec((1, 1, d), lambda i: (i // per_b, 0, 0))],
        out_specs=out_spec,
        out_shape=out_shape,
        compiler_params=_params(("arbitrary",), 40),
        name="ln_t" if transpose else "ln",
    )(x2d, g.reshape(1, d), sc, sh)


def _norm_rope(acc, g, cc, ss, scale, o_ref):
    for hd in range(acc.shape[1] // HEAD_DIM):
        xh = acc[:, hd * HEAD_DIM:(hd + 1) * HEAD_DIM]
        ms = jnp.mean(xh * xh, axis=-1, keepdims=True)
        y = xh * lax.rsqrt(ms + EPS) * g
        y = y * cc + pltpu.roll(y, HEAD_DIM // 2, axis=1) * ss
        if scale != 1.0:
            y = y * scale
        o_ref[:, hd * HEAD_DIM:(hd + 1) * HEAD_DIM] = y.astype(BF16)


def _inproj_kernel(h_ref, w_ref, qg_ref, kg_ref, cc_ref, ss_ref, o_ref, *, nq, nk, nv, q_scale):
    j = pl.program_id(1)
    acc = jnp.dot(h_ref[...], w_ref[...], preferred_element_type=F32)

    @pl.when(j < nq)
    def _():
        _norm_rope(acc, qg_ref[...], cc_ref[...], ss_ref[...], q_scale, o_ref)

    @pl.when(jnp.logical_and(j >= nq, j < nq + nk))
    def _():
        _norm_rope(acc, kg_ref[...], cc_ref[...], ss_ref[...], 1.0, o_ref)

    @pl.when(jnp.logical_and(j >= nq + nk, j < nq + nk + nv))
    def _():
        o_ref[...] = acc.astype(BF16)

    @pl.when(j >= nq + nk + nv)
    def _():
        o_ref[...] = jax.nn.sigmoid(acc).astype(BF16)


def _inproj(h, w_main, qg, kg, cc, ss, seq, att_w, kv_w):
    t, d = h.shape
    wtot = w_main.shape[1]
    tm = 512
    tn = min(1024, kv_w)
    per_b = seq // tm
    kern = functools.partial(_inproj_kernel, nq=att_w // tn, nk=kv_w // tn, nv=kv_w // tn,
                             q_scale=HEAD_DIM ** -0.5 * LOG2E)
    return pl.pallas_call(
        kern,
        grid=(t // tm, wtot // tn),
        in_specs=[pl.BlockSpec((tm, d), lambda i, j: (i, 0)),
                  pl.BlockSpec((d, tn), lambda i, j: (0, j)),
                  pl.BlockSpec((1, HEAD_DIM), lambda i, j: (0, 0)),
                  pl.BlockSpec((1, HEAD_DIM), lambda i, j: (0, 0)),
                  pl.BlockSpec((tm, HEAD_DIM), lambda i, j: (i % per_b, 0)),
                  pl.BlockSpec((tm, HEAD_DIM), lambda i, j: (i % per_b, 0))],
        out_specs=pl.BlockSpec((tm, tn), lambda i, j: (i, j)),
        out_shape=jax.ShapeDtypeStruct((t, wtot), BF16),
        compiler_params=_params(("arbitrary", "arbitrary"), 48),
        name="inproj",
    )(h, w_main, qg, kg, cc, ss)


def _fproj_kernel(h_ref, w_ref, cs_ref, p_ref, q_ref, *, gd):
    acc = jnp.dot(h_ref[...], w_ref[...], preferred_element_type=F32)
    for g in range(acc.shape[1] // gd):
        u = acc[:, g * gd:(g + 1) * gd].astype(BF16)
        pq = jnp.dot(u, cs_ref[...], preferred_element_type=F32)
        p_ref[:, g * gd:(g + 1) * gd] = pq[:, :gd].astype(BF16)
        q_ref[:, g * gd:(g + 1) * gd] = pq[:, gd:].astype(BF16)


def _fproj(h, w_f, cs, gd):
    t, d = h.shape
    fw = w_f.shape[1]
    tm = 512
    tn = min(1024, fw)
    out = jax.ShapeDtypeStruct((t, fw), BF16)
    return pl.pallas_call(
        functools.partial(_fproj_kernel, gd=gd),
        grid=(t // tm, fw // tn),
        in_specs=[pl.BlockSpec((tm, d), lambda i, j: (i, 0)),
                  pl.BlockSpec((d, tn), lambda i, j: (0, j)),
                  pl.BlockSpec((gd, 2 * gd), lambda i, j: (0, 0))],
        out_specs=[pl.BlockSpec((tm, tn), lambda i, j: (i, j)),
                   pl.BlockSpec((tm, tn), lambda i, j: (i, j))],
        out_shape=[out, out],
        compiler_params=_params(("arbitrary", "arbitrary"), 48),
        name="fproj",
    )(h, w_f, cs)


def _attn_kernel(q_ref, k_ref, v_ref, o_ref, m_scr, acc_scr, qt_scr, sa_scr, sb_scr, *, tk):
    tq = q_ref.shape[0]
    seq = k_ref.shape[0]
    n_chunks = seq // tk
    for g in range(GQA_GROUP):
        qt_scr[:, g * tq:(g + 1) * tq] = q_ref[:, g * HEAD_DIM:(g + 1) * HEAD_DIM].astype(F32).T.astype(BF16)
    m_scr[...] = jnp.full(m_scr.shape, NEG_INF, F32)
    acc_scr[...] = jnp.zeros(acc_scr.shape, F32)
    ones = jnp.ones((ONES_ROWS, tk), BF16)

    def scores(c, st_ref):
        start = pl.multiple_of(c * tk, tk)
        st_ref[...] = jnp.dot(k_ref[pl.ds(start, tk), :], qt_scr[...], preferred_element_type=F32)

    def consume(c, st_ref):
        start = pl.multiple_of(c * tk, tk)
        v = v_ref[pl.ds(start, tk), :]
        st = st_ref[...]
        m_prev = m_scr[...]
        m_next = jnp.maximum(m_prev, jnp.max(st, axis=0, keepdims=True))
        p = jnp.exp2(st - m_next).astype(BF16)
        alpha = jnp.exp2(m_prev - m_next)
        vt = jnp.concatenate([v.astype(F32).T.astype(BF16), ones], axis=0)
        acc_scr[...] = alpha * acc_scr[...] + jnp.dot(vt, p, preferred_element_type=F32)
        m_scr[...] = m_next

    scores(0, sa_scr)

    def body(i, carry):
        c = 2 * i
        scores(c + 1, sb_scr)
        consume(c, sa_scr)
        scores(c + 2, sa_scr)
        consume(c + 1, sb_scr)
        return carry

    lax.fori_loop(0, n_chunks // 2 - 1, body, 0)
    scores(n_chunks - 1, sb_scr)
    consume(n_chunks - 2, sa_scr)
    consume(n_chunks - 1, sb_scr)
    acc = acc_scr[...]
    out_t = acc[:HEAD_DIM] / acc[HEAD_DIM:HEAD_DIM + 1]
    for g in range(GQA_GROUP):
        o_ref[:, g * HEAD_DIM:(g + 1) * HEAD_DIM] = out_t[:, g * tq:(g + 1) * tq].T.astype(BF16)


def _attention(proj, nb, seq, att_w, kv_w):
    t = proj.shape[0]
    tq = 256
    tk = min(512, seq)
    n_kv = kv_w // HEAD_DIM
    per_b = seq // tq
    k_off = att_w // HEAD_DIM
    v_off = (att_w + kv_w) // HEAD_DIM
    cols = GQA_GROUP * tq
    return pl.pallas_call(
        functools.partial(_attn_kernel, tk=tk),
        grid=(nb, n_kv, per_b),
        in_specs=[pl.BlockSpec((tq, Q_PER_TILE), lambda b, g, i: (b * per_b + i, g)),
                  pl.BlockSpec((seq, HEAD_DIM), lambda b, g, i: (b, k_off + g)),
                  pl.BlockSpec((seq, HEAD_DIM), lambda b, g, i: (b, v_off + g))],
        out_specs=pl.BlockSpec((tq, Q_PER_TILE), lambda b, g, i: (b * per_b + i, g)),
        out_shape=jax.ShapeDtypeStruct((t, att_w), BF16),
        scratch_shapes=[pltpu.VMEM((1, cols), F32),
                        pltpu.VMEM((HEAD_DIM + ONES_ROWS, cols), F32),
                        pltpu.VMEM((HEAD_DIM, cols), BF16),
                        pltpu.VMEM((tk, cols), F32),
                        pltpu.VMEM((tk, cols), F32)],
        compiler_params=_params(("arbitrary", "arbitrary", "arbitrary"), 48),
        name="attn",
    )(proj, proj, proj)


def _fft_a_kernel(m_ref, p_ref, q_ref, twc_ref, tws_ref, ar_ref, ai_ref, *, fw):
    n1 = p_ref.shape[1]
    rhs = jnp.concatenate([p_ref[0], q_ref[0]], axis=0)
    a = jnp.dot(m_ref[...], rhs, preferred_element_type=F32)
    rep = fw // LANES
    for u in range(p_ref.shape[2] // fw):
        ar = a[:n1, u * fw:(u + 1) * fw]
        ai = a[n1:, u * fw:(u + 1) * fw]
        c = jnp.tile(twc_ref[u], (1, rep))
        s = jnp.tile(tws_ref[u], (1, rep))
        ar_ref[0, :, u * fw:(u + 1) * fw] = (ar * c + ai * s).astype(BF16)
        ai_ref[0, :, u * fw:(u + 1) * fw] = (ai * c - ar * s).astype(BF16)


def _fft_c_kernel(cs_ref, ar_ref, ai_ref, o_ref, *, scale):
    rhs = jnp.concatenate([ar_ref[0, 0], ai_ref[0, 0]], axis=0)
    o_ref[0] = (jnp.dot(cs_ref[...], rhs, preferred_element_type=F32) * scale).astype(BF16)


def _dft_tables(n):
    idx = (np.arange(n)[:, None] * np.arange(n)[None, :]) % n
    ang = 2.0 * np.pi * idx / n
    return np.cos(ang), np.sin(ang)


def _seq_dft(p, q, nb, seq, gd):
    fw = p.shape[1]
    n1 = 64 if seq >= 4096 else 16
    n2 = seq // n1
    c1, s1 = _dft_tables(n1)
    c2, s2 = _dft_tables(n2)
    m_a = jnp.asarray(np.block([[c1, -s1], [-s1, -c1]]), BF16)
    cs_c = jnp.asarray(np.concatenate([c2, s2], axis=1), BF16)
    tw = 2.0 * np.pi * (np.arange(n2)[:, None] * np.arange(n1)[None, :]) / seq
    twc = jnp.asarray(np.broadcast_to(np.cos(tw)[:, :, None], (n2, n1, LANES)), F32)
    tws = jnp.asarray(np.broadcast_to(np.sin(tw)[:, :, None], (n2, n1, LANES)), F32)

    nt2 = 4
    tc = nt2 * fw
    p3 = p.reshape(nb, n1, n2 * fw)
    q3 = q.reshape(nb, n1, n2 * fw)
    a_shape = jax.ShapeDtypeStruct((nb, n1, n2 * fw), BF16)
    blk = pl.BlockSpec((1, n1, tc), lambda b, j: (b, 0, j))
    tw_blk = pl.BlockSpec((nt2, n1, LANES), lambda b, j: (j, 0, 0))
    ar, ai = pl.pallas_call(
        functools.partial(_fft_a_kernel, fw=fw),
        grid=(nb, n2 // nt2),
        in_specs=[pl.BlockSpec((2 * n1, 2 * n1), lambda b, j: (0, 0)), blk, blk, tw_blk, tw_blk],
        out_specs=[blk, blk],
        out_shape=[a_shape, a_shape],
        compiler_params=_params(("arbitrary", "arbitrary"), 40),
        name="fft_a",
    )(m_a, p3, q3, twc, tws)

    in_blk = pl.BlockSpec((1, 1, n2, fw), lambda b, k: (b, k, 0, 0))
    out = pl.pallas_call(
        functools.partial(_fft_c_kernel, scale=1.0 / math.sqrt(seq * gd)),
        grid=(nb, n1),
        in_specs=[pl.BlockSpec((n2, 2 * n2), lambda b, k: (0, 0)), in_blk, in_blk],
        out_specs=pl.BlockSpec((1, n2, fw), lambda b, k: (b, 0, k)),
        out_shape=jax.ShapeDtypeStruct((nb, n2, n1 * fw), BF16),
        compiler_params=_params(("arbitrary", "arbitrary"), 40),
        name="fft_c",
    )(cs_c, ar.reshape(nb, n1, n2, fw), ai.reshape(nb, n1, n2, fw))
    return out.reshape(nb * seq, fw)


def _merge_kernel(a_ref, f_ref, wa_ref, wf_ref, ga_ref, gf_ref, o_ref):
    a_br = jnp.dot(a_ref[...], wa_ref[...], preferred_element_type=F32)
    f_br = jnp.dot(f_ref[...], wf_ref[...], preferred_element_type=F32)
    o_ref[...] = (ga_ref[...].astype(F32) * a_br + gf_ref[...].astype(F32) * f_br).astype(BF16)


def _merge(att, four, wa, wf, proj, gate_off):
    t, aw = att.shape
    fw = four.shape[1]
    d = wa.shape[1]
    tm, tn = 512, 512
    ga0 = gate_off // tn
    gf0 = (gate_off + d) // tn
    return pl.pallas_call(
        _merge_kernel,
        grid=(t // tm, d // tn),
        in_specs=[pl.BlockSpec((tm, aw), lambda i, j: (i, 0)),
                  pl.BlockSpec((tm, fw), lambda i, j: (i, 0)),
                  pl.BlockSpec((aw, tn), lambda i, j: (0, j)),
                  pl.BlockSpec((fw, tn), lambda i, j: (0, j)),
                  pl.BlockSpec((tm, tn), lambda i, j: (i, ga0 + j)),
                  pl.BlockSpec((tm, tn), lambda i, j: (i, gf0 + j))],
        out_specs=pl.BlockSpec((tm, tn), lambda i, j: (i, j)),
        out_shape=jax.ShapeDtypeStruct((t, d), BF16),
        compiler_params=_params(("arbitrary", "arbitrary"), 48),
        name="merge",
    )(att, four, wa, wf, proj, proj)


def _outproj_kernel(m_ref, w_ref, x_ref, g_ref, o_ref):
    acc = jnp.dot(m_ref[...], w_ref[...], preferred_element_type=F32)
    o_ref[...] = x_ref[...] + g_ref[0] * acc


def _outproj(merged, w_out, x2d, g1, seq):
    t, d = merged.shape
    tm, tn = 512, 1024
    per_b = seq // tm
    return pl.pallas_call(
        _outproj_kernel,
        grid=(t // tm, d // tn),
        in_specs=[pl.BlockSpec((tm, d), lambda i, j: (i, 0)),
                  pl.BlockSpec((d, tn), lambda i, j: (0, j)),
                  pl.BlockSpec((tm, tn), lambda i, j: (i, j)),
                  pl.BlockSpec((1, 1, tn), lambda i, j: (i // per_b, 0, j))],
        out_specs=pl.BlockSpec((tm, tn), lambda i, j: (i, j)),
        out_shape=jax.ShapeDtypeStruct((t, d), F32),
        compiler_params=_params(("arbitrary", "arbitrary"), 48),
        name="outproj",
    )(merged, w_out, x2d, g1)


def _top_values(s, k, store):
    n = s.shape[0]
    iota = lax.broadcasted_iota(jnp.int32, s.shape, 0)
    cur = s
    for r in range(k):
        m = jnp.max(cur, axis=0, keepdims=True)
        store(r, m)
        first = jnp.min(jnp.where(cur == m, iota, n), axis=0, keepdims=True)
        cur = jnp.where(iota == first, NEG_INF, cur)


def _peer_score_kernel(h_ref, wq_ref, keys_ref, g0_ref, phi_ref, e1_ref, top_scr, best_scr):
    k1 = PEER_TOPK + 1
    qt = jnp.dot(wq_ref[...], h_ref[...], preferred_element_type=F32).astype(BF16)
    top_scr[...] = jnp.full(top_scr.shape, NEG_INF, F32)
    s = []
    for p in range(2):
        s.append(jnp.dot(keys_ref[0, p], qt[p * PEER_HALF:(p + 1) * PEER_HALF], preferred_element_type=F32))

        def store(r, row, p=p):
            top_scr[p, r:r + 1, :] = row
        _top_values(s[p], k1, store)

    cand = jnp.concatenate([top_scr[0, 0:1, :] + top_scr[1]]
                           + [top_scr[0, i:i + 1, :] + top_scr[1, 0:8, :] for i in range(1, k1)], axis=0)

    def store_best(r, row):
        best_scr[r:r + 1, :] = row
    _top_values(cand, k1, store_best)

    best = best_scr[0:PEER_TOPK]
    z = jnp.sum(jnp.exp(best - best[0:1]), axis=0, keepdims=True)
    tau = 0.5 * (best_scr[PEER_TOPK - 1:PEER_TOPK] + best_scr[PEER_TOPK:k1])
    max0 = top_scr[0, 0:1, :]
    max1 = top_scr[1, 0:1, :]
    g0_ref[0] = jnp.exp(s[0] - max0) / z
    phi_ref[0] = jnp.exp((tau - max1) - s[0])
    e1_ref[0] = jnp.exp(s[1] - max1)


def _peer_scores(h2t, wq_t, keys):
    d, t = h2t.shape
    tm = 512
    top_rows = 24
    s_shape = jax.ShapeDtypeStruct((PEER_HEADS, PEER_NKEYS, t), F32)
    s_blk = pl.BlockSpec((1, PEER_NKEYS, tm), lambda i, h: (h, 0, i))
    return pl.pallas_call(
        _peer_score_kernel,
        grid=(t // tm, PEER_HEADS),
        in_specs=[pl.BlockSpec((d, tm), lambda i, h: (0, i)),
                  pl.BlockSpec((2 * PEER_HALF, d), lambda i, h: (h, 0)),
                  pl.BlockSpec((1, 2, PEER_NKEYS, PEER_HALF), lambda i, h: (h, 0, 0, 0))],
        out_specs=[s_blk, s_blk, s_blk],
        out_shape=[s_shape, s_shape, s_shape],
        scratch_shapes=[pltpu.VMEM((2, top_rows, tm), F32), pltpu.VMEM((top_rows, tm), F32)],
        compiler_params=_params(("arbitrary", "arbitrary"), 40),
        name="peer_scores",
    )(h2t, wq_t, keys)


def _peer_w_kernel(h_ref, u_ref, g0_ref, phi_ref, e1_ref, w_ref, st_scr, *, n_tiles):
    j = pl.program_id(1)
    tm, te = w_ref.shape
    rows = te // PEER_NKEYS

    @pl.when(j == 0)
    def _():
        st_scr[1] = jnp.zeros(st_scr.shape[1:], F32)

    gate_tile = jnp.clip(j - 1, 0, n_tiles - 1)

    half = tm // 2

    def step(par):
        def body(a, carry):
            ts = pl.multiple_of(a * half, half)
            u = pltpu.bitcast(u_ref[...], BF16)
            st_scr[par, :, pl.ds(ts, half)] = jnp.dot(u, h_ref[:, pl.ds(ts, half)],
                                                      preferred_element_type=F32)
            for r in range(rows):
                n0 = gate_tile * rows + r
                ks = slice(r * PEER_NKEYS, (r + 1) * PEER_NKEYS)
                st = st_scr[1 - par, ks, pl.ds(ts, half)]
                act = 0.5 * st * (1.0 + lax.erf(st * math.sqrt(0.5)))
                gate = None
                for hd in range(PEER_HEADS):
                    e1 = e1_ref[hd, :, pl.ds(ts, half)]
                    g = jnp.where(e1 >= phi_ref[hd, pl.ds(n0, 1), pl.ds(ts, half)],
                                  e1 * g0_ref[hd, pl.ds(n0, 1), pl.ds(ts, half)], 0.0)
                    gate = g if gate is None else gate + g
                w_ref[pl.ds(ts, half), ks] = (act * gate).T.astype(BF16)
            return carry

        lax.fori_loop(0, 2, body, 0)

    @pl.when(j % 2 == 0)
    def _():
        step(0)

    @pl.when(j % 2 == 1)
    def _():
        step(1)


def _peer_weights(h2t, u_pairs, g0, phi, e1):
    d, t = h2t.shape
    n_exp = 2 * u_pairs.shape[0]
    tm, te = 512, 512
    n_tiles = n_exp // te
    once = pl.Buffered(1)
    sc_blk = pl.BlockSpec((PEER_HEADS, PEER_NKEYS, tm), lambda i, j: (0, 0, i), pipeline_mode=once)
    return pl.pallas_call(
        functools.partial(_peer_w_kernel, n_tiles=n_tiles),
        grid=(t // tm, n_tiles + 1),
        in_specs=[pl.BlockSpec((d, tm), lambda i, j: (0, i), pipeline_mode=once),
                  pl.BlockSpec((te // 2, d), lambda i, j: (jnp.minimum(j, n_tiles - 1), 0)),
                  sc_blk, sc_blk, sc_blk],
        out_specs=pl.BlockSpec((tm, te), lambda i, j: (i, jnp.maximum(j - 1, 0))),
        out_shape=jax.ShapeDtypeStruct((t, n_exp), BF16),
        scratch_shapes=[pltpu.VMEM((2, te, tm), F32)],
        compiler_params=_params(("arbitrary", "arbitrary"), 48),
        name="peer_w",
    )(h2t, u_pairs, g0, phi, e1)


def _peer_v_kernel(w_ref, v_ref, o_ref):
    o_ref[...] = jnp.dot(w_ref[...], v_ref[...], preferred_element_type=F32)


def _peer_values(w, v_bf):
    t, n_exp = w.shape
    d = v_bf.shape[1]
    tm, tn = 512, min(256, d)
    return pl.pallas_call(
        _peer_v_kernel,
        grid=(t // tm, d // tn),
        in_specs=[pl.BlockSpec((tm, n_exp), lambda i, j: (i, 0), pipeline_mode=pl.Buffered(1)),
                  pl.BlockSpec((n_exp, tn), lambda i, j: (0, j))],
        out_specs=pl.BlockSpec((tm, tn), lambda i, j: (i, j)),
        out_shape=jax.ShapeDtypeStruct((t, d), F32),
        compiler_params=_params(("arbitrary", "arbitrary"), 48),
        name="peer_v",
    )(w, v_bf)


def _final_kernel(x_ref, p_ref, g2_ref, fg_ref, o_ref):
    x = x_ref[...] + g2_ref[0] * p_ref[...]
    ms = jnp.mean(x * x, axis=-1, keepdims=True)
    o_ref[...] = x * lax.rsqrt(ms + EPS) * fg_ref[...]


def _final(x1, peer, g2, final_g, seq):
    t, d = x1.shape
    tm = 256
    per_b = seq // tm
    row = pl.BlockSpec((tm, d), lambda i: (i, 0))
    return pl.pallas_call(
        _final_kernel,
        grid=(t // tm,),
        in_specs=[row, row,
                  pl.BlockSpec((1, 1, d), lambda i: (i // per_b, 0, 0)),
                  pl.BlockSpec((1, d), lambda i: (0, 0))],
        out_specs=row,
        out_shape=jax.ShapeDtypeStruct((t, d), F32),
        compiler_params=_params(("arbitrary",), 48),
        name="final",
    )(x1, peer, g2, final_g.reshape(1, d))


def _rope_tables(seq):
    rows = seq // GRID_W
    row_ids = jnp.repeat(jnp.arange(rows), GRID_W).astype(F32)
    col_ids = jnp.tile(jnp.arange(GRID_W), rows).astype(F32)
    axis_dim = HEAD_DIM // 2
    inv_freq = ROPE_THETA ** (-jnp.arange(0, axis_dim, 2, dtype=F32) / axis_dim)
    ang = jnp.concatenate([row_ids[:, None] * inv_freq, col_ids[:, None] * inv_freq], axis=-1)
    cos, sin = jnp.cos(ang), jnp.sin(ang)
    return jnp.concatenate([cos, cos], axis=-1), jnp.concatenate([-sin, sin], axis=-1)


def _pack_row_pairs(a):
    n, d = a.shape
    return lax.bitcast_convert_type(a.reshape(n // 2, 2, d).transpose(0, 2, 1), jnp.uint32)


def _trunk(x, mod, wts, dims):
    nb, seq, d = x.shape
    att_w, kv_w, f_w, gd = dims
    x2d = x.reshape(nb * seq, d)
    sh1, sc1, g1, sh2, sc2, g2 = [mod[:, k].reshape(nb, 1, d) for k in range(6)]
    cc, ss = _rope_tables(seq)

    h = _ln_mod(x2d, wts["norm1_g"], sc1, sh1, seq, transpose=False)
    proj = _inproj(h, wts["w_main"], wts["qg"], wts["kg"], cc, ss, seq, att_w, kv_w)
    p, q = _fproj(h, wts["w_f"], wts["cs_ch"], gd)
    att = _attention(proj, nb, seq, att_w, kv_w)
    four = _seq_dft(p, q, nb, seq, gd)
    merged = _merge(att, four, wts["wa"], wts["wf"], proj, att_w + 2 * kv_w)
    x1 = _outproj(merged, wts["w_out"], x2d, g1, seq)

    h2t = _ln_mod(x1, wts["norm2_g"], sc2, sh2, seq, transpose=True)
    g0, phi, e1 = _peer_scores(h2t, wts["wq_t"], wts["keys"])
    peer = _peer_values(_peer_weights(h2t, wts["u"], g0, phi, e1), wts["v"])
    y = _final(x1, peer, g2, wts["final_g"], seq)
    return y.reshape(nb, seq, d)


def kernel(x_prompt, x_sample, c_prompt, c_sample, w_ada, b_ada, norm1_g, norm2_g, w_in, q_norm_g, k_norm_g, w_attn_br, w_four_br, w_out, w_peer_q, peer_keys, peer_u, peer_v, final_g):
    d = x_prompt.shape[-1]
    assert w_ada.shape[0] == 1, "single-layer trunk"
    att_w = w_attn_br.shape[1]
    f_w = w_four_br.shape[1]
    kv_w = (w_in.shape[2] - att_w - f_w - 2 * d) // 2
    gd = f_w // N_FGROUPS
    dims = (att_w, kv_w, f_w, gd)

    perm = np.concatenate([np.arange(0, HEAD_DIM, 2), np.arange(1, HEAD_DIM, 2)])
    w = w_in[0]

    def deinterleave(cols):
        return cols.reshape(d, -1, HEAD_DIM)[:, :, perm].reshape(d, -1)

    s_q, s_k, s_v, s_f = att_w, att_w + kv_w, att_w + 2 * kv_w, att_w + 2 * kv_w + f_w
    w_main = jnp.concatenate([deinterleave(w[:, :s_q]), deinterleave(w[:, s_q:s_k]), w[:, s_k:s_v],
                              w[:, s_f:]], axis=1).astype(BF16)
    cch, sch = _dft_tables(gd)
    wts = {
        "norm1_g": norm1_g[0], "norm2_g": norm2_g[0], "final_g": final_g,
        "w_main": w_main,
        "w_f": w[:, s_v:s_f].astype(BF16),
        "cs_ch": jnp.asarray(np.concatenate([cch, sch], axis=1), BF16),
        "qg": q_norm_g[0][perm].reshape(1, HEAD_DIM),
        "kg": k_norm_g[0][perm].reshape(1, HEAD_DIM),
        "wa": w_attn_br[0].astype(BF16), "wf": w_four_br[0].astype(BF16),
        "w_out": w_out[0].astype(BF16),
        "wq_t": w_peer_q[0].T.astype(BF16),
        "keys": peer_keys[0].astype(BF16),
        "u": _pack_row_pairs(peer_u[0].astype(BF16)), "v": peer_v[0].astype(BF16),
    }

    nbp, nbs = c_prompt.shape[0], c_sample.shape[0]
    c_all = jnp.concatenate([c_prompt, c_sample], axis=0)
    c_pad = jnp.pad(c_all, ((0, (-c_all.shape[0]) % 8), (0, 0)))
    mod = _modulation(c_pad, w_ada[0], b_ada[0]).reshape(c_pad.shape[0], 6, d)

    y_prompt = _trunk(x_prompt, mod[:nbp], wts, dims)
    y_sample = _trunk(x_sample, mod[nbp:nbp + nbs], wts, dims)
    return (y_prompt, y_sample)
```

```python
import functools
import math

import jax
import jax.numpy as jnp
import numpy as np
from jax import lax
from jax.experimental import pallas as pl
from jax.experimental.pallas import tpu as pltpu

F32 = jnp.float32
BF16 = jnp.bfloat16
FP8 = jnp.float8_e4m3fn
FP8_TARGET = 256.0
FP8_TINY = 1e-30

HEAD_DIM = 128
GQA_GROUP = 4
Q_PER_TILE = GQA_GROUP * HEAD_DIM
ROPE_THETA = 10000.0
GRID_W = 64
N_FGROUPS = 8
PEER_HEADS = 8
PEER_NKEYS = 128
PEER_HALF = 128
PEER_TOPK = 16
EPS = 1e-6
LANES = 128
NEG_INF = float("-inf")
ONES_ROWS = 16
LOG2E = math.log2(math.e)
MIB = 1024 * 1024


def _params(sem, vmem_mib):
    return pltpu.CompilerParams(dimension_semantics=sem, vmem_limit_bytes=vmem_mib * MIB)


def _mod_kernel(c_ref, w_ref, b_ref, o_ref):
    c = c_ref[...]
    a = c * jax.nn.sigmoid(c)
    o_ref[...] = jnp.dot(a, w_ref[...], preferred_element_type=F32,
                         precision=lax.Precision.HIGHEST) + b_ref[...]


def _modulation(c_pad, w_ada, b_ada):
    rows, d = c_pad.shape
    n = w_ada.shape[1]
    tn = 512
    return pl.pallas_call(
        _mod_kernel,
        grid=(n // tn,),
        in_specs=[pl.BlockSpec((rows, d), lambda j: (0, 0)),
                  pl.BlockSpec((d, tn), lambda j: (0, j)),
                  pl.BlockSpec((1, tn), lambda j: (0, j))],
        out_specs=pl.BlockSpec((rows, tn), lambda j: (0, j)),
        out_shape=jax.ShapeDtypeStruct((rows, n), F32),
        compiler_params=_params(("arbitrary",), 40),
        name="mod",
    )(c_pad, w_ada, b_ada.reshape(1, n))


def _ln_rows(x_ref, g_ref, sc_ref, sh_ref):
    x = x_ref[...]
    ms = jnp.mean(x * x, axis=-1, keepdims=True)
    y = x * lax.rsqrt(ms + EPS) * g_ref[...]
    return y * (1.0 + sc_ref[0]) + sh_ref[0]


def _ln_kernel(x_ref, g_ref, sc_ref, sh_ref, o_ref):
    o_ref[...] = _ln_rows(x_ref, g_ref, sc_ref, sh_ref).astype(BF16)


def _ln_t_kernel(x_ref, g_ref, sc_ref, sh_ref, o_ref, q_ref, s_ref, ht_scr, *, chunk):
    h = _ln_rows(x_ref, g_ref, sc_ref, sh_ref)
    amax = jnp.full(s_ref.shape, FP8_TINY, F32)
    for c in range(h.shape[1] // chunk):
        rows = slice(c * chunk, (c + 1) * chunk)
        blk = h[:, rows].T
        ht_scr[rows, :] = blk
        o_ref[rows, :] = blk.astype(BF16)
        amax = jnp.maximum(amax, jnp.max(jnp.abs(blk), axis=0, keepdims=True))
    q_ref[...] = (ht_scr[...] * (FP8_TARGET / amax)).astype(FP8)
    s_ref[...] = amax * (1.0 / FP8_TARGET)


def _ln_mod(x2d, g, sc, sh, seq, *, transpose):
    t, d = x2d.shape
    tm = 256
    per_b = seq // tm
    in_specs = [pl.BlockSpec((tm, d), lambda i: (i, 0)),
                pl.BlockSpec((1, d), lambda i: (0, 0)),
                pl.BlockSpec((1, 1, d), lambda i: (i // per_b, 0, 0)),
                pl.BlockSpec((1, 1, d), lambda i: (i // per_b, 0, 0))]
    args = (x2d, g.reshape(1, d), sc, sh)
    if not transpose:
        return pl.pallas_call(
            _ln_kernel, grid=(t // tm,), in_specs=in_specs,
            out_specs=pl.BlockSpec((tm, d), lambda i: (i, 0)),
            out_shape=jax.ShapeDtypeStruct((t, d), BF16),
            compiler_params=_params(("arbitrary",), 40), name="ln",
        )(*args)
    col_blk = pl.BlockSpec((d, tm), lambda i: (0, i))
    return pl.pallas_call(
        functools.partial(_ln_t_kernel, chunk=min(512, d)),
        grid=(t // tm,), in_specs=in_specs,
        out_specs=[col_blk, col_blk, pl.BlockSpec((1, tm), lambda i: (0, i))],
        out_shape=[jax.ShapeDtypeStruct((d, t), BF16), jax.ShapeDtypeStruct((d, t), FP8),
                   jax.ShapeDtypeStruct((1, t), F32)],
        scratch_shapes=[pltpu.VMEM((d, tm), F32)],
        compiler_params=_params(("arbitrary",), 40), name="ln_t",
    )(*args)


def _norm_rope(acc, g, cc, ss, scale, o_ref):
    for hd in range(acc.shape[1] // HEAD_DIM):
        xh = acc[:, hd * HEAD_DIM:(hd + 1) * HEAD_DIM]
        ms = jnp.mean(xh * xh, axis=-1, keepdims=True)
        y = xh * lax.rsqrt(ms + EPS) * g
        y = y * cc + pltpu.roll(y, HEAD_DIM // 2, axis=1) * ss
        if scale != 1.0:
            y = y * scale
        o_ref[:, hd * HEAD_DIM:(hd + 1) * HEAD_DIM] = y.astype(BF16)


def _inproj_kernel(h_ref, w_ref, qg_ref, kg_ref, cc_ref, ss_ref, o_ref, *, nq, nk, nv, q_scale):
    j = pl.program_id(1)
    acc = jnp.dot(h_ref[...], w_ref[...], preferred_element_type=F32)

    @pl.when(j < nq)
    def _():
        _norm_rope(acc, qg_ref[...], cc_ref[...], ss_ref[...], q_scale, o_ref)

    @pl.when(jnp.logical_and(j >= nq, j < nq + nk))
    def _():
        _norm_rope(acc, kg_ref[...], cc_ref[...], ss_ref[...], 1.0, o_ref)

    @pl.when(jnp.logical_and(j >= nq + nk, j < nq + nk + nv))
    def _():
        o_ref[...] = acc.astype(BF16)

    @pl.when(j >= nq + nk + nv)
    def _():
        o_ref[...] = jax.nn.sigmoid(acc).astype(BF16)


def _inproj(h, w_main, qg, kg, cc, ss, seq, att_w, kv_w):
    t, d = h.shape
    wtot = w_main.shape[1]
    tm = 512
    tn = min(1024, kv_w)
    per_b = seq // tm
    kern = functools.partial(_inproj_kernel, nq=att_w // tn, nk=kv_w // tn, nv=kv_w // tn,
                             q_scale=HEAD_DIM ** -0.5 * LOG2E)
    return pl.pallas_call(
        kern,
        grid=(t // tm, wtot // tn),
        in_specs=[pl.BlockSpec((tm, d), lambda i, j: (i, 0)),
                  pl.BlockSpec((d, tn), lambda i, j: (0, j)),
                  pl.BlockSpec((1, HEAD_DIM), lambda i, j: (0, 0)),
                  pl.BlockSpec((1, HEAD_DIM), lambda i, j: (0, 0)),
                  pl.BlockSpec((tm, HEAD_DIM), lambda i, j: (i % per_b, 0)),
                  pl.BlockSpec((tm, HEAD_DIM), lambda i, j: (i % per_b, 0))],
        out_specs=pl.BlockSpec((tm, tn), lambda i, j: (i, j)),
        out_shape=jax.ShapeDtypeStruct((t, wtot), BF16),
        compiler_params=_params(("arbitrary", "arbitrary"), 48),
        name="inproj",
    )(h, w_main, qg, kg, cc, ss)


def _fproj_kernel(h_ref, w_ref, cs_ref, p_ref, q_ref, *, gd):
    acc = jnp.dot(h_ref[...], w_ref[...], preferred_element_type=F32)
    for g in range(acc.shape[1] // gd):
        u = acc[:, g * gd:(g + 1) * gd].astype(BF16)
        pq = jnp.dot(u, cs_ref[...], preferred_element_type=F32)
        p_ref[:, g * gd:(g + 1) * gd] = pq[:, :gd].astype(BF16)
        q_ref[:, g * gd:(g + 1) * gd] = pq[:, gd:].astype(BF16)


def _fproj(h, w_f, cs, gd):
    t, d = h.shape
    fw = w_f.shape[1]
    tm = 512
    tn = min(1024, fw)
    out = jax.ShapeDtypeStruct((t, fw), BF16)
    return pl.pallas_call(
        functools.partial(_fproj_kernel, gd=gd),
        grid=(t // tm, fw // tn),
        in_specs=[pl.BlockSpec((tm, d), lambda i, j: (i, 0)),
                  pl.BlockSpec((d, tn), lambda i, j: (0, j)),
                  pl.BlockSpec((gd, 2 * gd), lambda i, j: (0, 0))],
        out_specs=[pl.BlockSpec((tm, tn), lambda i, j: (i, j)),
                   pl.BlockSpec((tm, tn), lambda i, j: (i, j))],
        out_shape=[out, out],
        compiler_params=_params(("arbitrary", "arbitrary"), 48),
        name="fproj",
    )(h, w_f, cs)


def _attn_kernel(q_ref, k_ref, v_ref, o_ref, m_scr, acc_scr, qt_scr, sa_scr, sb_scr, *, tk):
    tq = q_ref.shape[0]
    seq = k_ref.shape[0]
    n_chunks = seq // tk
    for g in range(GQA_GROUP):
        qt_scr[:, g * tq:(g + 1) * tq] = q_ref[:, g * HEAD_DIM:(g + 1) * HEAD_DIM].astype(F32).T.astype(BF16)
    m_scr[...] = jnp.full(m_scr.shape, NEG_INF, F32)
    acc_scr[...] = jnp.zeros(acc_scr.shape, F32)
    ones = jnp.ones((ONES_ROWS, tk), BF16)

    def scores(c, st_ref):
        start = pl.multiple_of(c * tk, tk)
        st_ref[...] = jnp.dot(k_ref[pl.ds(start, tk), :], qt_scr[...], preferred_element_type=F32)

    def consume(c, st_ref):
        start = pl.multiple_of(c * tk, tk)
        v = v_ref[pl.ds(start, tk), :]
        st = st_ref[...]
        m_prev = m_scr[...]
        m_next = jnp.maximum(m_prev, jnp.max(st, axis=0, keepdims=True))
        p = jnp.exp2(st - m_next).astype(BF16)
        alpha = jnp.exp2(m_prev - m_next)
        vt = jnp.concatenate([v.astype(F32).T.astype(BF16), ones], axis=0)
        acc_scr[...] = alpha * acc_scr[...] + jnp.dot(vt, p, preferred_element_type=F32)
        m_scr[...] = m_next

    scores(0, sa_scr)

    def body(i, carry):
        c = 2 * i
        scores(c + 1, sb_scr)
        consume(c, sa_scr)
        scores(c + 2, sa_scr)
        consume(c + 1, sb_scr)
        return carry

    lax.fori_loop(0, n_chunks // 2 - 1, body, 0)
    scores(n_chunks - 1, sb_scr)
    consume(n_chunks - 2, sa_scr)
    consume(n_chunks - 1, sb_scr)
    acc = acc_scr[...]
    out_t = acc[:HEAD_DIM] / acc[HEAD_DIM:HEAD_DIM + 1]
    for g in range(GQA_GROUP):
        o_ref[:, g * HEAD_DIM:(g + 1) * HEAD_DIM] = out_t[:, g * tq:(g + 1) * tq].T.astype(BF16)


def _attention(proj, nb, seq, att_w, kv_w):
    t = proj.shape[0]
    tq = 256
    tk = min(512, seq)
    n_kv = kv_w // HEAD_DIM
    per_b = seq // tq
    k_off = att_w // HEAD_DIM
    v_off = (att_w + kv_w) // HEAD_DIM
    cols = GQA_GROUP * tq
    return pl.pallas_call(
        functools.partial(_attn_kernel, tk=tk),
        grid=(nb, n_kv, per_b),
        in_specs=[pl.BlockSpec((tq, Q_PER_TILE), lambda b, g, i: (b * per_b + i, g)),
                  pl.BlockSpec((seq, HEAD_DIM), lambda b, g, i: (b, k_off + g)),
                  pl.BlockSpec((seq, HEAD_DIM), lambda b, g, i: (b, v_off + g))],
        out_specs=pl.BlockSpec((tq, Q_PER_TILE), lambda b, g, i: (b * per_b + i, g)),
        out_shape=jax.ShapeDtypeStruct((t, att_w), BF16),
        scratch_shapes=[pltpu.VMEM((1, cols), F32),
                        pltpu.VMEM((HEAD_DIM + ONES_ROWS, cols), F32),
                        pltpu.VMEM((HEAD_DIM, cols), BF16),
                        pltpu.VMEM((tk, cols), F32),
                        pltpu.VMEM((tk, cols), F32)],
        compiler_params=_params(("arbitrary", "arbitrary", "arbitrary"), 48),
        name="attn",
    )(proj, proj, proj)


def _fft_a_kernel(m_ref, p_ref, q_ref, twc_ref, tws_ref, ar_ref, ai_ref, *, fw):
    n1 = p_ref.shape[1]
    rhs = jnp.concatenate([p_ref[0], q_ref[0]], axis=0)
    a = jnp.dot(m_ref[...], rhs, preferred_element_type=F32)
    rep = fw // LANES
    for u in range(p_ref.shape[2] // fw):
        ar = a[:n1, u * fw:(u + 1) * fw]
        ai = a[n1:, u * fw:(u + 1) * fw]
        c = jnp.tile(twc_ref[u], (1, rep))
        s = jnp.tile(tws_ref[u], (1, rep))
        ar_ref[0, :, u * fw:(u + 1) * fw] = (ar * c + ai * s).astype(BF16)
        ai_ref[0, :, u * fw:(u + 1) * fw] = (ai * c - ar * s).astype(BF16)


def _fft_c_kernel(cs_ref, ar_ref, ai_ref, o_ref, *, scale):
    rhs = jnp.concatenate([ar_ref[0, 0], ai_ref[0, 0]], axis=0)
    o_ref[0] = (jnp.dot(cs_ref[...], rhs, preferred_element_type=F32) * scale).astype(BF16)


def _dft_tables(n):
    idx = (np.arange(n)[:, None] * np.arange(n)[None, :]) % n
    ang = 2.0 * np.pi * idx / n
    return np.cos(ang), np.sin(ang)


def _seq_dft(p, q, nb, seq, gd):
    fw = p.shape[1]
    n1 = 64 if seq >= 4096 else 16
    n2 = seq // n1
    c1, s1 = _dft_tables(n1)
    c2, s2 = _dft_tables(n2)
    m_a = jnp.asarray(np.block([[c1, -s1], [-s1, -c1]]), BF16)
    cs_c = jnp.asarray(np.concatenate([c2, s2], axis=1), BF16)
    tw = 2.0 * np.pi * (np.arange(n2)[:, None] * np.arange(n1)[None, :]) / seq
    twc = jnp.asarray(np.broadcast_to(np.cos(tw)[:, :, None], (n2, n1, LANES)), F32)
    tws = jnp.asarray(np.broadcast_to(np.sin(tw)[:, :, None], (n2, n1, LANES)), F32)

    nt2 = 4
    tc = nt2 * fw
    p3 = p.reshape(nb, n1, n2 * fw)
    q3 = q.reshape(nb, n1, n2 * fw)
    a_shape = jax.ShapeDtypeStruct((nb, n1, n2 * fw), BF16)
    blk = pl.BlockSpec((1, n1, tc), lambda b, j: (b, 0, j))
    tw_blk = pl.BlockSpec((nt2, n1, LANES), lambda b, j: (j, 0, 0))
    ar, ai = pl.pallas_call(
        functools.partial(_fft_a_kernel, fw=fw),
        grid=(nb, n2 // nt2),
        in_specs=[pl.BlockSpec((2 * n1, 2 * n1), lambda b, j: (0, 0)), blk, blk, tw_blk, tw_blk],
        out_specs=[blk, blk],
        out_shape=[a_shape, a_shape],
        compiler_params=_params(("arbitrary", "arbitrary"), 40),
        name="fft_a",
    )(m_a, p3, q3, twc, tws)

    in_blk = pl.BlockSpec((1, 1, n2, fw), lambda b, k: (b, k, 0, 0))
    out = pl.pallas_call(
        functools.partial(_fft_c_kernel, scale=1.0 / math.sqrt(seq * gd)),
        grid=(nb, n1),
        in_specs=[pl.BlockSpec((n2, 2 * n2), lambda b, k: (0, 0)), in_blk, in_blk],
        out_specs=pl.BlockSpec((1, n2, fw), lambda b, k: (b, 0, k)),
        out_shape=jax.ShapeDtypeStruct((nb, n2, n1 * fw), BF16),
        compiler_params=_params(("arbitrary", "arbitrary"), 40),
        name="fft_c",
    )(cs_c, ar.reshape(nb, n1, n2, fw), ai.reshape(nb, n1, n2, fw))
    return out.reshape(nb * seq, fw)


def _merge_kernel(a_ref, f_ref, wa_ref, wf_ref, ga_ref, gf_ref, o_ref):
    a_br = jnp.dot(a_ref[...], wa_ref[...], preferred_element_type=F32)
    f_br = jnp.dot(f_ref[...], wf_ref[...], preferred_element_type=F32)
    o_ref[...] = (ga_ref[...].astype(F32) * a_br + gf_ref[...].astype(F32) * f_br).astype(BF16)


def _merge(att, four, wa, wf, proj, gate_off):
    t, aw = att.shape
    fw = four.shape[1]
    d = wa.shape[1]
    tm, tn = 512, 512
    ga0 = gate_off // tn
    gf0 = (gate_off + d) // tn
    return pl.pallas_call(
        _merge_kernel,
        grid=(t // tm, d // tn),
        in_specs=[pl.BlockSpec((tm, aw), lambda i, j: (i, 0)),
                  pl.BlockSpec((tm, fw), lambda i, j: (i, 0)),
                  pl.BlockSpec((aw, tn), lambda i, j: (0, j)),
                  pl.BlockSpec((fw, tn), lambda i, j: (0, j)),
                  pl.BlockSpec((tm, tn), lambda i, j: (i, ga0 + j)),
                  pl.BlockSpec((tm, tn), lambda i, j: (i, gf0 + j))],
        out_specs=pl.BlockSpec((tm, tn), lambda i, j: (i, j)),
        out_shape=jax.ShapeDtypeStruct((t, d), BF16),
        compiler_params=_params(("arbitrary", "arbitrary"), 48),
        name="merge",
    )(att, four, wa, wf, proj, proj)


def _outproj_kernel(m_ref, w_ref, x_ref, g_ref, o_ref):
    acc = jnp.dot(m_ref[...], w_ref[...], preferred_element_type=F32)
    o_ref[...] = x_ref[...] + g_ref[0] * acc


def _outproj(merged, w_out, x2d, g1, seq):
    t, d = merged.shape
    tm, tn = 512, 1024
    per_b = seq // tm
    return pl.pallas_call(
        _outproj_kernel,
        grid=(t // tm, d // tn),
        in_specs=[pl.BlockSpec((tm, d), lambda i, j: (i, 0)),
                  pl.BlockSpec((d, tn), lambda i, j: (0, j)),
                  pl.BlockSpec((tm, tn), lambda i, j: (i, j)),
                  pl.BlockSpec((1, 1, tn), lambda i, j: (i // per_b, 0, j))],
        out_specs=pl.BlockSpec((tm, tn), lambda i, j: (i, j)),
        out_shape=jax.ShapeDtypeStruct((t, d), F32),
        compiler_params=_params(("arbitrary", "arbitrary"), 48),
        name="outproj",
    )(merged, w_out, x2d, g1)


def _top_values(s, k, store):
    n = s.shape[0]
    iota = lax.broadcasted_iota(jnp.int32, s.shape, 0)
    cur = s
    for r in range(k):
        m = jnp.max(cur, axis=0, keepdims=True)
        store(r, m)
        first = jnp.min(jnp.where(cur == m, iota, n), axis=0, keepdims=True)
        cur = jnp.where(iota == first, NEG_INF, cur)


def _peer_score_kernel(h_ref, wq_ref, keys_ref, g0_ref, phi_ref, e1_ref, top_scr, best_scr):
    k1 = PEER_TOPK + 1
    qt = jnp.dot(wq_ref[...], h_ref[...], preferred_element_type=F32).astype(BF16)
    top_scr[...] = jnp.full(top_scr.shape, NEG_INF, F32)
    s = []
    for p in range(2):
        s.append(jnp.dot(keys_ref[0, p], qt[p * PEER_HALF:(p + 1) * PEER_HALF], preferred_element_type=F32))

        def store(r, row, p=p):
            top_scr[p, r:r + 1, :] = row
        _top_values(s[p], k1, store)

    cand = jnp.concatenate([top_scr[0, 0:1, :] + top_scr[1]]
                           + [top_scr[0, i:i + 1, :] + top_scr[1, 0:8, :] for i in range(1, k1)], axis=0)

    def store_best(r, row):
        best_scr[r:r + 1, :] = row
    _top_values(cand, k1, store_best)

    best = best_scr[0:PEER_TOPK]
    z = jnp.sum(jnp.exp(best - best[0:1]), axis=0, keepdims=True)
    tau = 0.5 * (best_scr[PEER_TOPK - 1:PEER_TOPK] + best_scr[PEER_TOPK:k1])
    max0 = top_scr[0, 0:1, :]
    max1 = top_scr[1, 0:1, :]
    g0_ref[0] = jnp.exp(s[0] - max0) / z
    phi_ref[0] = jnp.exp((tau - max1) - s[0])
    e1_ref[0] = jnp.exp(s[1] - max1)


def _peer_scores(h2t, wq_t, keys):
    d, t = h2t.shape
    tm = 512
    top_rows = 24
    s_shape = jax.ShapeDtypeStruct((PEER_HEADS, PEER_NKEYS, t), F32)
    s_blk = pl.BlockSpec((1, PEER_NKEYS, tm), lambda i, h: (h, 0, i))
    return pl.pallas_call(
        _peer_score_kernel,
        grid=(t // tm, PEER_HEADS),
        in_specs=[pl.BlockSpec((d, tm), lambda i, h: (0, i)),
                  pl.BlockSpec((2 * PEER_HALF, d), lambda i, h: (h, 0)),
                  pl.BlockSpec((1, 2, PEER_NKEYS, PEER_HALF), lambda i, h: (h, 0, 0, 0))],
        out_specs=[s_blk, s_blk, s_blk],
        out_shape=[s_shape, s_shape, s_shape],
        scratch_shapes=[pltpu.VMEM((2, top_rows, tm), F32), pltpu.VMEM((top_rows, tm), F32)],
        compiler_params=_params(("arbitrary", "arbitrary"), 40),
        name="peer_scores",
    )(h2t, wq_t, keys)


def _peer_w_kernel(h_ref, hs_ref, u_ref, g0_ref, phi_ref, e1_ref, w_ref, st_scr, *, n_tiles):
    j = pl.program_id(1)
    tm, te = w_ref.shape
    rows = te // PEER_NKEYS

    @pl.when(j == 0)
    def _():
        st_scr[1] = jnp.zeros(st_scr.shape[1:], F32)

    gate_tile = jnp.clip(j - 1, 0, n_tiles - 1)

    half = tm // 2

    def step(par):
        def body(a, carry):
            ts = pl.multiple_of(a * half, half)
            st_scr[par, :, pl.ds(ts, half)] = jnp.dot(u_ref[...], h_ref[:, pl.ds(ts, half)],
                                                      preferred_element_type=F32)
            inv_scale = hs_ref[:, pl.ds(ts, half)]
            for r in range(rows):
                n0 = gate_tile * rows + r
                ks = slice(r * PEER_NKEYS, (r + 1) * PEER_NKEYS)
                st = st_scr[1 - par, ks, pl.ds(ts, half)] * inv_scale
                act = 0.5 * st * (1.0 + lax.erf(st * math.sqrt(0.5)))
                gate = None
                for hd in range(PEER_HEADS):
                    e1 = e1_ref[hd, :, pl.ds(ts, half)]
                    g = jnp.where(e1 >= phi_ref[hd, pl.ds(n0, 1), pl.ds(ts, half)],
                                  e1 * g0_ref[hd, pl.ds(n0, 1), pl.ds(ts, half)], 0.0)
                    gate = g if gate is None else gate + g
                w_ref[pl.ds(ts, half), ks] = (act * gate).T.astype(BF16)
            return carry

        lax.fori_loop(0, 2, body, 0)

    @pl.when(j % 2 == 0)
    def _():
        step(0)

    @pl.when(j % 2 == 1)
    def _():
        step(1)


def _peer_weights(h2t, hs, u_q, g0, phi, e1):
    d, t = h2t.shape
    n_exp = u_q.shape[0]
    tm, te = 512, 512
    n_tiles = n_exp // te
    once = pl.Buffered(1)
    sc_blk = pl.BlockSpec((PEER_HEADS, PEER_NKEYS, tm), lambda i, j: (0, 0, i), pipeline_mode=once)
    return pl.pallas_call(
        functools.partial(_peer_w_kernel, n_tiles=n_tiles),
        grid=(t // tm, n_tiles + 1),
        in_specs=[pl.BlockSpec((d, tm), lambda i, j: (0, i), pipeline_mode=once),
                  pl.BlockSpec((1, tm), lambda i, j: (0, i)),
                  pl.BlockSpec((te, d), lambda i, j: (jnp.minimum(j, n_tiles - 1), 0)),
                  sc_blk, sc_blk, sc_blk],
        out_specs=pl.BlockSpec((tm, te), lambda i, j: (i, jnp.maximum(j - 1, 0))),
        out_shape=jax.ShapeDtypeStruct((t, n_exp), BF16),
        scratch_shapes=[pltpu.VMEM((2, te, tm), F32)],
        compiler_params=_params(("arbitrary", "arbitrary"), 48),
        name="peer_w",
    )(h2t, hs, u_q, g0, phi, e1)


def _peer_v_kernel(w_ref, v_ref, o_ref):
    o_ref[...] = jnp.dot(w_ref[...], v_ref[...], preferred_element_type=F32)


def _peer_values(w, v_bf):
    t, n_exp = w.shape
    d = v_bf.shape[1]
    tm, tn = 512, min(256, d)
    return pl.pallas_call(
        _peer_v_kernel,
        grid=(t // tm, d // tn),
        in_specs=[pl.BlockSpec((tm, n_exp), lambda i, j: (i, 0), pipeline_mode=pl.Buffered(1)),
                  pl.BlockSpec((n_exp, tn), lambda i, j: (0, j))],
        out_specs=pl.BlockSpec((tm, tn), lambda i, j: (i, j)),
        out_shape=jax.ShapeDtypeStruct((t, d), F32),
        compiler_params=_params(("arbitrary", "arbitrary"), 48),
        name="peer_v",
    )(w, v_bf)


def _final_kernel(x_ref, p_ref, g2_ref, fg_ref, o_ref):
    x = x_ref[...] + g2_ref[0] * p_ref[...]
    ms = jnp.mean(x * x, axis=-1, keepdims=True)
    o_ref[...] = x * lax.rsqrt(ms + EPS) * fg_ref[...]


def _final(x1, peer, g2, final_g, seq):
    t, d = x1.shape
    tm = 256
    per_b = seq // tm
    row = pl.BlockSpec((tm, d), lambda i: (i, 0))
    return pl.pallas_call(
        _final_kernel,
        grid=(t // tm,),
        in_specs=[row, row,
                  pl.BlockSpec((1, 1, d), lambda i: (i // per_b, 0, 0)),
                  pl.BlockSpec((1, d), lambda i: (0, 0))],
        out_specs=row,
        out_shape=jax.ShapeDtypeStruct((t, d), F32),
        compiler_params=_params(("arbitrary",), 48),
        name="final",
    )(x1, peer, g2, final_g.reshape(1, d))


def _rope_tables(seq):
    rows = seq // GRID_W
    row_ids = jnp.repeat(jnp.arange(rows), GRID_W).astype(F32)
    col_ids = jnp.tile(jnp.arange(GRID_W), rows).astype(F32)
    axis_dim = HEAD_DIM // 2
    inv_freq = ROPE_THETA ** (-jnp.arange(0, axis_dim, 2, dtype=F32) / axis_dim)
    ang = jnp.concatenate([row_ids[:, None] * inv_freq, col_ids[:, None] * inv_freq], axis=-1)
    cos, sin = jnp.cos(ang), jnp.sin(ang)
    return jnp.concatenate([cos, cos], axis=-1), jnp.concatenate([-sin, sin], axis=-1)


def _trunk(x, mod, wts, dims):
    nb, seq, d = x.shape
    att_w, kv_w, f_w, gd = dims
    x2d = x.reshape(nb * seq, d)
    sh1, sc1, g1, sh2, sc2, g2 = [mod[:, k].reshape(nb, 1, d) for k in range(6)]
    cc, ss = _rope_tables(seq)

    h = _ln_mod(x2d, wts["norm1_g"], sc1, sh1, seq, transpose=False)
    proj = _inproj(h, wts["w_main"], wts["qg"], wts["kg"], cc, ss, seq, att_w, kv_w)
    p, q = _fproj(h, wts["w_f"], wts["cs_ch"], gd)
    att = _attention(proj, nb, seq, att_w, kv_w)
    four = _seq_dft(p, q, nb, seq, gd)
    merged = _merge(att, four, wts["wa"], wts["wf"], proj, att_w + 2 * kv_w)
    x1 = _outproj(merged, wts["w_out"], x2d, g1, seq)

    h2t, h2t_q, h2_inv = _ln_mod(x1, wts["norm2_g"], sc2, sh2, seq, transpose=True)
    g0, phi, e1 = _peer_scores(h2t, wts["wq_t"], wts["keys"])
    w = _peer_weights(h2t_q, h2_inv * wts["u_inv"], wts["u_q"], g0, phi, e1)
    peer = _peer_values(w, wts["v"])
    y = _final(x1, peer, g2, wts["final_g"], seq)
    return y.reshape(nb, seq, d)


def kernel(x_prompt, x_sample, c_prompt, c_sample, w_ada, b_ada, norm1_g, norm2_g, w_in, q_norm_g, k_norm_g, w_attn_br, w_four_br, w_out, w_peer_q, peer_keys, peer_u, peer_v, final_g):
    d = x_prompt.shape[-1]
    assert w_ada.shape[0] == 1, "single-layer trunk"
    att_w = w_attn_br.shape[1]
    f_w = w_four_br.shape[1]
    kv_w = (w_in.shape[2] - att_w - f_w - 2 * d) // 2
    gd = f_w // N_FGROUPS
    dims = (att_w, kv_w, f_w, gd)

    perm = np.concatenate([np.arange(0, HEAD_DIM, 2), np.arange(1, HEAD_DIM, 2)])
    w = w_in[0]

    def deinterleave(cols):
        return cols.reshape(d, -1, HEAD_DIM)[:, :, perm].reshape(d, -1)

    s_q, s_k, s_v, s_f = att_w, att_w + kv_w, att_w + 2 * kv_w, att_w + 2 * kv_w + f_w
    w_main = jnp.concatenate([deinterleave(w[:, :s_q]), deinterleave(w[:, s_q:s_k]), w[:, s_k:s_v],
                              w[:, s_f:]], axis=1).astype(BF16)
    cch, sch = _dft_tables(gd)
    u_max = jnp.maximum(jnp.max(jnp.abs(peer_u[0])), FP8_TINY)
    wts = {
        "norm1_g": norm1_g[0], "norm2_g": norm2_g[0], "final_g": final_g,
        "w_main": w_main,
        "w_f": w[:, s_v:s_f].astype(BF16),
        "cs_ch": jnp.asarray(np.concatenate([cch, sch], axis=1), BF16),
        "qg": q_norm_g[0][perm].reshape(1, HEAD_DIM),
        "kg": k_norm_g[0][perm].reshape(1, HEAD_DIM),
        "wa": w_attn_br[0].astype(BF16), "wf": w_four_br[0].astype(BF16),
        "w_out": w_out[0].astype(BF16),
        "wq_t": w_peer_q[0].T.astype(BF16),
        "keys": peer_keys[0].astype(BF16),
        "u_q": (peer_u[0] * (FP8_TARGET / u_max)).astype(FP8), "u_inv": u_max * (1.0 / FP8_TARGET),
        "v": peer_v[0].astype(BF16),
    }

    nbp, nbs = c_prompt.shape[0], c_sample.shape[0]
    c_all = jnp.concatenate([c_prompt, c_sample], axis=0)
    c_pad = jnp.pad(c_all, ((0, (-c_all.shape[0]) % 8), (0, 0)))
    mod = _modulation(c_pad, w_ada[0], b_ada[0]).reshape(c_pad.shape[0], 6, d)

    y_prompt = _trunk(x_prompt, mod[:nbp], wts, dims)
    y_sample = _trunk(x_sample, mod[nbp:nbp + nbs], wts, dims)
    return (y_prompt, y_sample)
```

```python
import functools
import math

import jax
import jax.numpy as jnp
import numpy as np
from jax import lax
from jax.experimental import pallas as pl
from jax.experimental.pallas import tpu as pltpu

F32 = jnp.float32
BF16 = jnp.bfloat16
FP8 = jnp.float8_e4m3fn
FP8_TARGET = 256.0
FP8_TINY = 1e-30

HEAD_DIM = 128
GQA_GROUP = 4
Q_PER_TILE = GQA_GROUP * HEAD_DIM
ROPE_THETA = 10000.0
GRID_W = 64
N_FGROUPS = 8
PEER_HEADS = 8
PEER_NKEYS = 128
PEER_HALF = 128
PEER_TOPK = 16
EPS = 1e-6
LANES = 128
NEG_INF = float("-inf")
ONES_ROWS = 16
LOG2E = math.log2(math.e)
MIB = 1024 * 1024


def _params(sem, vmem_mib):
    return pltpu.CompilerParams(dimension_semantics=sem, vmem_limit_bytes=vmem_mib * MIB)


def _mod_kernel(c_ref, w_ref, b_ref, o_ref):
    c = c_ref[...]
    a = c * jax.nn.sigmoid(c)
    o_ref[...] = jnp.dot(a, w_ref[...], preferred_element_type=F32,
                         precision=lax.Precision.HIGHEST) + b_ref[...]


def _modulation(c_pad, w_ada, b_ada):
    rows, d = c_pad.shape
    n = w_ada.shape[1]
    tn = 512
    return pl.pallas_call(
        _mod_kernel,
        grid=(n // tn,),
        in_specs=[pl.BlockSpec((rows, d), lambda j: (0, 0)),
                  pl.BlockSpec((d, tn), lambda j: (0, j)),
                  pl.BlockSpec((1, tn), lambda j: (0, j))],
        out_specs=pl.BlockSpec((rows, tn), lambda j: (0, j)),
        out_shape=jax.ShapeDtypeStruct((rows, n), F32),
        compiler_params=_params(("arbitrary",), 40),
        name="mod",
    )(c_pad, w_ada, b_ada.reshape(1, n))


def _ln_rows(x_ref, g_ref, sc_ref, sh_ref):
    x = x_ref[...]
    ms = jnp.mean(x * x, axis=-1, keepdims=True)
    y = x * lax.rsqrt(ms + EPS) * g_ref[...]
    return y * (1.0 + sc_ref[0]) + sh_ref[0]


def _ln_kernel(x_ref, g_ref, sc_ref, sh_ref, o_ref):
    o_ref[...] = _ln_rows(x_ref, g_ref, sc_ref, sh_ref).astype(BF16)


def _ln_t_kernel(x_ref, g_ref, sc_ref, sh_ref, o_ref, q_ref, s_ref, ht_scr, *, chunk):
    h = _ln_rows(x_ref, g_ref, sc_ref, sh_ref)
    amax = jnp.full(s_ref.shape, FP8_TINY, F32)
    for c in range(h.shape[1] // chunk):
        rows = slice(c * chunk, (c + 1) * chunk)
        blk = h[:, rows].T
        ht_scr[rows, :] = blk
        o_ref[rows, :] = blk.astype(BF16)
        amax = jnp.maximum(amax, jnp.max(jnp.abs(blk), axis=0, keepdims=True))
    q_ref[...] = (ht_scr[...] * (FP8_TARGET / amax)).astype(FP8)
    s_ref[...] = amax * (1.0 / FP8_TARGET)


def _ln_mod(x2d, g, sc, sh, seq, *, transpose):
    t, d = x2d.shape
    tm = 256
    per_b = seq // tm
    in_specs = [pl.BlockSpec((tm, d), lambda i: (i, 0)),
                pl.BlockSpec((1, d), lambda i: (0, 0)),
                pl.BlockSpec((1, 1, d), lambda i: (i // per_b, 0, 0)),
                pl.BlockSpec((1, 1, d), lambda i: (i // per_b, 0, 0))]
    args = (x2d, g.reshape(1, d), sc, sh)
    if not transpose:
        return pl.pallas_call(
            _ln_kernel, grid=(t // tm,), in_specs=in_specs,
            out_specs=pl.BlockSpec((tm, d), lambda i: (i, 0)),
            out_shape=jax.ShapeDtypeStruct((t, d), BF16),
            compiler_params=_params(("arbitrary",), 40), name="ln",
        )(*args)
    col_blk = pl.BlockSpec((d, tm), lambda i: (0, i))
    return pl.pallas_call(
        functools.partial(_ln_t_kernel, chunk=min(512, d)),
        grid=(t // tm,), in_specs=in_specs,
        out_specs=[col_blk, col_blk, pl.BlockSpec((1, tm), lambda i: (0, i))],
        out_shape=[jax.ShapeDtypeStruct((d, t), BF16), jax.ShapeDtypeStruct((d, t), FP8),
                   jax.ShapeDtypeStruct((1, t), F32)],
        scratch_shapes=[pltpu.VMEM((d, tm), F32)],
        compiler_params=_params(("arbitrary",), 40), name="ln_t",
    )(*args)


def _norm_rope(acc, g, cc, ss, scale, o_ref):
    for hd in range(acc.shape[1] // HEAD_DIM):
        xh = acc[:, hd * HEAD_DIM:(hd + 1) * HEAD_DIM]
        ms = jnp.mean(xh * xh, axis=-1, keepdims=True)
        y = xh * lax.rsqrt(ms + EPS) * g
        y = y * cc + pltpu.roll(y, HEAD_DIM // 2, axis=1) * ss
        if scale != 1.0:
            y = y * scale
        o_ref[:, hd * HEAD_DIM:(hd + 1) * HEAD_DIM] = y.astype(BF16)


def _inproj_kernel(h_ref, w_ref, qg_ref, kg_ref, cc_ref, ss_ref, o_ref, *, nq, nk, nv, q_scale):
    j = pl.program_id(1)
    acc = jnp.dot(h_ref[...], w_ref[...], preferred_element_type=F32)

    @pl.when(j < nq)
    def _():
        _norm_rope(acc, qg_ref[...], cc_ref[...], ss_ref[...], q_scale, o_ref)

    @pl.when(jnp.logical_and(j >= nq, j < nq + nk))
    def _():
        _norm_rope(acc, kg_ref[...], cc_ref[...], ss_ref[...], 1.0, o_ref)

    @pl.when(jnp.logical_and(j >= nq + nk, j < nq + nk + nv))
    def _():
        o_ref[...] = acc.astype(BF16)

    @pl.when(j >= nq + nk + nv)
    def _():
        o_ref[...] = jax.nn.sigmoid(acc).astype(BF16)


def _inproj(h, w_main, qg, kg, cc, ss, seq, att_w, kv_w):
    t, d = h.shape
    wtot = w_main.shape[1]
    tm = 512
    tn = min(1024, kv_w)
    per_b = seq // tm
    kern = functools.partial(_inproj_kernel, nq=att_w // tn, nk=kv_w // tn, nv=kv_w // tn,
                             q_scale=HEAD_DIM ** -0.5 * LOG2E)
    return pl.pallas_call(
        kern,
        grid=(t // tm, wtot // tn),
        in_specs=[pl.BlockSpec((tm, d), lambda i, j: (i, 0)),
                  pl.BlockSpec((d, tn), lambda i, j: (0, j)),
                  pl.BlockSpec((1, HEAD_DIM), lambda i, j: (0, 0)),
                  pl.BlockSpec((1, HEAD_DIM), lambda i, j: (0, 0)),
                  pl.BlockSpec((tm, HEAD_DIM), lambda i, j: (i % per_b, 0)),
                  pl.BlockSpec((tm, HEAD_DIM), lambda i, j: (i % per_b, 0))],
        out_specs=pl.BlockSpec((tm, tn), lambda i, j: (i, j)),
        out_shape=jax.ShapeDtypeStruct((t, wtot), BF16),
        compiler_params=_params(("arbitrary", "arbitrary"), 48),
        name="inproj",
    )(h, w_main, qg, kg, cc, ss)


def _fproj_kernel(h_ref, w_ref, cs_ref, p_ref, q_ref, *, gd):
    acc = jnp.dot(h_ref[...], w_ref[...], preferred_element_type=F32)
    for g in range(acc.shape[1] // gd):
        u = acc[:, g * gd:(g + 1) * gd].astype(BF16)
        pq = jnp.dot(u, cs_ref[...], preferred_element_type=F32)
        p_ref[:, g * gd:(g + 1) * gd] = pq[:, :gd].astype(BF16)
        q_ref[:, g * gd:(g + 1) * gd] = pq[:, gd:].astype(BF16)


def _fproj(h, w_f, cs, gd):
    t, d = h.shape
    fw = w_f.shape[1]
    tm = 512
    tn = min(1024, fw)
    out = jax.ShapeDtypeStruct((t, fw), BF16)
    return pl.pallas_call(
        functools.partial(_fproj_kernel, gd=gd),
        grid=(t // tm, fw // tn),
        in_specs=[pl.BlockSpec((tm, d), lambda i, j: (i, 0)),
                  pl.BlockSpec((d, tn), lambda i, j: (0, j)),
                  pl.BlockSpec((gd, 2 * gd), lambda i, j: (0, 0))],
        out_specs=[pl.BlockSpec((tm, tn), lambda i, j: (i, j)),
                   pl.BlockSpec((tm, tn), lambda i, j: (i, j))],
        out_shape=[out, out],
        compiler_params=_params(("arbitrary", "arbitrary"), 48),
        name="fproj",
    )(h, w_f, cs)


def _attn_kernel(q_ref, k_ref, v_ref, o_ref, m_scr, acc_scr, qt_scr, sa_scr, sb_scr, *, tk):
    tq = q_ref.shape[0]
    seq = k_ref.shape[0]
    n_chunks = seq // tk
    for g in range(GQA_GROUP):
        qt_scr[:, g * tq:(g + 1) * tq] = q_ref[:, g * HEAD_DIM:(g + 1) * HEAD_DIM].astype(F32).T.astype(BF16)
    m_scr[...] = jnp.full(m_scr.shape, NEG_INF, F32)
    acc_scr[...] = jnp.zeros(acc_scr.shape, F32)
    ones = jnp.ones((ONES_ROWS, tk), BF16)

    def scores(c, st_ref):
        start = pl.multiple_of(c * tk, tk)
        st_ref[...] = jnp.dot(k_ref[pl.ds(start, tk), :], qt_scr[...], preferred_element_type=F32)

    def consume(c, st_ref):
        start = pl.multiple_of(c * tk, tk)
        v = v_ref[pl.ds(start, tk), :]
        st = st_ref[...]
        m_prev = m_scr[...]
        m_next = jnp.maximum(m_prev, jnp.max(st, axis=0, keepdims=True))
        p = jnp.exp2(st - m_next).astype(BF16)
        alpha = jnp.exp2(m_prev - m_next)
        vt = jnp.concatenate([v.astype(F32).T.astype(BF16), ones], axis=0)
        acc_scr[...] = alpha * acc_scr[...] + jnp.dot(vt, p, preferred_element_type=F32)
        m_scr[...] = m_next

    scores(0, sa_scr)

    def body(i, carry):
        c = 2 * i
        scores(c + 1, sb_scr)
        consume(c, sa_scr)
        scores(c + 2, sa_scr)
        consume(c + 1, sb_scr)
        return carry

    lax.fori_loop(0, n_chunks // 2 - 1, body, 0)
    scores(n_chunks - 1, sb_scr)
    consume(n_chunks - 2, sa_scr)
    consume(n_chunks - 1, sb_scr)
    acc = acc_scr[...]
    out_t = acc[:HEAD_DIM] / acc[HEAD_DIM:HEAD_DIM + 1]
    for g in range(GQA_GROUP):
        o_ref[:, g * HEAD_DIM:(g + 1) * HEAD_DIM] = out_t[:, g * tq:(g + 1) * tq].T.astype(BF16)


def _attention(proj, nb, seq, att_w, kv_w):
    t = proj.shape[0]
    tq = 256
    tk = min(512, seq)
    n_kv = kv_w // HEAD_DIM
    per_b = seq // tq
    k_off = att_w // HEAD_DIM
    v_off = (att_w + kv_w) // HEAD_DIM
    cols = GQA_GROUP * tq
    return pl.pallas_call(
        functools.partial(_attn_kernel, tk=tk),
        grid=(nb, n_kv, per_b),
        in_specs=[pl.BlockSpec((tq, Q_PER_TILE), lambda b, g, i: (b * per_b + i, g)),
                  pl.BlockSpec((seq, HEAD_DIM), lambda b, g, i: (b, k_off + g)),
                  pl.BlockSpec((seq, HEAD_DIM), lambda b, g, i: (b, v_off + g))],
        out_specs=pl.BlockSpec((tq, Q_PER_TILE), lambda b, g, i: (b * per_b + i, g)),
        out_shape=jax.ShapeDtypeStruct((t, att_w), BF16),
        scratch_shapes=[pltpu.VMEM((1, cols), F32),
                        pltpu.VMEM((HEAD_DIM + ONES_ROWS, cols), F32),
                        pltpu.VMEM((HEAD_DIM, cols), BF16),
                        pltpu.VMEM((tk, cols), F32),
                        pltpu.VMEM((tk, cols), F32)],
        compiler_params=_params(("arbitrary", "arbitrary", "arbitrary"), 48),
        name="attn",
    )(proj, proj, proj)


def _fft_a_kernel(m_ref, p_ref, q_ref, twc_ref, tws_ref, ar_ref, ai_ref, *, fw):
    n1 = p_ref.shape[1]
    rhs = jnp.concatenate([p_ref[0], q_ref[0]], axis=0)
    a = jnp.dot(m_ref[...], rhs, preferred_element_type=F32)
    rep = fw // LANES
    for u in range(p_ref.shape[2] // fw):
        ar = a[:n1, u * fw:(u + 1) * fw]
        ai = a[n1:, u * fw:(u + 1) * fw]
        c = jnp.tile(twc_ref[u], (1, rep))
        s = jnp.tile(tws_ref[u], (1, rep))
        ar_ref[0, :, u * fw:(u + 1) * fw] = (ar * c + ai * s).astype(BF16)
        ai_ref[0, :, u * fw:(u + 1) * fw] = (ai * c - ar * s).astype(BF16)


def _fft_c_kernel(cs_ref, ar_ref, ai_ref, o_ref, *, scale):
    rhs = jnp.concatenate([ar_ref[0, 0], ai_ref[0, 0]], axis=0)
    o_ref[0] = (jnp.dot(cs_ref[...], rhs, preferred_element_type=F32) * scale).astype(BF16)


def _dft_tables(n):
    idx = (np.arange(n)[:, None] * np.arange(n)[None, :]) % n
    ang = 2.0 * np.pi * idx / n
    return np.cos(ang), np.sin(ang)


def _seq_dft(p, q, nb, seq, gd):
    fw = p.shape[1]
    n1 = 64 if seq >= 4096 else 16
    n2 = seq // n1
    c1, s1 = _dft_tables(n1)
    c2, s2 = _dft_tables(n2)
    m_a = jnp.asarray(np.block([[c1, -s1], [-s1, -c1]]), BF16)
    cs_c = jnp.asarray(np.concatenate([c2, s2], axis=1), BF16)
    tw = 2.0 * np.pi * (np.arange(n2)[:, None] * np.arange(n1)[None, :]) / seq
    twc = jnp.asarray(np.broadcast_to(np.cos(tw)[:, :, None], (n2, n1, LANES)), F32)
    tws = jnp.asarray(np.broadcast_to(np.sin(tw)[:, :, None], (n2, n1, LANES)), F32)

    nt2 = 4
    tc = nt2 * fw
    p3 = p.reshape(nb, n1, n2 * fw)
    q3 = q.reshape(nb, n1, n2 * fw)
    a_shape = jax.ShapeDtypeStruct((nb, n1, n2 * fw), BF16)
    blk = pl.BlockSpec((1, n1, tc), lambda b, j: (b, 0, j))
    tw_blk = pl.BlockSpec((nt2, n1, LANES), lambda b, j: (j, 0, 0))
    ar, ai = pl.pallas_call(
        functools.partial(_fft_a_kernel, fw=fw),
        grid=(nb, n2 // nt2),
        in_specs=[pl.BlockSpec((2 * n1, 2 * n1), lambda b, j: (0, 0)), blk, blk, tw_blk, tw_blk],
        out_specs=[blk, blk],
        out_shape=[a_shape, a_shape],
        compiler_params=_params(("arbitrary", "arbitrary"), 40),
        name="fft_a",
    )(m_a, p3, q3, twc, tws)

    in_blk = pl.BlockSpec((1, 1, n2, fw), lambda b, k: (b, k, 0, 0))
    out = pl.pallas_call(
        functools.partial(_fft_c_kernel, scale=1.0 / math.sqrt(seq * gd)),
        grid=(nb, n1),
        in_specs=[pl.BlockSpec((n2, 2 * n2), lambda b, k: (0, 0)), in_blk, in_blk],
        out_specs=pl.BlockSpec((1, n2, fw), lambda b, k: (b, 0, k)),
        out_shape=jax.ShapeDtypeStruct((nb, n2, n1 * fw), BF16),
        compiler_params=_params(("arbitrary", "arbitrary"), 40),
        name="fft_c",
    )(cs_c, ar.reshape(nb, n1, n2, fw), ai.reshape(nb, n1, n2, fw))
    return out.reshape(nb * seq, fw)


def _merge_kernel(a_ref, f_ref, wa_ref, wf_ref, ga_ref, gf_ref, o_ref):
    a_br = jnp.dot(a_ref[...], wa_ref[...], preferred_element_type=F32)
    f_br = jnp.dot(f_ref[...], wf_ref[...], preferred_element_type=F32)
    o_ref[...] = (ga_ref[...].astype(F32) * a_br + gf_ref[...].astype(F32) * f_br).astype(BF16)


def _merge(att, four, wa, wf, proj, gate_off):
    t, aw = att.shape
    fw = four.shape[1]
    d = wa.shape[1]
    tm, tn = 512, 512
    ga0 = gate_off // tn
    gf0 = (gate_off + d) // tn
    return pl.pallas_call(
        _merge_kernel,
        grid=(t // tm, d // tn),
        in_specs=[pl.BlockSpec((tm, aw), lambda i, j: (i, 0)),
                  pl.BlockSpec((tm, fw), lambda i, j: (i, 0)),
                  pl.BlockSpec((aw, tn), lambda i, j: (0, j)),
                  pl.BlockSpec((fw, tn), lambda i, j: (0, j)),
                  pl.BlockSpec((tm, tn), lambda i, j: (i, ga0 + j)),
                  pl.BlockSpec((tm, tn), lambda i, j: (i, gf0 + j))],
        out_specs=pl.BlockSpec((tm, tn), lambda i, j: (i, j)),
        out_shape=jax.ShapeDtypeStruct((t, d), BF16),
        compiler_params=_params(("arbitrary", "arbitrary"), 48),
        name="merge",
    )(att, four, wa, wf, proj, proj)


def _outproj_kernel(m_ref, w_ref, x_ref, g_ref, o_ref):
    acc = jnp.dot(m_ref[...], w_ref[...], preferred_element_type=F32)
    o_ref[...] = x_ref[...] + g_ref[0] * acc


def _outproj(merged, w_out, x2d, g1, seq):
    t, d = merged.shape
    tm, tn = 512, 1024
    per_b = seq // tm
    return pl.pallas_call(
        _outproj_kernel,
        grid=(t // tm, d // tn),
        in_specs=[pl.BlockSpec((tm, d), lambda i, j: (i, 0)),
                  pl.BlockSpec((d, tn), lambda i, j: (0, j)),
                  pl.BlockSpec((tm, tn), lambda i, j: (i, j)),
                  pl.BlockSpec((1, 1, tn), lambda i, j: (i // per_b, 0, j))],
        out_specs=pl.BlockSpec((tm, tn), lambda i, j: (i, j)),
        out_shape=jax.ShapeDtypeStruct((t, d), F32),
        compiler_params=_params(("arbitrary", "arbitrary"), 48),
        name="outproj",
    )(merged, w_out, x2d, g1)


def _top_values(s, k, store):
    n = s.shape[0]
    iota = lax.broadcasted_iota(jnp.int32, s.shape, 0)
    cur = s
    for r in range(k):
        m = jnp.max(cur, axis=0, keepdims=True)
        store(r, m)
        first = jnp.min(jnp.where(cur == m, iota, n), axis=0, keepdims=True)
        cur = jnp.where(iota == first, NEG_INF, cur)


def _peer_score_kernel(h_ref, wq_ref, keys_ref, g0_ref, phi_ref, e1_ref, top_scr, best_scr):
    k1 = PEER_TOPK + 1
    qt = jnp.dot(wq_ref[...], h_ref[...], preferred_element_type=F32).astype(BF16)
    top_scr[...] = jnp.full(top_scr.shape, NEG_INF, F32)
    s = []
    for p in range(2):
        s.append(jnp.dot(keys_ref[0, p], qt[p * PEER_HALF:(p + 1) * PEER_HALF], preferred_element_type=F32))

        def store(r, row, p=p):
            top_scr[p, r:r + 1, :] = row
        _top_values(s[p], k1, store)

    cand = jnp.concatenate([top_scr[0, 0:1, :] + top_scr[1]]
                           + [top_scr[0, i:i + 1, :] + top_scr[1, 0:8, :] for i in range(1, k1)], axis=0)

    def store_best(r, row):
        best_scr[r:r + 1, :] = row
    _top_values(cand, k1, store_best)

    best = best_scr[0:PEER_TOPK]
    z = jnp.sum(jnp.exp(best - best[0:1]), axis=0, keepdims=True)
    tau = 0.5 * (best_scr[PEER_TOPK - 1:PEER_TOPK] + best_scr[PEER_TOPK:k1])
    max0 = top_scr[0, 0:1, :]
    max1 = top_scr[1, 0:1, :]
    g0_ref[0] = jnp.exp(s[0] - max0) / z
    phi_ref[0] = jnp.exp((tau - max1) - s[0])
    e1_ref[0] = jnp.exp(s[1] - max1)


def _peer_scores(h2t, wq_t, keys):
    d, t = h2t.shape
    tm = 512
    top_rows = 24
    s_shape = jax.ShapeDtypeStruct((PEER_HEADS, PEER_NKEYS, t), F32)
    s_blk = pl.BlockSpec((1, PEER_NKEYS, tm), lambda i, h: (h, 0, i))
    return pl.pallas_call(
        _peer_score_kernel,
        grid=(t // tm, PEER_HEADS),
        in_specs=[pl.BlockSpec((d, tm), lambda i, h: (0, i)),
                  pl.BlockSpec((2 * PEER_HALF, d), lambda i, h: (h, 0)),
                  pl.BlockSpec((1, 2, PEER_NKEYS, PEER_HALF), lambda i, h: (h, 0, 0, 0))],
        out_specs=[s_blk, s_blk, s_blk],
        out_shape=[s_shape, s_shape, s_shape],
        scratch_shapes=[pltpu.VMEM((2, top_rows, tm), F32), pltpu.VMEM((top_rows, tm), F32)],
        compiler_params=_params(("arbitrary", "arbitrary"), 40),
        name="peer_scores",
    )(h2t, wq_t, keys)


def _peer_w_kernel(h_ref, hs_ref, u_ref, g0_ref, phi_ref, e1_ref, w_ref, wmax_ref, st_scr, *, n_tiles):
    j = pl.program_id(1)
    tm, te = w_ref.shape
    rows = te // PEER_NKEYS

    @pl.when(j == 0)
    def _():
        st_scr[1] = jnp.zeros(st_scr.shape[1:], F32)
        wmax_ref[...] = jnp.zeros(wmax_ref.shape, F32)

    gate_tile = jnp.clip(j - 1, 0, n_tiles - 1)

    half = tm // 2

    def step(par):
        def body(a, carry):
            ts = pl.multiple_of(a * half, half)
            st_scr[par, :, pl.ds(ts, half)] = jnp.dot(u_ref[...], h_ref[:, pl.ds(ts, half)],
                                                      preferred_element_type=F32)
            inv_scale = hs_ref[:, pl.ds(ts, half)]
            for r in range(rows):
                n0 = gate_tile * rows + r
                ks = slice(r * PEER_NKEYS, (r + 1) * PEER_NKEYS)
                st = st_scr[1 - par, ks, pl.ds(ts, half)] * inv_scale
                act = 0.5 * st * (1.0 + lax.erf(st * math.sqrt(0.5)))
                gate = None
                for hd in range(PEER_HEADS):
                    e1 = e1_ref[hd, :, pl.ds(ts, half)]
                    g = jnp.where(e1 >= phi_ref[hd, pl.ds(n0, 1), pl.ds(ts, half)],
                                  e1 * g0_ref[hd, pl.ds(n0, 1), pl.ds(ts, half)], 0.0)
                    gate = g if gate is None else gate + g
                wt = act * gate
                w_ref[pl.ds(ts, half), ks] = wt.T.astype(BF16)
                wmax_ref[:, pl.ds(ts, half)] = jnp.maximum(wmax_ref[:, pl.ds(ts, half)],
                                                           jnp.max(jnp.abs(wt), axis=0, keepdims=True))
            return carry

        lax.fori_loop(0, 2, body, 0)

    @pl.when(j % 2 == 0)
    def _():
        step(0)

    @pl.when(j % 2 == 1)
    def _():
        step(1)


def _peer_weights(h2t, hs, u_q, g0, phi, e1):
    d, t = h2t.shape
    n_exp = u_q.shape[0]
    tm, te = 512, 512
    n_tiles = n_exp // te
    once = pl.Buffered(1)
    sc_blk = pl.BlockSpec((PEER_HEADS, PEER_NKEYS, tm), lambda i, j: (0, 0, i), pipeline_mode=once)
    return pl.pallas_call(
        functools.partial(_peer_w_kernel, n_tiles=n_tiles),
        grid=(t // tm, n_tiles + 1),
        in_specs=[pl.BlockSpec((d, tm), lambda i, j: (0, i), pipeline_mode=once),
                  pl.BlockSpec((1, tm), lambda i, j: (0, i)),
                  pl.BlockSpec((te, d), lambda i, j: (jnp.minimum(j, n_tiles - 1), 0)),
                  sc_blk, sc_blk, sc_blk],
        out_specs=[pl.BlockSpec((tm, te), lambda i, j: (i, jnp.maximum(j - 1, 0))),
                   pl.BlockSpec((1, tm), lambda i, j: (0, i))],
        out_shape=[jax.ShapeDtypeStruct((t, n_exp), BF16), jax.ShapeDtypeStruct((1, t), F32)],
        scratch_shapes=[pltpu.VMEM((2, te, tm), F32)],
        compiler_params=_params(("arbitrary", "arbitrary"), 48),
        name="peer_w",
    )(h2t, hs, u_q, g0, phi, e1)


def _peer_v_kernel(w_ref, sc_ref, dq_ref, v_ref, o_ref, wq_scr, *, chunk):
    @pl.when(pl.program_id(1) == 0)
    def _():
        for c in range(w_ref.shape[1] // chunk):
            cols = slice(c * chunk, (c + 1) * chunk)
            wq_scr[:, cols] = (w_ref[:, cols].astype(F32) * sc_ref[...]).astype(FP8)

    o_ref[...] = jnp.dot(wq_scr[...], v_ref[...], preferred_element_type=F32) * dq_ref[...]


def _peer_values(w, wmax, v_q, v_inv):
    t, n_exp = w.shape
    d = v_q.shape[1]
    tm, tn = 512, min(256, d)
    wmax_col = jnp.maximum(wmax.reshape(t, 1), FP8_TINY)
    scale = FP8_TARGET / wmax_col
    dequant = wmax_col * (v_inv / FP8_TARGET)
    col = pl.BlockSpec((tm, 1), lambda i, j: (i, 0))
    return pl.pallas_call(
        functools.partial(_peer_v_kernel, chunk=min(2048, n_exp)),
        grid=(t // tm, d // tn),
        in_specs=[pl.BlockSpec((tm, n_exp), lambda i, j: (i, 0), pipeline_mode=pl.Buffered(1)),
                  col, col,
                  pl.BlockSpec((n_exp, tn), lambda i, j: (0, j))],
        out_specs=pl.BlockSpec((tm, tn), lambda i, j: (i, j)),
        out_shape=jax.ShapeDtypeStruct((t, d), F32),
        scratch_shapes=[pltpu.VMEM((tm, n_exp), FP8)],
        compiler_params=_params(("arbitrary", "arbitrary"), 48),
        name="peer_v",
    )(w, scale, dequant, v_q)


def _final_kernel(x_ref, p_ref, g2_ref, fg_ref, o_ref):
    x = x_ref[...] + g2_ref[0] * p_ref[...]
    ms = jnp.mean(x * x, axis=-1, keepdims=True)
    o_ref[...] = x * lax.rsqrt(ms + EPS) * fg_ref[...]


def _final(x1, peer, g2, final_g, seq):
    t, d = x1.shape
    tm = 256
    per_b = seq // tm
    row = pl.BlockSpec((tm, d), lambda i: (i, 0))
    return pl.pallas_call(
        _final_kernel,
        grid=(t // tm,),
        in_specs=[row, row,
                  pl.BlockSpec((1, 1, d), lambda i: (i // per_b, 0, 0)),
                  pl.BlockSpec((1, d), lambda i: (0, 0))],
        out_specs=row,
        out_shape=jax.ShapeDtypeStruct((t, d), F32),
        compiler_params=_params(("arbitrary",), 48),
        name="final",
    )(x1, peer, g2, final_g.reshape(1, d))


def _rope_tables(seq):
    rows = seq // GRID_W
    row_ids = jnp.repeat(jnp.arange(rows), GRID_W).astype(F32)
    col_ids = jnp.tile(jnp.arange(GRID_W), rows).astype(F32)
    axis_dim = HEAD_DIM // 2
    inv_freq = ROPE_THETA ** (-jnp.arange(0, axis_dim, 2, dtype=F32) / axis_dim)
    ang = jnp.concatenate([row_ids[:, None] * inv_freq, col_ids[:, None] * inv_freq], axis=-1)
    cos, sin = jnp.cos(ang), jnp.sin(ang)
    return jnp.concatenate([cos, cos], axis=-1), jnp.concatenate([-sin, sin], axis=-1)


def _trunk(x, mod, wts, dims):
    nb, seq, d = x.shape
    att_w, kv_w, f_w, gd = dims
    x2d = x.reshape(nb * seq, d)
    sh1, sc1, g1, sh2, sc2, g2 = [mod[:, k].reshape(nb, 1, d) for k in range(6)]
    cc, ss = _rope_tables(seq)

    h = _ln_mod(x2d, wts["norm1_g"], sc1, sh1, seq, transpose=False)
    proj = _inproj(h, wts["w_main"], wts["qg"], wts["kg"], cc, ss, seq, att_w, kv_w)
    p, q = _fproj(h, wts["w_f"], wts["cs_ch"], gd)
    att = _attention(proj, nb, seq, att_w, kv_w)
    four = _seq_dft(p, q, nb, seq, gd)
    merged = _merge(att, four, wts["wa"], wts["wf"], proj, att_w + 2 * kv_w)
    x1 = _outproj(merged, wts["w_out"], x2d, g1, seq)

    h2t, h2t_q, h2_inv = _ln_mod(x1, wts["norm2_g"], sc2, sh2, seq, transpose=True)
    g0, phi, e1 = _peer_scores(h2t, wts["wq_t"], wts["keys"])
    w, wmax = _peer_weights(h2t_q, h2_inv * wts["u_inv"], wts["u_q"], g0, phi, e1)
    peer = _peer_values(w, wmax, wts["v_q"], wts["v_inv"])
    y = _final(x1, peer, g2, wts["final_g"], seq)
    return y.reshape(nb, seq, d)


def kernel(x_prompt, x_sample, c_prompt, c_sample, w_ada, b_ada, norm1_g, norm2_g, w_in, q_norm_g, k_norm_g, w_attn_br, w_four_br, w_out, w_peer_q, peer_keys, peer_u, peer_v, final_g):
    d = x_prompt.shape[-1]
    assert w_ada.shape[0] == 1, "single-layer trunk"
    att_w = w_attn_br.shape[1]
    f_w = w_four_br.shape[1]
    kv_w = (w_in.shape[2] - att_w - f_w - 2 * d) // 2
    gd = f_w // N_FGROUPS
    dims = (att_w, kv_w, f_w, gd)

    perm = np.concatenate([np.arange(0, HEAD_DIM, 2), np.arange(1, HEAD_DIM, 2)])
    w = w_in[0]

    def deinterleave(cols):
        return cols.reshape(d, -1, HEAD_DIM)[:, :, perm].reshape(d, -1)

    s_q, s_k, s_v, s_f = att_w, att_w + kv_w, att_w + 2 * kv_w, att_w + 2 * kv_w + f_w
    w_main = jnp.concatenate([deinterleave(w[:, :s_q]), deinterleave(w[:, s_q:s_k]), w[:, s_k:s_v],
                              w[:, s_f:]], axis=1).astype(BF16)
    cch, sch = _dft_tables(gd)
    u_max = jnp.maximum(jnp.max(jnp.abs(peer_u[0])), FP8_TINY)
    v_max = jnp.maximum(jnp.max(jnp.abs(peer_v[0])), FP8_TINY)
    wts = {
        "norm1_g": norm1_g[0], "norm2_g": norm2_g[0], "final_g": final_g,
        "w_main": w_main,
        "w_f": w[:, s_v:s_f].astype(BF16),
        "cs_ch": jnp.asarray(np.concatenate([cch, sch], axis=1), BF16),
        "qg": q_norm_g[0][perm].reshape(1, HEAD_DIM),
        "kg": k_norm_g[0][perm].reshape(1, HEAD_DIM),
        "wa": w_attn_br[0].astype(BF16), "wf": w_four_br[0].astype(BF16),
        "w_out": w_out[0].astype(BF16),
        "wq_t": w_peer_q[0].T.astype(BF16),
        "keys": peer_keys[0].astype(BF16),
        "u_q": (peer_u[0] * (FP8_TARGET / u_max)).astype(FP8), "u_inv": u_max * (1.0 / FP8_TARGET),
        "v_q": (peer_v[0] * (FP8_TARGET / v_max)).astype(FP8), "v_inv": v_max * (1.0 / FP8_TARGET),
    }

    nbp, nbs = c_prompt.shape[0], c_sample.shape[0]
    c_all = jnp.concatenate([c_prompt, c_sample], axis=0)
    c_pad = jnp.pad(c_all, ((0, (-c_all.shape[0]) % 8), (0, 0)))
    mod = _modulation(c_pad, w_ada[0], b_ada[0]).reshape(c_pad.shape[0], 6, d)

    y_prompt = _trunk(x_prompt, mod[:nbp], wts, dims)
    y_sample = _trunk(x_sample, mod[nbp:nbp + nbs], wts, dims)
    return (y_prompt, y_sample)
```

```python
import functools
import math

import jax
import jax.numpy as jnp
import numpy as np
from jax import lax
from jax.experimental import pallas as pl
from jax.experimental.pallas import tpu as pltpu

F32 = jnp.float32
BF16 = jnp.bfloat16
FP8 = jnp.float8_e4m3fn
FP8_TARGET = 256.0
FP8_TINY = 1e-30

HEAD_DIM = 128
GQA_GROUP = 4
Q_PER_TILE = GQA_GROUP * HEAD_DIM
ROPE_THETA = 10000.0
GRID_W = 64
N_FGROUPS = 8
PEER_HEADS = 8
PEER_NKEYS = 128
PEER_HALF = 128
PEER_TOPK = 16
EPS = 1e-6
LANES = 128
NEG_INF = float("-inf")
ONES_ROWS = 16
LOG2E = math.log2(math.e)
MIB = 1024 * 1024


def _params(sem, vmem_mib):
    return pltpu.CompilerParams(dimension_semantics=sem, vmem_limit_bytes=vmem_mib * MIB)


def _mod_kernel(c_ref, w_ref, b_ref, o_ref):
    c = c_ref[...]
    a = c * jax.nn.sigmoid(c)
    o_ref[...] = jnp.dot(a, w_ref[...], preferred_element_type=F32,
                         precision=lax.Precision.HIGHEST) + b_ref[...]


def _modulation(c_pad, w_ada, b_ada):
    rows, d = c_pad.shape
    n = w_ada.shape[1]
    tn = 512
    return pl.pallas_call(
        _mod_kernel,
        grid=(n // tn,),
        in_specs=[pl.BlockSpec((rows, d), lambda j: (0, 0)),
                  pl.BlockSpec((d, tn), lambda j: (0, j)),
                  pl.BlockSpec((1, tn), lambda j: (0, j))],
        out_specs=pl.BlockSpec((rows, tn), lambda j: (0, j)),
        out_shape=jax.ShapeDtypeStruct((rows, n), F32),
        compiler_params=_params(("arbitrary",), 40),
        name="mod",
    )(c_pad, w_ada, b_ada.reshape(1, n))


def _ln_rows(x_ref, g_ref, sc_ref, sh_ref):
    x = x_ref[...]
    ms = jnp.mean(x * x, axis=-1, keepdims=True)
    y = x * lax.rsqrt(ms + EPS) * g_ref[...]
    return y * (1.0 + sc_ref[0]) + sh_ref[0]


def _ln_kernel(x_ref, g_ref, sc_ref, sh_ref, o_ref):
    o_ref[...] = _ln_rows(x_ref, g_ref, sc_ref, sh_ref).astype(BF16)


def _ln_t_kernel(x_ref, g_ref, sc_ref, sh_ref, o_ref, q_ref, s_ref, ht_scr, *, chunk):
    h = _ln_rows(x_ref, g_ref, sc_ref, sh_ref)
    amax = jnp.full(s_ref.shape, FP8_TINY, F32)
    for c in range(h.shape[1] // chunk):
        rows = slice(c * chunk, (c + 1) * chunk)
        blk = h[:, rows].T
        ht_scr[rows, :] = blk
        o_ref[rows, :] = blk.astype(BF16)
        amax = jnp.maximum(amax, jnp.max(jnp.abs(blk), axis=0, keepdims=True))
    q_ref[...] = (ht_scr[...] * (FP8_TARGET / amax)).astype(FP8)
    s_ref[...] = amax * (1.0 / FP8_TARGET)


def _ln_mod(x2d, g, sc, sh, seq, *, transpose):
    t, d = x2d.shape
    tm = 256
    per_b = seq // tm
    in_specs = [pl.BlockSpec((tm, d), lambda i: (i, 0)),
                pl.BlockSpec((1, d), lambda i: (0, 0)),
                pl.BlockSpec((1, 1, d), lambda i: (i // per_b, 0, 0)),
                pl.BlockSpec((1, 1, d), lambda i: (i // per_b, 0, 0))]
    args = (x2d, g.reshape(1, d), sc, sh)
    if not transpose:
        return pl.pallas_call(
            _ln_kernel, grid=(t // tm,), in_specs=in_specs,
            out_specs=pl.BlockSpec((tm, d), lambda i: (i, 0)),
            out_shape=jax.ShapeDtypeStruct((t, d), BF16),
            compiler_params=_params(("arbitrary",), 40), name="ln",
        )(*args)
    col_blk = pl.BlockSpec((d, tm), lambda i: (0, i))
    return pl.pallas_call(
        functools.partial(_ln_t_kernel, chunk=min(512, d)),
        grid=(t // tm,), in_specs=in_specs,
        out_specs=[col_blk, col_blk, pl.BlockSpec((1, tm), lambda i: (0, i))],
        out_shape=[jax.ShapeDtypeStruct((d, t), BF16), jax.ShapeDtypeStruct((d, t), FP8),
                   jax.ShapeDtypeStruct((1, t), F32)],
        scratch_shapes=[pltpu.VMEM((d, tm), F32)],
        compiler_params=_params(("arbitrary",), 40), name="ln_t",
    )(*args)


def _norm_rope(acc, g, cc, ss, scale, o_ref):
    for hd in range(acc.shape[1] // HEAD_DIM):
        xh = acc[:, hd * HEAD_DIM:(hd + 1) * HEAD_DIM]
        ms = jnp.mean(xh * xh, axis=-1, keepdims=True)
        y = xh * lax.rsqrt(ms + EPS) * g
        y = y * cc + pltpu.roll(y, HEAD_DIM // 2, axis=1) * ss
        if scale != 1.0:
            y = y * scale
        o_ref[:, hd * HEAD_DIM:(hd + 1) * HEAD_DIM] = y.astype(BF16)


def _inproj_kernel(h_ref, w_ref, qg_ref, kg_ref, cc_ref, ss_ref, o_ref, *, nq, nk, nv, q_scale):
    j = pl.program_id(1)
    acc = jnp.dot(h_ref[...], w_ref[...], preferred_element_type=F32)

    @pl.when(j < nq)
    def _():
        _norm_rope(acc, qg_ref[...], cc_ref[...], ss_ref[...], q_scale, o_ref)

    @pl.when(jnp.logical_and(j >= nq, j < nq + nk))
    def _():
        _norm_rope(acc, kg_ref[...], cc_ref[...], ss_ref[...], 1.0, o_ref)

    @pl.when(jnp.logical_and(j >= nq + nk, j < nq + nk + nv))
    def _():
        o_ref[...] = acc.astype(BF16)

    @pl.when(j >= nq + nk + nv)
    def _():
        o_ref[...] = jax.nn.sigmoid(acc).astype(BF16)


def _inproj(h, w_main, qg, kg, cc, ss, seq, att_w, kv_w):
    t, d = h.shape
    wtot = w_main.shape[1]
    tm = 512
    tn = min(1024, kv_w)
    per_b = seq // tm
    kern = functools.partial(_inproj_kernel, nq=att_w // tn, nk=kv_w // tn, nv=kv_w // tn,
                             q_scale=HEAD_DIM ** -0.5 * LOG2E)
    return pl.pallas_call(
        kern,
        grid=(t // tm, wtot // tn),
        in_specs=[pl.BlockSpec((tm, d), lambda i, j: (i, 0)),
                  pl.BlockSpec((d, tn), lambda i, j: (0, j)),
                  pl.BlockSpec((1, HEAD_DIM), lambda i, j: (0, 0)),
                  pl.BlockSpec((1, HEAD_DIM), lambda i, j: (0, 0)),
                  pl.BlockSpec((tm, HEAD_DIM), lambda i, j: (i % per_b, 0)),
                  pl.BlockSpec((tm, HEAD_DIM), lambda i, j: (i % per_b, 0))],
        out_specs=pl.BlockSpec((tm, tn), lambda i, j: (i, j)),
        out_shape=jax.ShapeDtypeStruct((t, wtot), BF16),
        compiler_params=_params(("arbitrary", "arbitrary"), 48),
        name="inproj",
    )(h, w_main, qg, kg, cc, ss)


def _fproj_kernel(h_ref, w_ref, cs_ref, p_ref, q_ref, *, gd):
    acc = jnp.dot(h_ref[...], w_ref[...], preferred_element_type=F32)
    for g in range(acc.shape[1] // gd):
        u = acc[:, g * gd:(g + 1) * gd].astype(BF16)
        pq = jnp.dot(u, cs_ref[...], preferred_element_type=F32)
        p_ref[:, g * gd:(g + 1) * gd] = pq[:, :gd].astype(BF16)
        q_ref[:, g * gd:(g + 1) * gd] = pq[:, gd:].astype(BF16)


def _fproj(h, w_f, cs, gd):
    t, d = h.shape
    fw = w_f.shape[1]
    tm = 512
    tn = min(1024, fw)
    out = jax.ShapeDtypeStruct((t, fw), BF16)
    return pl.pallas_call(
        functools.partial(_fproj_kernel, gd=gd),
        grid=(t // tm, fw // tn),
        in_specs=[pl.BlockSpec((tm, d), lambda i, j: (i, 0)),
                  pl.BlockSpec((d, tn), lambda i, j: (0, j)),
                  pl.BlockSpec((gd, 2 * gd), lambda i, j: (0, 0))],
        out_specs=[pl.BlockSpec((tm, tn), lambda i, j: (i, j)),
                   pl.BlockSpec((tm, tn), lambda i, j: (i, j))],
        out_shape=[out, out],
        compiler_params=_params(("arbitrary", "arbitrary"), 48),
        name="fproj",
    )(h, w_f, cs)


def _attn_kernel(q_ref, k_ref, v_ref, o_ref, m_scr, acc_scr, qt_scr, sa_scr, sb_scr, *, tk):
    tq = q_ref.shape[0]
    seq = k_ref.shape[0]
    n_chunks = seq // tk
    for g in range(GQA_GROUP):
        qt_scr[:, g * tq:(g + 1) * tq] = q_ref[:, g * HEAD_DIM:(g + 1) * HEAD_DIM].astype(F32).T.astype(BF16)
    m_scr[...] = jnp.full(m_scr.shape, NEG_INF, F32)
    acc_scr[...] = jnp.zeros(acc_scr.shape, F32)
    ones = jnp.ones((ONES_ROWS, tk), BF16)

    def scores(c, st_ref):
        start = pl.multiple_of(c * tk, tk)
        st_ref[...] = jnp.dot(k_ref[pl.ds(start, tk), :], qt_scr[...], preferred_element_type=F32)

    def consume(c, st_ref):
        start = pl.multiple_of(c * tk, tk)
        v = v_ref[pl.ds(start, tk), :]
        st = st_ref[...]
        m_prev = m_scr[...]
        m_next = jnp.maximum(m_prev, jnp.max(st, axis=0, keepdims=True))
        p = jnp.exp2(st - m_next).astype(BF16)
        alpha = jnp.exp2(m_prev - m_next)
        vt = jnp.concatenate([v.astype(F32).T.astype(BF16), ones], axis=0)
        acc_scr[...] = alpha * acc_scr[...] + jnp.dot(vt, p, preferred_element_type=F32)
        m_scr[...] = m_next

    scores(0, sa_scr)

    def body(i, carry):
        c = 2 * i
        scores(c + 1, sb_scr)
        consume(c, sa_scr)
        scores(c + 2, sa_scr)
        consume(c + 1, sb_scr)
        return carry

    lax.fori_loop(0, n_chunks // 2 - 1, body, 0)
    scores(n_chunks - 1, sb_scr)
    consume(n_chunks - 2, sa_scr)
    consume(n_chunks - 1, sb_scr)
    acc = acc_scr[...]
    out_t = acc[:HEAD_DIM] / acc[HEAD_DIM:HEAD_DIM + 1]
    for g in range(GQA_GROUP):
        o_ref[:, g * HEAD_DIM:(g + 1) * HEAD_DIM] = out_t[:, g * tq:(g + 1) * tq].T.astype(BF16)


def _attention(proj, nb, seq, att_w, kv_w):
    t = proj.shape[0]
    tq = 256
    tk = min(1024, seq // 2)
    assert seq % (2 * tk) == 0
    n_kv = kv_w // HEAD_DIM
    per_b = seq // tq
    k_off = att_w // HEAD_DIM
    v_off = (att_w + kv_w) // HEAD_DIM
    cols = GQA_GROUP * tq
    return pl.pallas_call(
        functools.partial(_attn_kernel, tk=tk),
        grid=(nb, n_kv, per_b),
        in_specs=[pl.BlockSpec((tq, Q_PER_TILE), lambda b, g, i: (b * per_b + i, g)),
                  pl.BlockSpec((seq, HEAD_DIM), lambda b, g, i: (b, k_off + g)),
                  pl.BlockSpec((seq, HEAD_DIM), lambda b, g, i: (b, v_off + g))],
        out_specs=pl.BlockSpec((tq, Q_PER_TILE), lambda b, g, i: (b * per_b + i, g)),
        out_shape=jax.ShapeDtypeStruct((t, att_w), BF16),
        scratch_shapes=[pltpu.VMEM((1, cols), F32),
                        pltpu.VMEM((HEAD_DIM + ONES_ROWS, cols), F32),
                        pltpu.VMEM((HEAD_DIM, cols), BF16),
                        pltpu.VMEM((tk, cols), F32),
                        pltpu.VMEM((tk, cols), F32)],
        compiler_params=_params(("arbitrary", "arbitrary", "arbitrary"), 48),
        name="attn",
    )(proj, proj, proj)


def _fft_a_kernel(m_ref, p_ref, q_ref, twc_ref, tws_ref, ar_ref, ai_ref, *, fw):
    n1 = p_ref.shape[1]
    rhs = jnp.concatenate([p_ref[0], q_ref[0]], axis=0)
    a = jnp.dot(m_ref[...], rhs, preferred_element_type=F32)
    rep = fw // LANES
    for u in range(p_ref.shape[2] // fw):
        ar = a[:n1, u * fw:(u + 1) * fw]
        ai = a[n1:, u * fw:(u + 1) * fw]
        c = jnp.tile(twc_ref[u], (1, rep))
        s = jnp.tile(tws_ref[u], (1, rep))
        ar_ref[0, :, u * fw:(u + 1) * fw] = (ar * c + ai * s).astype(BF16)
        ai_ref[0, :, u * fw:(u + 1) * fw] = (ai * c - ar * s).astype(BF16)


def _fft_c_kernel(cs_ref, ar_ref, ai_ref, o_ref, *, scale):
    rhs = jnp.concatenate([ar_ref[0, 0], ai_ref[0, 0]], axis=0)
    o_ref[0] = (jnp.dot(cs_ref[...], rhs, preferred_element_type=F32) * scale).astype(BF16)


def _dft_tables(n):
    idx = (np.arange(n)[:, None] * np.arange(n)[None, :]) % n
    ang = 2.0 * np.pi * idx / n
    return np.cos(ang), np.sin(ang)


def _seq_dft(p, q, nb, seq, gd):
    fw = p.shape[1]
    n1 = 64 if seq >= 4096 else 16
    n2 = seq // n1
    c1, s1 = _dft_tables(n1)
    c2, s2 = _dft_tables(n2)
    m_a = jnp.asarray(np.block([[c1, -s1], [-s1, -c1]]), BF16)
    cs_c = jnp.asarray(np.concatenate([c2, s2], axis=1), BF16)
    tw = 2.0 * np.pi * (np.arange(n2)[:, None] * np.arange(n1)[None, :]) / seq
    twc = jnp.asarray(np.broadcast_to(np.cos(tw)[:, :, None], (n2, n1, LANES)), F32)
    tws = jnp.asarray(np.broadcast_to(np.sin(tw)[:, :, None], (n2, n1, LANES)), F32)

    nt2 = 4
    tc = nt2 * fw
    p3 = p.reshape(nb, n1, n2 * fw)
    q3 = q.reshape(nb, n1, n2 * fw)
    a_shape = jax.ShapeDtypeStruct((nb, n1, n2 * fw), BF16)
    blk = pl.BlockSpec((1, n1, tc), lambda b, j: (b, 0, j))
    tw_blk = pl.BlockSpec((nt2, n1, LANES), lambda b, j: (j, 0, 0))
    ar, ai = pl.pallas_call(
        functools.partial(_fft_a_kernel, fw=fw),
        grid=(nb, n2 // nt2),
        in_specs=[pl.BlockSpec((2 * n1, 2 * n1), lambda b, j: (0, 0)), blk, blk, tw_blk, tw_blk],
        out_specs=[blk, blk],
        out_shape=[a_shape, a_shape],
        compiler_params=_params(("arbitrary", "arbitrary"), 40),
        name="fft_a",
    )(m_a, p3, q3, twc, tws)

    in_blk = pl.BlockSpec((1, 1, n2, fw), lambda b, k: (b, k, 0, 0))
    out = pl.pallas_call(
        functools.partial(_fft_c_kernel, scale=1.0 / math.sqrt(seq * gd)),
        grid=(nb, n1),
        in_specs=[pl.BlockSpec((n2, 2 * n2), lambda b, k: (0, 0)), in_blk, in_blk],
        out_specs=pl.BlockSpec((1, n2, fw), lambda b, k: (b, 0, k)),
        out_shape=jax.ShapeDtypeStruct((nb, n2, n1 * fw), BF16),
        compiler_params=_params(("arbitrary", "arbitrary"), 40),
        name="fft_c",
    )(cs_c, ar.reshape(nb, n1, n2, fw), ai.reshape(nb, n1, n2, fw))
    return out.reshape(nb * seq, fw)


def _merge_kernel(a_ref, f_ref, wa_ref, wf_ref, ga_ref, gf_ref, o_ref):
    a_br = jnp.dot(a_ref[...], wa_ref[...], preferred_element_type=F32)
    f_br = jnp.dot(f_ref[...], wf_ref[...], preferred_element_type=F32)
    o_ref[...] = (ga_ref[...].astype(F32) * a_br + gf_ref[...].astype(F32) * f_br).astype(BF16)


def _merge(att, four, wa, wf, proj, gate_off):
    t, aw = att.shape
    fw = four.shape[1]
    d = wa.shape[1]
    tm, tn = 512, 512
    ga0 = gate_off // tn
    gf0 = (gate_off + d) // tn
    return pl.pallas_call(
        _merge_kernel,
        grid=(t // tm, d // tn),
        in_specs=[pl.BlockSpec((tm, aw), lambda i, j: (i, 0)),
                  pl.BlockSpec((tm, fw), lambda i, j: (i, 0)),
                  pl.BlockSpec((aw, tn), lambda i, j: (0, j)),
                  pl.BlockSpec((fw, tn), lambda i, j: (0, j)),
                  pl.BlockSpec((tm, tn), lambda i, j: (i, ga0 + j)),
                  pl.BlockSpec((tm, tn), lambda i, j: (i, gf0 + j))],
        out_specs=pl.BlockSpec((tm, tn), lambda i, j: (i, j)),
        out_shape=jax.ShapeDtypeStruct((t, d), BF16),
        compiler_params=_params(("arbitrary", "arbitrary"), 48),
        name="merge",
    )(att, four, wa, wf, proj, proj)


def _outproj_kernel(m_ref, w_ref, x_ref, g_ref, o_ref):
    acc = jnp.dot(m_ref[...], w_ref[...], preferred_element_type=F32)
    o_ref[...] = x_ref[...] + g_ref[0] * acc


def _outproj(merged, w_out, x2d, g1, seq):
    t, d = merged.shape
    tm, tn = 512, 1024
    per_b = seq // tm
    return pl.pallas_call(
        _outproj_kernel,
        grid=(t // tm, d // tn),
        in_specs=[pl.BlockSpec((tm, d), lambda i, j: (i, 0)),
                  pl.BlockSpec((d, tn), lambda i, j: (0, j)),
                  pl.BlockSpec((tm, tn), lambda i, j: (i, j)),
                  pl.BlockSpec((1, 1, tn), lambda i, j: (i // per_b, 0, j))],
        out_specs=pl.BlockSpec((tm, tn), lambda i, j: (i, j)),
        out_shape=jax.ShapeDtypeStruct((t, d), F32),
        compiler_params=_params(("arbitrary", "arbitrary"), 48),
        name="outproj",
    )(merged, w_out, x2d, g1)


def _top_values(s, k, store):
    n = s.shape[0]
    iota = lax.broadcasted_iota(jnp.int32, s.shape, 0)
    cur = s
    for r in range(k):
        m = jnp.max(cur, axis=0, keepdims=True)
        store(r, m)
        first = jnp.min(jnp.where(cur == m, iota, n), axis=0, keepdims=True)
        cur = jnp.where(iota == first, NEG_INF, cur)


def _top_values_distinct(s, k, store):
    cur = s
    for r in range(k):
        m = jnp.max(cur, axis=0, keepdims=True)
        store(r, m)
        cur = jnp.where(cur == m, NEG_INF, cur)
    return jnp.sum(jnp.where(cur == NEG_INF, 1.0, 0.0), axis=0, keepdims=True)


def _peer_score_kernel(h_ref, wq_ref, keys_ref, g0_ref, phi_ref, e1_ref, top_scr, best_scr):
    k1 = PEER_TOPK + 1
    qt = jnp.dot(wq_ref[...], h_ref[...], preferred_element_type=F32).astype(BF16)
    top_scr[...] = jnp.full(top_scr.shape, NEG_INF, F32)
    pad_rows = top_scr.shape[1] - k1
    s = [jnp.dot(keys_ref[0, p], qt[p * PEER_HALF:(p + 1) * PEER_HALF], preferred_element_type=F32)
         for p in range(2)]

    def store_best(r, row):
        best_scr[r:r + 1, :] = row

    def extract(top_fn):
        results = []
        for p in range(2):
            def store(r, row, p=p):
                top_scr[p, r:r + 1, :] = row
            results.append(top_fn(s[p], k1, store))
        cand = jnp.concatenate([top_scr[0, 0:1, :] + top_scr[1]]
                               + [top_scr[0, i:i + 1, :] + top_scr[1, 0:8, :] for i in range(1, k1)], axis=0)
        results.append(top_fn(cand, k1, store_best))
        return results

    n0, n1, nc = extract(_top_values_distinct)
    suspect = jnp.where((n0 != k1) | (n1 != k1) | (nc != k1 + pad_rows), 1.0, 0.0)

    @pl.when(jnp.max(suspect) > 0.0)
    def _():
        extract(_top_values)

    best = best_scr[0:PEER_TOPK]
    z = jnp.sum(jnp.exp(best - best[0:1]), axis=0, keepdims=True)
    tau = 0.5 * (best_scr[PEER_TOPK - 1:PEER_TOPK] + best_scr[PEER_TOPK:k1])
    max0 = top_scr[0, 0:1, :]
    max1 = top_scr[1, 0:1, :]
    g0_ref[0] = jnp.exp(s[0] - max0) / z
    phi_ref[0] = jnp.exp((tau - max1) - s[0])
    e1_ref[0] = jnp.exp(s[1] - max1)


def _peer_scores(h2t, wq_t, keys):
    d, t = h2t.shape
    tm = 512
    top_rows = 24
    s_shape = jax.ShapeDtypeStruct((PEER_HEADS, PEER_NKEYS, t), F32)
    s_blk = pl.BlockSpec((1, PEER_NKEYS, tm), lambda i, h: (h, 0, i))
    return pl.pallas_call(
        _peer_score_kernel,
        grid=(t // tm, PEER_HEADS),
        in_specs=[pl.BlockSpec((d, tm), lambda i, h: (0, i)),
                  pl.BlockSpec((2 * PEER_HALF, d), lambda i, h: (h, 0)),
                  pl.BlockSpec((1, 2, PEER_NKEYS, PEER_HALF), lambda i, h: (h, 0, 0, 0))],
        out_specs=[s_blk, s_blk, s_blk],
        out_shape=[s_shape, s_shape, s_shape],
        scratch_shapes=[pltpu.VMEM((2, top_rows, tm), F32), pltpu.VMEM((top_rows, tm), F32)],
        compiler_params=_params(("arbitrary", "arbitrary"), 40),
        name="peer_scores",
    )(h2t, wq_t, keys)


def _peer_w_kernel(h_ref, hs_ref, u_ref, g0_ref, phi_ref, e1_ref, w_ref, wmax_ref, st_scr, *, n_tiles):
    j = pl.program_id(1)
    tm, te = w_ref.shape
    rows = te // PEER_NKEYS

    @pl.when(j == 0)
    def _():
        st_scr[1] = jnp.zeros(st_scr.shape[1:], F32)
        wmax_ref[...] = jnp.zeros(wmax_ref.shape, F32)

    gate_tile = jnp.clip(j - 1, 0, n_tiles - 1)

    half = tm // 2

    def step(par):
        def body(a, carry):
            ts = pl.multiple_of(a * half, half)
            st_scr[par, :, pl.ds(ts, half)] = jnp.dot(u_ref[...], h_ref[:, pl.ds(ts, half)],
                                                      preferred_element_type=F32)
            inv_scale = hs_ref[:, pl.ds(ts, half)]
            for r in range(rows):
                n0 = gate_tile * rows + r
                ks = slice(r * PEER_NKEYS, (r + 1) * PEER_NKEYS)
                st = st_scr[1 - par, ks, pl.ds(ts, half)] * inv_scale
                act = 0.5 * st * (1.0 + lax.erf(st * math.sqrt(0.5)))
                gate = None
                for hd in range(PEER_HEADS):
                    e1 = e1_ref[hd, :, pl.ds(ts, half)]
                    g = jnp.where(e1 >= phi_ref[hd, pl.ds(n0, 1), pl.ds(ts, half)],
                                  e1 * g0_ref[hd, pl.ds(n0, 1), pl.ds(ts, half)], 0.0)
                    gate = g if gate is None else gate + g
                wt = act * gate
                w_ref[pl.ds(ts, half), ks] = wt.T.astype(BF16)
                wmax_ref[:, pl.ds(ts, half)] = jnp.maximum(wmax_ref[:, pl.ds(ts, half)],
                                                           jnp.max(jnp.abs(wt), axis=0, keepdims=True))
            return carry

        lax.fori_loop(0, 2, body, 0)

    @pl.when(j % 2 == 0)
    def _():
        step(0)

    @pl.when(j % 2 == 1)
    def _():
        step(1)


def _peer_weights(h2t, hs, u_q, g0, phi, e1):
    d, t = h2t.shape
    n_exp = u_q.shape[0]
    tm, te = 512, 512
    n_tiles = n_exp // te
    once = pl.Buffered(1)
    sc_blk = pl.BlockSpec((PEER_HEADS, PEER_NKEYS, tm), lambda i, j: (0, 0, i), pipeline_mode=once)
    return pl.pallas_call(
        functools.partial(_peer_w_kernel, n_tiles=n_tiles),
        grid=(t // tm, n_tiles + 1),
        in_specs=[pl.BlockSpec((d, tm), lambda i, j: (0, i), pipeline_mode=once),
                  pl.BlockSpec((1, tm), lambda i, j: (0, i)),
                  pl.BlockSpec((te, d), lambda i, j: (jnp.minimum(j, n_tiles - 1), 0)),
                  sc_blk, sc_blk, sc_blk],
        out_specs=[pl.BlockSpec((tm, te), lambda i, j: (i, jnp.maximum(j - 1, 0))),
                   pl.BlockSpec((1, tm), lambda i, j: (0, i))],
        out_shape=[jax.ShapeDtypeStruct((t, n_exp), BF16), jax.ShapeDtypeStruct((1, t), F32)],
        scratch_shapes=[pltpu.VMEM((2, te, tm), F32)],
        compiler_params=_params(("arbitrary", "arbitrary"), 48),
        name="peer_w",
    )(h2t, hs, u_q, g0, phi, e1)


def _peer_v_kernel(w_ref, sc_ref, dq_ref, v_ref, o_ref, wq_scr, *, chunk):
    @pl.when(pl.program_id(1) == 0)
    def _():
        for c in range(w_ref.shape[1] // chunk):
            cols = slice(c * chunk, (c + 1) * chunk)
            wq_scr[:, cols] = (w_ref[:, cols].astype(F32) * sc_ref[...]).astype(FP8)

    o_ref[...] = jnp.dot(wq_scr[...], v_ref[...], preferred_element_type=F32) * dq_ref[...]


def _peer_values(w, wmax, v_q, v_inv):
    t, n_exp = w.shape
    d = v_q.shape[1]
    tm, tn = 512, min(256, d)
    wmax_col = jnp.maximum(wmax.reshape(t, 1), FP8_TINY)
    scale = FP8_TARGET / wmax_col
    dequant = wmax_col * (v_inv / FP8_TARGET)
    col = pl.BlockSpec((tm, 1), lambda i, j: (i, 0))
    return pl.pallas_call(
        functools.partial(_peer_v_kernel, chunk=min(2048, n_exp)),
        grid=(t // tm, d // tn),
        in_specs=[pl.BlockSpec((tm, n_exp), lambda i, j: (i, 0), pipeline_mode=pl.Buffered(1)),
                  col, col,
                  pl.BlockSpec((n_exp, tn), lambda i, j: (0, j))],
        out_specs=pl.BlockSpec((tm, tn), lambda i, j: (i, j)),
        out_shape=jax.ShapeDtypeStruct((t, d), F32),
        scratch_shapes=[pltpu.VMEM((tm, n_exp), FP8)],
        compiler_params=_params(("arbitrary", "arbitrary"), 48),
        name="peer_v",
    )(w, scale, dequant, v_q)


def _final_kernel(x_ref, p_ref, g2_ref, fg_ref, o_ref):
    x = x_ref[...] + g2_ref[0] * p_ref[...]
    ms = jnp.mean(x * x, axis=-1, keepdims=True)
    o_ref[...] = x * lax.rsqrt(ms + EPS) * fg_ref[...]


def _final(x1, peer, g2, final_g, seq):
    t, d = x1.shape
    tm = 256
    per_b = seq // tm
    row = pl.BlockSpec((tm, d), lambda i: (i, 0))
    return pl.pallas_call(
        _final_kernel,
        grid=(t // tm,),
        in_specs=[row, row,
                  pl.BlockSpec((1, 1, d), lambda i: (i // per_b, 0, 0)),
                  pl.BlockSpec((1, d), lambda i: (0, 0))],
        out_specs=row,
        out_shape=jax.ShapeDtypeStruct((t, d), F32),
        compiler_params=_params(("arbitrary",), 48),
        name="final",
    )(x1, peer, g2, final_g.reshape(1, d))


def _rope_tables(seq):
    rows = seq // GRID_W
    row_ids = jnp.repeat(jnp.arange(rows), GRID_W).astype(F32)
    col_ids = jnp.tile(jnp.arange(GRID_W), rows).astype(F32)
    axis_dim = HEAD_DIM // 2
    inv_freq = ROPE_THETA ** (-jnp.arange(0, axis_dim, 2, dtype=F32) / axis_dim)
    ang = jnp.concatenate([row_ids[:, None] * inv_freq, col_ids[:, None] * inv_freq], axis=-1)
    cos, sin = jnp.cos(ang), jnp.sin(ang)
    return jnp.concatenate([cos, cos], axis=-1), jnp.concatenate([-sin, sin], axis=-1)


def _trunk(x, mod, wts, dims):
    nb, seq, d = x.shape
    att_w, kv_w, f_w, gd = dims
    x2d = x.reshape(nb * seq, d)
    sh1, sc1, g1, sh2, sc2, g2 = [mod[:, k].reshape(nb, 1, d) for k in range(6)]
    cc, ss = _rope_tables(seq)

    h = _ln_mod(x2d, wts["norm1_g"], sc1, sh1, seq, transpose=False)
    proj = _inproj(h, wts["w_main"], wts["qg"], wts["kg"], cc, ss, seq, att_w, kv_w)
    p, q = _fproj(h, wts["w_f"], wts["cs_ch"], gd)
    att = _attention(proj, nb, seq, att_w, kv_w)
    four = _seq_dft(p, q, nb, seq, gd)
    merged = _merge(att, four, wts["wa"], wts["wf"], proj, att_w + 2 * kv_w)
    x1 = _outproj(merged, wts["w_out"], x2d, g1, seq)

    h2t, h2t_q, h2_inv = _ln_mod(x1, wts["norm2_g"], sc2, sh2, seq, transpose=True)
    g0, phi, e1 = _peer_scores(h2t, wts["wq_t"], wts["keys"])
    w, wmax = _peer_weights(h2t_q, h2_inv * wts["u_inv"], wts["u_q"], g0, phi, e1)
    peer = _peer_values(w, wmax, wts["v_q"], wts["v_inv"])
    y = _final(x1, peer, g2, wts["final_g"], seq)
    return y.reshape(nb, seq, d)


def kernel(x_prompt, x_sample, c_prompt, c_sample, w_ada, b_ada, norm1_g, norm2_g, w_in, q_norm_g, k_norm_g, w_attn_br, w_four_br, w_out, w_peer_q, peer_keys, peer_u, peer_v, final_g):
    d = x_prompt.shape[-1]
    assert w_ada.shape[0] == 1, "single-layer trunk"
    att_w = w_attn_br.shape[1]
    f_w = w_four_br.shape[1]
    kv_w = (w_in.shape[2] - att_w - f_w - 2 * d) // 2
    gd = f_w // N_FGROUPS
    dims = (att_w, kv_w, f_w, gd)

    perm = np.concatenate([np.arange(0, HEAD_DIM, 2), np.arange(1, HEAD_DIM, 2)])
    w = w_in[0]

    def deinterleave(cols):
        return cols.reshape(d, -1, HEAD_DIM)[:, :, perm].reshape(d, -1)

    s_q, s_k, s_v, s_f = att_w, att_w + kv_w, att_w + 2 * kv_w, att_w + 2 * kv_w + f_w
    w_main = jnp.concatenate([deinterleave(w[:, :s_q]), deinterleave(w[:, s_q:s_k]), w[:, s_k:s_v],
                              w[:, s_f:]], axis=1).astype(BF16)
    cch, sch = _dft_tables(gd)
    u_max = jnp.maximum(jnp.max(jnp.abs(peer_u[0])), FP8_TINY)
    v_max = jnp.maximum(jnp.max(jnp.abs(peer_v[0])), FP8_TINY)
    wts = {
        "norm1_g": norm1_g[0], "norm2_g": norm2_g[0], "final_g": final_g,
        "w_main": w_main,
        "w_f": w[:, s_v:s_f].astype(BF16),
        "cs_ch": jnp.asarray(np.concatenate([cch, sch], axis=1), BF16),
        "qg": q_norm_g[0][perm].reshape(1, HEAD_DIM),
        "kg": k_norm_g[0][perm].reshape(1, HEAD_DIM),
        "wa": w_attn_br[0].astype(BF16), "wf": w_four_br[0].astype(BF16),
        "w_out": w_out[0].astype(BF16),
        "wq_t": w_peer_q[0].T.astype(BF16),
        "keys": peer_keys[0].astype(BF16),
        "u_q": (peer_u[0] * (FP8_TARGET / u_max)).astype(FP8), "u_inv": u_max * (1.0 / FP8_TARGET),
        "v_q": (peer_v[0] * (FP8_TARGET / v_max)).astype(FP8), "v_inv": v_max * (1.0 / FP8_TARGET),
    }

    nbp, nbs = c_prompt.shape[0], c_sample.shape[0]
    c_all = jnp.concatenate([c_prompt, c_sample], axis=0)
    c_pad = jnp.pad(c_all, ((0, (-c_all.shape[0]) % 8), (0, 0)))
    mod = _modulation(c_pad, w_ada[0], b_ada[0]).reshape(c_pad.shape[0], 6, d)

    y_prompt = _trunk(x_prompt, mod[:nbp], wts, dims)
    y_sample = _trunk(x_sample, mod[nbp:nbp + nbs], wts, dims)
    return (y_prompt, y_sample)
```

```python
import functools
import math

import jax
import jax.numpy as jnp
import numpy as np
from jax import lax
from jax.experimental import pallas as pl
from jax.experimental.pallas import tpu as pltpu

F32 = jnp.float32
BF16 = jnp.bfloat16
FP8 = jnp.float8_e4m3fn
FP8_TARGET = 256.0
FP8_TINY = 1e-30

HEAD_DIM = 128
GQA_GROUP = 4
Q_PER_TILE = GQA_GROUP * HEAD_DIM
ROPE_THETA = 10000.0
GRID_W = 64
N_FGROUPS = 8
PEER_HEADS = 8
PEER_NKEYS = 128
PEER_HALF = 128
PEER_TOPK = 16
EPS = 1e-6
LANES = 128
NEG_INF = float("-inf")
ONES_ROWS = 16
LOG2E = math.log2(math.e)
MIB = 1024 * 1024


def _params(sem, vmem_mib):
    return pltpu.CompilerParams(dimension_semantics=sem, vmem_limit_bytes=vmem_mib * MIB)


def _mod_kernel(c_ref, w_ref, b_ref, o_ref):
    c = c_ref[...]
    a = c * jax.nn.sigmoid(c)
    o_ref[...] = jnp.dot(a, w_ref[...], preferred_element_type=F32,
                         precision=lax.Precision.HIGHEST) + b_ref[...]


def _modulation(c_pad, w_ada, b_ada):
    rows, d = c_pad.shape
    n = w_ada.shape[1]
    tn = 512
    return pl.pallas_call(
        _mod_kernel,
        grid=(n // tn,),
        in_specs=[pl.BlockSpec((rows, d), lambda j: (0, 0)),
                  pl.BlockSpec((d, tn), lambda j: (0, j)),
                  pl.BlockSpec((1, tn), lambda j: (0, j))],
        out_specs=pl.BlockSpec((rows, tn), lambda j: (0, j)),
        out_shape=jax.ShapeDtypeStruct((rows, n), F32),
        compiler_params=_params(("arbitrary",), 40),
        name="mod",
    )(c_pad, w_ada, b_ada.reshape(1, n))


def _ln_rows(x_ref, g_ref, sc_ref, sh_ref):
    x = x_ref[...]
    ms = jnp.mean(x * x, axis=-1, keepdims=True)
    y = x * lax.rsqrt(ms + EPS) * g_ref[...]
    return y * (1.0 + sc_ref[0]) + sh_ref[0]


def _ln_kernel(x_ref, g_ref, sc_ref, sh_ref, o_ref):
    o_ref[...] = _ln_rows(x_ref, g_ref, sc_ref, sh_ref).astype(BF16)


def _ln_t_kernel(x_ref, g_ref, sc_ref, sh_ref, o_ref, q_ref, s_ref, ht_scr, *, chunk):
    h = _ln_rows(x_ref, g_ref, sc_ref, sh_ref)
    amax = jnp.full(s_ref.shape, FP8_TINY, F32)
    for c in range(h.shape[1] // chunk):
        rows = slice(c * chunk, (c + 1) * chunk)
        blk = h[:, rows].T
        ht_scr[rows, :] = blk
        o_ref[rows, :] = blk.astype(BF16)
        amax = jnp.maximum(amax, jnp.max(jnp.abs(blk), axis=0, keepdims=True))
    q_ref[...] = (ht_scr[...] * (FP8_TARGET / amax)).astype(FP8)
    s_ref[...] = amax * (1.0 / FP8_TARGET)


def _ln_mod(x2d, g, sc, sh, seq, *, transpose):
    t, d = x2d.shape
    tm = 256
    per_b = seq // tm
    in_specs = [pl.BlockSpec((tm, d), lambda i: (i, 0)),
                pl.BlockSpec((1, d), lambda i: (0, 0)),
                pl.BlockSpec((1, 1, d), lambda i: (i // per_b, 0, 0)),
                pl.BlockSpec((1, 1, d), lambda i: (i // per_b, 0, 0))]
    args = (x2d, g.reshape(1, d), sc, sh)
    if not transpose:
        return pl.pallas_call(
            _ln_kernel, grid=(t // tm,), in_specs=in_specs,
            out_specs=pl.BlockSpec((tm, d), lambda i: (i, 0)),
            out_shape=jax.ShapeDtypeStruct((t, d), BF16),
            compiler_params=_params(("arbitrary",), 40), name="ln",
        )(*args)
    col_blk = pl.BlockSpec((d, tm), lambda i: (0, i))
    return pl.pallas_call(
        functools.partial(_ln_t_kernel, chunk=min(512, d)),
        grid=(t // tm,), in_specs=in_specs,
        out_specs=[col_blk, col_blk, pl.BlockSpec((1, tm), lambda i: (0, i))],
        out_shape=[jax.ShapeDtypeStruct((d, t), BF16), jax.ShapeDtypeStruct((d, t), FP8),
                   jax.ShapeDtypeStruct((1, t), F32)],
        scratch_shapes=[pltpu.VMEM((d, tm), F32)],
        compiler_params=_params(("arbitrary",), 40), name="ln_t",
    )(*args)


def _norm_rope(acc, g, cc, ss, scale, o_ref):
    for hd in range(acc.shape[1] // HEAD_DIM):
        xh = acc[:, hd * HEAD_DIM:(hd + 1) * HEAD_DIM]
        ms = jnp.mean(xh * xh, axis=-1, keepdims=True)
        y = xh * lax.rsqrt(ms + EPS) * g
        y = y * cc + pltpu.roll(y, HEAD_DIM // 2, axis=1) * ss
        if scale != 1.0:
            y = y * scale
        o_ref[:, hd * HEAD_DIM:(hd + 1) * HEAD_DIM] = y.astype(BF16)


def _inproj_kernel(h_ref, w_ref, qg_ref, kg_ref, cc_ref, ss_ref, o_ref, *, nq, nk, nv, q_scale):
    j = pl.program_id(1)
    acc = jnp.dot(h_ref[...], w_ref[...], preferred_element_type=F32)

    @pl.when(j < nq)
    def _():
        _norm_rope(acc, qg_ref[...], cc_ref[...], ss_ref[...], q_scale, o_ref)

    @pl.when(jnp.logical_and(j >= nq, j < nq + nk))
    def _():
        _norm_rope(acc, kg_ref[...], cc_ref[...], ss_ref[...], 1.0, o_ref)

    @pl.when(jnp.logical_and(j >= nq + nk, j < nq + nk + nv))
    def _():
        o_ref[...] = acc.astype(BF16)

    @pl.when(j >= nq + nk + nv)
    def _():
        o_ref[...] = jax.nn.sigmoid(acc).astype(BF16)


def _inproj(h, w_main, qg, kg, cc, ss, seq, att_w, kv_w):
    t, d = h.shape
    wtot = w_main.shape[1]
    tm = 512
    tn = min(1024, kv_w)
    per_b = seq // tm
    kern = functools.partial(_inproj_kernel, nq=att_w // tn, nk=kv_w // tn, nv=kv_w // tn,
                             q_scale=HEAD_DIM ** -0.5 * LOG2E)
    return pl.pallas_call(
        kern,
        grid=(t // tm, wtot // tn),
        in_specs=[pl.BlockSpec((tm, d), lambda i, j: (i, 0)),
                  pl.BlockSpec((d, tn), lambda i, j: (0, j)),
                  pl.BlockSpec((1, HEAD_DIM), lambda i, j: (0, 0)),
                  pl.BlockSpec((1, HEAD_DIM), lambda i, j: (0, 0)),
                  pl.BlockSpec((tm, HEAD_DIM), lambda i, j: (i % per_b, 0)),
                  pl.BlockSpec((tm, HEAD_DIM), lambda i, j: (i % per_b, 0))],
        out_specs=pl.BlockSpec((tm, tn), lambda i, j: (i, j)),
        out_shape=jax.ShapeDtypeStruct((t, wtot), BF16),
        compiler_params=_params(("arbitrary", "arbitrary"), 48),
        name="inproj",
    )(h, w_main, qg, kg, cc, ss)


def _fproj_kernel(h_ref, w_ref, cs_ref, p_ref, q_ref, *, gd):
    acc = jnp.dot(h_ref[...], w_ref[...], preferred_element_type=F32)
    for g in range(acc.shape[1] // gd):
        u = acc[:, g * gd:(g + 1) * gd].astype(BF16)
        pq = jnp.dot(u, cs_ref[...], preferred_element_type=F32)
        p_ref[:, g * gd:(g + 1) * gd] = pq[:, :gd].astype(BF16)
        q_ref[:, g * gd:(g + 1) * gd] = pq[:, gd:].astype(BF16)


def _fproj(h, w_f, cs, gd):
    t, d = h.shape
    fw = w_f.shape[1]
    tm = 512
    tn = min(1024, fw)
    out = jax.ShapeDtypeStruct((t, fw), BF16)
    return pl.pallas_call(
        functools.partial(_fproj_kernel, gd=gd),
        grid=(t // tm, fw // tn),
        in_specs=[pl.BlockSpec((tm, d), lambda i, j: (i, 0)),
                  pl.BlockSpec((d, tn), lambda i, j: (0, j)),
                  pl.BlockSpec((gd, 2 * gd), lambda i, j: (0, 0))],
        out_specs=[pl.BlockSpec((tm, tn), lambda i, j: (i, j)),
                   pl.BlockSpec((tm, tn), lambda i, j: (i, j))],
        out_shape=[out, out],
        compiler_params=_params(("arbitrary", "arbitrary"), 48),
        name="fproj",
    )(h, w_f, cs)


def _attn_kernel(q_ref, k_ref, v_ref, o_ref, m_scr, acc_scr, qt_scr, sa_scr, sb_scr, *, tk):
    tq = q_ref.shape[0]
    seq = k_ref.shape[0]
    n_chunks = seq // tk
    for g in range(GQA_GROUP):
        qt_scr[:, g * tq:(g + 1) * tq] = q_ref[:, g * HEAD_DIM:(g + 1) * HEAD_DIM].astype(F32).T.astype(BF16)
    m_scr[...] = jnp.full(m_scr.shape, NEG_INF, F32)
    acc_scr[...] = jnp.zeros(acc_scr.shape, F32)
    ones = jnp.ones((ONES_ROWS, tk), BF16)

    def scores(c, st_ref):
        start = pl.multiple_of(c * tk, tk)
        st_ref[...] = jnp.dot(k_ref[pl.ds(start, tk), :], qt_scr[...], preferred_element_type=F32)

    def consume(c, st_ref):
        start = pl.multiple_of(c * tk, tk)
        v = v_ref[pl.ds(start, tk), :]
        st = st_ref[...]
        m_prev = m_scr[...]
        m_next = jnp.maximum(m_prev, jnp.max(st, axis=0, keepdims=True))
        p = jnp.exp2(st - m_next).astype(BF16)
        alpha = jnp.exp2(m_prev - m_next)
        vt = jnp.concatenate([v.astype(F32).T.astype(BF16), ones], axis=0)
        acc_scr[...] = alpha * acc_scr[...] + jnp.dot(vt, p, preferred_element_type=F32)
        m_scr[...] = m_next

    scores(0, sa_scr)

    def body(i, carry):
        c = 2 * i
        scores(c + 1, sb_scr)
        consume(c, sa_scr)
        scores(c + 2, sa_scr)
        consume(c + 1, sb_scr)
        return carry

    lax.fori_loop(0, n_chunks // 2 - 1, body, 0)
    scores(n_chunks - 1, sb_scr)
    consume(n_chunks - 2, sa_scr)
    consume(n_chunks - 1, sb_scr)
    acc = acc_scr[...]
    out_t = acc[:HEAD_DIM] / acc[HEAD_DIM:HEAD_DIM + 1]
    for g in range(GQA_GROUP):
        o_ref[:, g * HEAD_DIM:(g + 1) * HEAD_DIM] = out_t[:, g * tq:(g + 1) * tq].T.astype(BF16)


def _attention(proj, nb, seq, att_w, kv_w):
    t = proj.shape[0]
    tq = 512
    tk = min(1024, seq // 2)
    assert seq % (2 * tk) == 0
    n_kv = kv_w // HEAD_DIM
    per_b = seq // tq
    k_off = att_w // HEAD_DIM
    v_off = (att_w + kv_w) // HEAD_DIM
    cols = GQA_GROUP * tq
    return pl.pallas_call(
        functools.partial(_attn_kernel, tk=tk),
        grid=(nb, n_kv, per_b),
        in_specs=[pl.BlockSpec((tq, Q_PER_TILE), lambda b, g, i: (b * per_b + i, g)),
                  pl.BlockSpec((seq, HEAD_DIM), lambda b, g, i: (b, k_off + g)),
                  pl.BlockSpec((seq, HEAD_DIM), lambda b, g, i: (b, v_off + g))],
        out_specs=pl.BlockSpec((tq, Q_PER_TILE), lambda b, g, i: (b * per_b + i, g)),
        out_shape=jax.ShapeDtypeStruct((t, att_w), BF16),
        scratch_shapes=[pltpu.VMEM((1, cols), F32),
                        pltpu.VMEM((HEAD_DIM + ONES_ROWS, cols), F32),
                        pltpu.VMEM((HEAD_DIM, cols), BF16),
                        pltpu.VMEM((tk, cols), F32),
                        pltpu.VMEM((tk, cols), F32)],
        compiler_params=_params(("arbitrary", "arbitrary", "arbitrary"), 48),
        name="attn",
    )(proj, proj, proj)


def _fft_a_kernel(m_ref, p_ref, q_ref, twc_ref, tws_ref, ar_ref, ai_ref, *, fw):
    n1 = p_ref.shape[1]
    rhs = jnp.concatenate([p_ref[0], q_ref[0]], axis=0)
    a = jnp.dot(m_ref[...], rhs, preferred_element_type=F32)
    rep = fw // LANES
    for u in range(p_ref.shape[2] // fw):
        ar = a[:n1, u * fw:(u + 1) * fw]
        ai = a[n1:, u * fw:(u + 1) * fw]
        c = jnp.tile(twc_ref[u], (1, rep))
        s = jnp.tile(tws_ref[u], (1, rep))
        ar_ref[0, :, u * fw:(u + 1) * fw] = (ar * c + ai * s).astype(BF16)
        ai_ref[0, :, u * fw:(u + 1) * fw] = (ai * c - ar * s).astype(BF16)


def _fft_c_kernel(cs_ref, ar_ref, ai_ref, o_ref, *, scale):
    rhs = jnp.concatenate([ar_ref[0, 0], ai_ref[0, 0]], axis=0)
    o_ref[0] = (jnp.dot(cs_ref[...], rhs, preferred_element_type=F32) * scale).astype(BF16)


def _dft_tables(n):
    idx = (np.arange(n)[:, None] * np.arange(n)[None, :]) % n
    ang = 2.0 * np.pi * idx / n
    return np.cos(ang), np.sin(ang)


def _seq_dft(p, q, nb, seq, gd):
    fw = p.shape[1]
    n1 = 64 if seq >= 4096 else 16
    n2 = seq // n1
    c1, s1 = _dft_tables(n1)
    c2, s2 = _dft_tables(n2)
    m_a = jnp.asarray(np.block([[c1, -s1], [-s1, -c1]]), BF16)
    cs_c = jnp.asarray(np.concatenate([c2, s2], axis=1), BF16)
    tw = 2.0 * np.pi * (np.arange(n2)[:, None] * np.arange(n1)[None, :]) / seq
    twc = jnp.asarray(np.broadcast_to(np.cos(tw)[:, :, None], (n2, n1, LANES)), F32)
    tws = jnp.asarray(np.broadcast_to(np.sin(tw)[:, :, None], (n2, n1, LANES)), F32)

    nt2 = 4
    tc = nt2 * fw
    p3 = p.reshape(nb, n1, n2 * fw)
    q3 = q.reshape(nb, n1, n2 * fw)
    a_shape = jax.ShapeDtypeStruct((nb, n1, n2 * fw), BF16)
    blk = pl.BlockSpec((1, n1, tc), lambda b, j: (b, 0, j))
    tw_blk = pl.BlockSpec((nt2, n1, LANES), lambda b, j: (j, 0, 0))
    ar, ai = pl.pallas_call(
        functools.partial(_fft_a_kernel, fw=fw),
        grid=(nb, n2 // nt2),
        in_specs=[pl.BlockSpec((2 * n1, 2 * n1), lambda b, j: (0, 0)), blk, blk, tw_blk, tw_blk],
        out_specs=[blk, blk],
        out_shape=[a_shape, a_shape],
        compiler_params=_params(("arbitrary", "arbitrary"), 40),
        name="fft_a",
    )(m_a, p3, q3, twc, tws)

    in_blk = pl.BlockSpec((1, 1, n2, fw), lambda b, k: (b, k, 0, 0))
    out = pl.pallas_call(
        functools.partial(_fft_c_kernel, scale=1.0 / math.sqrt(seq * gd)),
        grid=(nb, n1),
        in_specs=[pl.BlockSpec((n2, 2 * n2), lambda b, k: (0, 0)), in_blk, in_blk],
        out_specs=pl.BlockSpec((1, n2, fw), lambda b, k: (b, 0, k)),
        out_shape=jax.ShapeDtypeStruct((nb, n2, n1 * fw), BF16),
        compiler_params=_params(("arbitrary", "arbitrary"), 40),
        name="fft_c",
    )(cs_c, ar.reshape(nb, n1, n2, fw), ai.reshape(nb, n1, n2, fw))
    return out.reshape(nb * seq, fw)


def _merge_kernel(a_ref, f_ref, wa_ref, wf_ref, ga_ref, gf_ref, o_ref):
    a_br = jnp.dot(a_ref[...], wa_ref[...], preferred_element_type=F32)
    f_br = jnp.dot(f_ref[...], wf_ref[...], preferred_element_type=F32)
    o_ref[...] = (ga_ref[...].astype(F32) * a_br + gf_ref[...].astype(F32) * f_br).astype(BF16)


def _merge(att, four, wa, wf, proj, gate_off):
    t, aw = att.shape
    fw = four.shape[1]
    d = wa.shape[1]
    tm, tn = 512, 512
    ga0 = gate_off // tn
    gf0 = (gate_off + d) // tn
    return pl.pallas_call(
        _merge_kernel,
        grid=(t // tm, d // tn),
        in_specs=[pl.BlockSpec((tm, aw), lambda i, j: (i, 0)),
                  pl.BlockSpec((tm, fw), lambda i, j: (i, 0)),
                  pl.BlockSpec((aw, tn), lambda i, j: (0, j)),
                  pl.BlockSpec((fw, tn), lambda i, j: (0, j)),
                  pl.BlockSpec((tm, tn), lambda i, j: (i, ga0 + j)),
                  pl.BlockSpec((tm, tn), lambda i, j: (i, gf0 + j))],
        out_specs=pl.BlockSpec((tm, tn), lambda i, j: (i, j)),
        out_shape=jax.ShapeDtypeStruct((t, d), BF16),
        compiler_params=_params(("arbitrary", "arbitrary"), 48),
        name="merge",
    )(att, four, wa, wf, proj, proj)


def _outproj_kernel(m_ref, w_ref, x_ref, g_ref, o_ref):
    acc = jnp.dot(m_ref[...], w_ref[...], preferred_element_type=F32)
    o_ref[...] = x_ref[...] + g_ref[0] * acc


def _outproj(merged, w_out, x2d, g1, seq):
    t, d = merged.shape
    tm, tn = 512, 1024
    per_b = seq // tm
    return pl.pallas_call(
        _outproj_kernel,
        grid=(t // tm, d // tn),
        in_specs=[pl.BlockSpec((tm, d), lambda i, j: (i, 0)),
                  pl.BlockSpec((d, tn), lambda i, j: (0, j)),
                  pl.BlockSpec((tm, tn), lambda i, j: (i, j)),
                  pl.BlockSpec((1, 1, tn), lambda i, j: (i // per_b, 0, j))],
        out_specs=pl.BlockSpec((tm, tn), lambda i, j: (i, j)),
        out_shape=jax.ShapeDtypeStruct((t, d), F32),
        compiler_params=_params(("arbitrary", "arbitrary"), 48),
        name="outproj",
    )(merged, w_out, x2d, g1)


def _top_values(s, k, store):
    n = s.shape[0]
    iota = lax.broadcasted_iota(jnp.int32, s.shape, 0)
    cur = s
    for r in range(k):
        m = jnp.max(cur, axis=0, keepdims=True)
        store(r, m)
        first = jnp.min(jnp.where(cur == m, iota, n), axis=0, keepdims=True)
        cur = jnp.where(iota == first, NEG_INF, cur)


def _top_values_distinct(s, k, store):
    cur = s
    for r in range(k):
        m = jnp.max(cur, axis=0, keepdims=True)
        store(r, m)
        cur = jnp.where(cur == m, NEG_INF, cur)
    return jnp.sum(jnp.where(cur == NEG_INF, 1.0, 0.0), axis=0, keepdims=True)


def _peer_score_kernel(h_ref, wq_ref, keys_ref, g0_ref, phi_ref, e1_ref, top_scr, best_scr):
    k1 = PEER_TOPK + 1
    qt = jnp.dot(wq_ref[...], h_ref[...], preferred_element_type=F32).astype(BF16)
    top_scr[...] = jnp.full(top_scr.shape, NEG_INF, F32)
    pad_rows = top_scr.shape[1] - k1
    s = [jnp.dot(keys_ref[0, p], qt[p * PEER_HALF:(p + 1) * PEER_HALF], preferred_element_type=F32)
         for p in range(2)]

    def store_best(r, row):
        best_scr[r:r + 1, :] = row

    def extract(top_fn):
        results = []
        for p in range(2):
            def store(r, row, p=p):
                top_scr[p, r:r + 1, :] = row
            results.append(top_fn(s[p], k1, store))
        cand = jnp.concatenate([top_scr[0, 0:1, :] + top_scr[1]]
                               + [top_scr[0, i:i + 1, :] + top_scr[1, 0:8, :] for i in range(1, k1)], axis=0)
        results.append(top_fn(cand, k1, store_best))
        return results

    n0, n1, nc = extract(_top_values_distinct)
    suspect = jnp.where((n0 != k1) | (n1 != k1) | (nc != k1 + pad_rows), 1.0, 0.0)

    @pl.when(jnp.max(suspect) > 0.0)
    def _():
        extract(_top_values)

    best = best_scr[0:PEER_TOPK]
    z = jnp.sum(jnp.exp(best - best[0:1]), axis=0, keepdims=True)
    tau = 0.5 * (best_scr[PEER_TOPK - 1:PEER_TOPK] + best_scr[PEER_TOPK:k1])
    max0 = top_scr[0, 0:1, :]
    max1 = top_scr[1, 0:1, :]
    g0_ref[0] = jnp.exp(s[0] - max0) / z
    phi_ref[0] = jnp.exp((tau - max1) - s[0])
    e1_ref[0] = jnp.exp(s[1] - max1).astype(BF16)


def _peer_scores(h2t, wq_t, keys):
    d, t = h2t.shape
    tm = 512
    top_rows = 24
    s_shape = jax.ShapeDtypeStruct((PEER_HEADS, PEER_NKEYS, t), F32)
    s_blk = pl.BlockSpec((1, PEER_NKEYS, tm), lambda i, h: (h, 0, i))
    return pl.pallas_call(
        _peer_score_kernel,
        grid=(t // tm, PEER_HEADS),
        in_specs=[pl.BlockSpec((d, tm), lambda i, h: (0, i)),
                  pl.BlockSpec((2 * PEER_HALF, d), lambda i, h: (h, 0)),
                  pl.BlockSpec((1, 2, PEER_NKEYS, PEER_HALF), lambda i, h: (h, 0, 0, 0))],
        out_specs=[s_blk, s_blk, s_blk],
        out_shape=[s_shape, s_shape, jax.ShapeDtypeStruct(s_shape.shape, BF16)],
        scratch_shapes=[pltpu.VMEM((2, top_rows, tm), F32), pltpu.VMEM((top_rows, tm), F32)],
        compiler_params=_params(("arbitrary", "arbitrary"), 40),
        name="peer_scores",
    )(h2t, wq_t, keys)


def _peer_w_kernel(h_ref, hs_ref, u_ref, g0_ref, phi_ref, e1_ref, w_ref, wmax_ref, st_scr, *, n_tiles):
    j = pl.program_id(1)
    tm, te = w_ref.shape
    rows = te // PEER_NKEYS

    @pl.when(j == 0)
    def _():
        st_scr[1] = jnp.zeros(st_scr.shape[1:], F32)
        wmax_ref[...] = jnp.zeros(wmax_ref.shape, F32)

    gate_tile = jnp.clip(j - 1, 0, n_tiles - 1)

    half = tm // 2

    def step(par):
        def body(a, carry):
            ts = pl.multiple_of(a * half, half)
            st_scr[par, :, pl.ds(ts, half)] = jnp.dot(u_ref[...], h_ref[:, pl.ds(ts, half)],
                                                      preferred_element_type=F32)
            inv_scale = hs_ref[:, pl.ds(ts, half)]
            for r in range(rows):
                n0 = gate_tile * rows + r
                ks = slice(r * PEER_NKEYS, (r + 1) * PEER_NKEYS)
                st = st_scr[1 - par, ks, pl.ds(ts, half)] * inv_scale
                act = 0.5 * st * (1.0 + lax.erf(st * math.sqrt(0.5)))
                gate = None
                for hd in range(PEER_HEADS):
                    e1 = e1_ref[hd, :, pl.ds(ts, half)]
                    phi = phi_ref[hd, pl.ds(n0, 1), pl.ds(ts, half)].astype(BF16)
                    g0 = g0_ref[hd, pl.ds(n0, 1), pl.ds(ts, half)].astype(BF16)
                    g = jnp.where(e1 >= phi, e1 * g0, jnp.zeros_like(e1))
                    gate = g if gate is None else gate + g
                wt = act * gate.astype(F32)
                w_ref[pl.ds(ts, half), ks] = wt.T.astype(BF16)
                wmax_ref[:, pl.ds(ts, half)] = jnp.maximum(wmax_ref[:, pl.ds(ts, half)],
                                                           jnp.max(jnp.abs(wt), axis=0, keepdims=True))
            return carry

        lax.fori_loop(0, 2, body, 0)

    @pl.when(j % 2 == 0)
    def _():
        step(0)

    @pl.when(j % 2 == 1)
    def _():
        step(1)


def _peer_weights(h2t, hs, u_q, g0, phi, e1):
    d, t = h2t.shape
    n_exp = u_q.shape[0]
    tm, te = 512, 512
    n_tiles = n_exp // te
    once = pl.Buffered(1)
    sc_blk = pl.BlockSpec((PEER_HEADS, PEER_NKEYS, tm), lambda i, j: (0, 0, i), pipeline_mode=once)
    return pl.pallas_call(
        functools.partial(_peer_w_kernel, n_tiles=n_tiles),
        grid=(t // tm, n_tiles + 1),
        in_specs=[pl.BlockSpec((d, tm), lambda i, j: (0, i), pipeline_mode=once),
                  pl.BlockSpec((1, tm), lambda i, j: (0, i)),
                  pl.BlockSpec((te, d), lambda i, j: (jnp.minimum(j, n_tiles - 1), 0)),
                  sc_blk, sc_blk, sc_blk],
        out_specs=[pl.BlockSpec((tm, te), lambda i, j: (i, jnp.maximum(j - 1, 0))),
                   pl.BlockSpec((1, tm), lambda i, j: (0, i))],
        out_shape=[jax.ShapeDtypeStruct((t, n_exp), BF16), jax.ShapeDtypeStruct((1, t), F32)],
        scratch_shapes=[pltpu.VMEM((2, te, tm), F32)],
        compiler_params=_params(("arbitrary", "arbitrary"), 48),
        name="peer_w",
    )(h2t, hs, u_q, g0, phi, e1)


def _peer_v_kernel(w_ref, sc_ref, dq_ref, v_ref, o_ref, wq_scr, *, chunk):
    @pl.when(pl.program_id(1) == 0)
    def _():
        for c in range(w_ref.shape[1] // chunk):
            cols = slice(c * chunk, (c + 1) * chunk)
            wq_scr[:, cols] = (w_ref[:, cols].astype(F32) * sc_ref[...]).astype(FP8)

    o_ref[...] = jnp.dot(wq_scr[...], v_ref[...], preferred_element_type=F32) * dq_ref[...]


def _peer_values(w, wmax, v_q, v_inv):
    t, n_exp = w.shape
    d = v_q.shape[1]
    tm, tn = 512, min(256, d)
    wmax_col = jnp.maximum(wmax.reshape(t, 1), FP8_TINY)
    scale = FP8_TARGET / wmax_col
    dequant = wmax_col * (v_inv / FP8_TARGET)
    col = pl.BlockSpec((tm, 1), lambda i, j: (i, 0))
    return pl.pallas_call(
        functools.partial(_peer_v_kernel, chunk=min(2048, n_exp)),
        grid=(t // tm, d // tn),
        in_specs=[pl.BlockSpec((tm, n_exp), lambda i, j: (i, 0), pipeline_mode=pl.Buffered(1)),
                  col, col,
                  pl.BlockSpec((n_exp, tn), lambda i, j: (0, j))],
        out_specs=pl.BlockSpec((tm, tn), lambda i, j: (i, j)),
        out_shape=jax.ShapeDtypeStruct((t, d), F32),
        scratch_shapes=[pltpu.VMEM((tm, n_exp), FP8)],
        compiler_params=_params(("arbitrary", "arbitrary"), 48),
        name="peer_v",
    )(w, scale, dequant, v_q)


def _final_kernel(x_ref, p_ref, g2_ref, fg_ref, o_ref):
    x = x_ref[...] + g2_ref[0] * p_ref[...]
    ms = jnp.mean(x * x, axis=-1, keepdims=True)
    o_ref[...] = x * lax.rsqrt(ms + EPS) * fg_ref[...]


def _final(x1, peer, g2, final_g, seq):
    t, d = x1.shape
    tm = 256
    per_b = seq // tm
    row = pl.BlockSpec((tm, d), lambda i: (i, 0))
    return pl.pallas_call(
        _final_kernel,
        grid=(t // tm,),
        in_specs=[row, row,
                  pl.BlockSpec((1, 1, d), lambda i: (i // per_b, 0, 0)),
                  pl.BlockSpec((1, d), lambda i: (0, 0))],
        out_specs=row,
        out_shape=jax.ShapeDtypeStruct((t, d), F32),
        compiler_params=_params(("arbitrary",), 48),
        name="final",
    )(x1, peer, g2, final_g.reshape(1, d))


def _rope_tables(seq):
    rows = seq // GRID_W
    row_ids = jnp.repeat(jnp.arange(rows), GRID_W).astype(F32)
    col_ids = jnp.tile(jnp.arange(GRID_W), rows).astype(F32)
    axis_dim = HEAD_DIM // 2
    inv_freq = ROPE_THETA ** (-jnp.arange(0, axis_dim, 2, dtype=F32) / axis_dim)
    ang = jnp.concatenate([row_ids[:, None] * inv_freq, col_ids[:, None] * inv_freq], axis=-1)
    cos, sin = jnp.cos(ang), jnp.sin(ang)
    return jnp.concatenate([cos, cos], axis=-1), jnp.concatenate([-sin, sin], axis=-1)


def _trunk(x, mod, wts, dims):
    nb, seq, d = x.shape
    att_w, kv_w, f_w, gd = dims
    x2d = x.reshape(nb * seq, d)
    sh1, sc1, g1, sh2, sc2, g2 = [mod[:, k].reshape(nb, 1, d) for k in range(6)]
    cc, ss = _rope_tables(seq)

    h = _ln_mod(x2d, wts["norm1_g"], sc1, sh1, seq, transpose=False)
    proj = _inproj(h, wts["w_main"], wts["qg"], wts["kg"], cc, ss, seq, att_w, kv_w)
    p, q = _fproj(h, wts["w_f"], wts["cs_ch"], gd)
    att = _attention(proj, nb, seq, att_w, kv_w)
    four = _seq_dft(p, q, nb, seq, gd)
    merged = _merge(att, four, wts["wa"], wts["wf"], proj, att_w + 2 * kv_w)
    x1 = _outproj(merged, wts["w_out"], x2d, g1, seq)

    h2t, h2t_q, h2_inv = _ln_mod(x1, wts["norm2_g"], sc2, sh2, seq, transpose=True)
    g0, phi, e1 = _peer_scores(h2t, wts["wq_t"], wts["keys"])
    w, wmax = _peer_weights(h2t_q, h2_inv * wts["u_inv"], wts["u_q"], g0, phi, e1)
    peer = _peer_values(w, wmax, wts["v_q"], wts["v_inv"])
    y = _final(x1, peer, g2, wts["final_g"], seq)
    return y.reshape(nb, seq, d)


def kernel(x_prompt, x_sample, c_prompt, c_sample, w_ada, b_ada, norm1_g, norm2_g, w_in, q_norm_g, k_norm_g, w_attn_br, w_four_br, w_out, w_peer_q, peer_keys, peer_u, peer_v, final_g):
    d = x_prompt.shape[-1]
    assert w_ada.shape[0] == 1, "single-layer trunk"
    att_w = w_attn_br.shape[1]
    f_w = w_four_br.shape[1]
    kv_w = (w_in.shape[2] - att_w - f_w - 2 * d) // 2
    gd = f_w // N_FGROUPS
    dims = (att_w, kv_w, f_w, gd)

    perm = np.concatenate([np.arange(0, HEAD_DIM, 2), np.arange(1, HEAD_DIM, 2)])
    w = w_in[0]

    def deinterleave(cols):
        return cols.reshape(d, -1, HEAD_DIM)[:, :, perm].reshape(d, -1)

    s_q, s_k, s_v, s_f = att_w, att_w + kv_w, att_w + 2 * kv_w, att_w + 2 * kv_w + f_w
    w_main = jnp.concatenate([deinterleave(w[:, :s_q]), deinterleave(w[:, s_q:s_k]), w[:, s_k:s_v],
                              w[:, s_f:]], axis=1).astype(BF16)
    cch, sch = _dft_tables(gd)
    u_max = jnp.maximum(jnp.max(jnp.abs(peer_u[0])), FP8_TINY)
    v_max = jnp.maximum(jnp.max(jnp.abs(peer_v[0])), FP8_TINY)
    wts = {
        "norm1_g": norm1_g[0], "norm2_g": norm2_g[0], "final_g": final_g,
        "w_main": w_main,
        "w_f": w[:, s_v:s_f].astype(BF16),
        "cs_ch": jnp.asarray(np.concatenate([cch, sch], axis=1), BF16),
        "qg": q_norm_g[0][perm].reshape(1, HEAD_DIM),
        "kg": k_norm_g[0][perm].reshape(1, HEAD_DIM),
        "wa": w_attn_br[0].astype(BF16), "wf": w_four_br[0].astype(BF16),
        "w_out": w_out[0].astype(BF16),
        "wq_t": w_peer_q[0].T.astype(BF16),
        "keys": peer_keys[0].astype(BF16),
        "u_q": (peer_u[0] * (FP8_TARGET / u_max)).astype(FP8), "u_inv": u_max * (1.0 / FP8_TARGET),
        "v_q": (peer_v[0] * (FP8_TARGET / v_max)).astype(FP8), "v_inv": v_max * (1.0 / FP8_TARGET),
    }

    nbp, nbs = c_prompt.shape[0], c_sample.shape[0]
    c_all = jnp.concatenate([c_prompt, c_sample], axis=0)
    c_pad = jnp.pad(c_all, ((0, (-c_all.shape[0]) % 8), (0, 0)))
    mod = _modulation(c_pad, w_ada[0], b_ada[0]).reshape(c_pad.shape[0], 6, d)

    y_prompt = _trunk(x_prompt, mod[:nbp], wts, dims)
    y_sample = _trunk(x_sample, mod[nbp:nbp + nbs], wts, dims)
    return (y_prompt, y_sample)
```

```python
import functools
import math

import jax
import jax.numpy as jnp
import numpy as np
from jax import lax
from jax.experimental import pallas as pl
from jax.experimental.pallas import tpu as pltpu

F32 = jnp.float32
BF16 = jnp.bfloat16
FP8 = jnp.float8_e4m3fn
FP8_TARGET = 256.0
FP8_TINY = 1e-30

HEAD_DIM = 128
GQA_GROUP = 4
Q_PER_TILE = GQA_GROUP * HEAD_DIM
ROPE_THETA = 10000.0
GRID_W = 64
N_FGROUPS = 8
PEER_HEADS = 8
PEER_NKEYS = 128
PEER_HALF = 128
PEER_TOPK = 16
EPS = 1e-6
LANES = 128
NEG_INF = float("-inf")
ONES_ROWS = 16
LOG2E = math.log2(math.e)
MIB = 1024 * 1024


def _params(sem, vmem_mib):
    return pltpu.CompilerParams(dimension_semantics=sem, vmem_limit_bytes=vmem_mib * MIB)


def _mod_kernel(c_ref, w_ref, b_ref, o_ref):
    c = c_ref[...]
    a = c * jax.nn.sigmoid(c)
    o_ref[...] = jnp.dot(a, w_ref[...], preferred_element_type=F32,
                         precision=lax.Precision.HIGHEST) + b_ref[...]


def _modulation(c_pad, w_ada, b_ada):
    rows, d = c_pad.shape
    n = w_ada.shape[1]
    tn = 512
    return pl.pallas_call(
        _mod_kernel,
        grid=(n // tn,),
        in_specs=[pl.BlockSpec((rows, d), lambda j: (0, 0)),
                  pl.BlockSpec((d, tn), lambda j: (0, j)),
                  pl.BlockSpec((1, tn), lambda j: (0, j))],
        out_specs=pl.BlockSpec((rows, tn), lambda j: (0, j)),
        out_shape=jax.ShapeDtypeStruct((rows, n), F32),
        compiler_params=_params(("arbitrary",), 40),
        name="mod",
    )(c_pad, w_ada, b_ada.reshape(1, n))


def _ln_rows(x_ref, g_ref, sc_ref, sh_ref):
    x = x_ref[...]
    ms = jnp.mean(x * x, axis=-1, keepdims=True)
    y = x * lax.rsqrt(ms + EPS) * g_ref[...]
    return y * (1.0 + sc_ref[0]) + sh_ref[0]


def _ln_kernel(x_ref, g_ref, sc_ref, sh_ref, o_ref):
    o_ref[...] = _ln_rows(x_ref, g_ref, sc_ref, sh_ref).astype(BF16)


def _ln_t_kernel(x_ref, g_ref, sc_ref, sh_ref, o_ref, q_ref, s_ref, ht_scr, *, chunk):
    h = _ln_rows(x_ref, g_ref, sc_ref, sh_ref)
    amax = jnp.full(s_ref.shape, FP8_TINY, F32)
    for c in range(h.shape[1] // chunk):
        rows = slice(c * chunk, (c + 1) * chunk)
        blk = h[:, rows].T
        ht_scr[rows, :] = blk
        o_ref[rows, :] = blk.astype(BF16)
        amax = jnp.maximum(amax, jnp.max(jnp.abs(blk), axis=0, keepdims=True))
    q_ref[...] = (ht_scr[...] * (FP8_TARGET / amax)).astype(FP8)
    s_ref[...] = amax * (1.0 / FP8_TARGET)


def _ln_mod(x2d, g, sc, sh, seq, *, transpose):
    t, d = x2d.shape
    tm = 256
    per_b = seq // tm
    in_specs = [pl.BlockSpec((tm, d), lambda i: (i, 0)),
                pl.BlockSpec((1, d), lambda i: (0, 0)),
                pl.BlockSpec((1, 1, d), lambda i: (i // per_b, 0, 0)),
                pl.BlockSpec((1, 1, d), lambda i: (i // per_b, 0, 0))]
    args = (x2d, g.reshape(1, d), sc, sh)
    if not transpose:
        return pl.pallas_call(
            _ln_kernel, grid=(t // tm,), in_specs=in_specs,
            out_specs=pl.BlockSpec((tm, d), lambda i: (i, 0)),
            out_shape=jax.ShapeDtypeStruct((t, d), BF16),
            compiler_params=_params(("arbitrary",), 40), name="ln",
        )(*args)
    col_blk = pl.BlockSpec((d, tm), lambda i: (0, i))
    return pl.pallas_call(
        functools.partial(_ln_t_kernel, chunk=min(512, d)),
        grid=(t // tm,), in_specs=in_specs,
        out_specs=[col_blk, col_blk, pl.BlockSpec((1, tm), lambda i: (0, i))],
        out_shape=[jax.ShapeDtypeStruct((d, t), BF16), jax.ShapeDtypeStruct((d, t), FP8),
                   jax.ShapeDtypeStruct((1, t), F32)],
        scratch_shapes=[pltpu.VMEM((d, tm), F32)],
        compiler_params=_params(("arbitrary",), 40), name="ln_t",
    )(*args)


def _norm_rope(acc, g, cc, ss, scale, o_ref):
    for hd in range(acc.shape[1] // HEAD_DIM):
        xh = acc[:, hd * HEAD_DIM:(hd + 1) * HEAD_DIM]
        ms = jnp.mean(xh * xh, axis=-1, keepdims=True)
        y = xh * lax.rsqrt(ms + EPS) * g
        y = y * cc + pltpu.roll(y, HEAD_DIM // 2, axis=1) * ss
        if scale != 1.0:
            y = y * scale
        o_ref[:, hd * HEAD_DIM:(hd + 1) * HEAD_DIM] = y.astype(BF16)


def _inproj_kernel(h_ref, w_ref, qg_ref, kg_ref, cc_ref, ss_ref, o_ref, *, nq, nk, nv, q_scale):
    j = pl.program_id(1)
    acc = jnp.dot(h_ref[...], w_ref[...], preferred_element_type=F32)

    @pl.when(j < nq)
    def _():
        _norm_rope(acc, qg_ref[...], cc_ref[...], ss_ref[...], q_scale, o_ref)

    @pl.when(jnp.logical_and(j >= nq, j < nq + nk))
    def _():
        _norm_rope(acc, kg_ref[...], cc_ref[...], ss_ref[...], 1.0, o_ref)

    @pl.when(jnp.logical_and(j >= nq + nk, j < nq + nk + nv))
    def _():
        o_ref[...] = acc.astype(BF16)

    @pl.when(j >= nq + nk + nv)
    def _():
        o_ref[...] = jax.nn.sigmoid(acc).astype(BF16)


def _inproj(h, w_main, qg, kg, cc, ss, seq, att_w, kv_w):
    t, d = h.shape
    wtot = w_main.shape[1]
    tm = 512
    tn = min(1024, kv_w)
    per_b = seq // tm
    kern = functools.partial(_inproj_kernel, nq=att_w // tn, nk=kv_w // tn, nv=kv_w // tn,
                             q_scale=1.0)
    return pl.pallas_call(
        kern,
        grid=(t // tm, wtot // tn),
        in_specs=[pl.BlockSpec((tm, d), lambda i, j: (i, 0)),
                  pl.BlockSpec((d, tn), lambda i, j: (0, j)),
                  pl.BlockSpec((1, HEAD_DIM), lambda i, j: (0, 0)),
                  pl.BlockSpec((1, HEAD_DIM), lambda i, j: (0, 0)),
                  pl.BlockSpec((tm, HEAD_DIM), lambda i, j: (i % per_b, 0)),
                  pl.BlockSpec((tm, HEAD_DIM), lambda i, j: (i % per_b, 0))],
        out_specs=pl.BlockSpec((tm, tn), lambda i, j: (i, j)),
        out_shape=jax.ShapeDtypeStruct((t, wtot), BF16),
        compiler_params=_params(("arbitrary", "arbitrary"), 48),
        name="inproj",
    )(h, w_main, qg, kg, cc, ss)


def _fproj_kernel(h_ref, w_ref, cs_ref, p_ref, q_ref, *, gd):
    acc = jnp.dot(h_ref[...], w_ref[...], preferred_element_type=F32)
    for g in range(acc.shape[1] // gd):
        u = acc[:, g * gd:(g + 1) * gd].astype(BF16)
        pq = jnp.dot(u, cs_ref[...], preferred_element_type=F32)
        p_ref[:, g * gd:(g + 1) * gd] = pq[:, :gd].astype(BF16)
        q_ref[:, g * gd:(g + 1) * gd] = pq[:, gd:].astype(BF16)


def _fproj(h, w_f, cs, gd):
    t, d = h.shape
    fw = w_f.shape[1]
    tm = 512
    tn = min(1024, fw)
    out = jax.ShapeDtypeStruct((t, fw), BF16)
    return pl.pallas_call(
        functools.partial(_fproj_kernel, gd=gd),
        grid=(t // tm, fw // tn),
        in_specs=[pl.BlockSpec((tm, d), lambda i, j: (i, 0)),
                  pl.BlockSpec((d, tn), lambda i, j: (0, j)),
                  pl.BlockSpec((gd, 2 * gd), lambda i, j: (0, 0))],
        out_specs=[pl.BlockSpec((tm, tn), lambda i, j: (i, j)),
                   pl.BlockSpec((tm, tn), lambda i, j: (i, j))],
        out_shape=[out, out],
        compiler_params=_params(("arbitrary", "arbitrary"), 48),
        name="fproj",
    )(h, w_f, cs)


def _attn_kernel(q_ref, k_ref, v_ref, o_ref, m_scr, acc_scr, qt_scr, sa_scr, sb_scr, *, tk):
    tq = q_ref.shape[0]
    seq = k_ref.shape[0]
    n_chunks = seq // tk
    mm_dtype = qt_scr.dtype
    for g in range(GQA_GROUP):
        qt_scr[:, g * tq:(g + 1) * tq] = q_ref[:, g * HEAD_DIM:(g + 1) * HEAD_DIM].astype(F32).T.astype(mm_dtype)
    m_scr[...] = jnp.full(m_scr.shape, NEG_INF, F32)
    acc_scr[...] = jnp.zeros(acc_scr.shape, F32)
    ones = jnp.ones((ONES_ROWS, tk), BF16)

    def scores(c, st_ref):
        start = pl.multiple_of(c * tk, tk)
        st_ref[...] = jnp.dot(k_ref[pl.ds(start, tk), :].astype(mm_dtype), qt_scr[...],
                              preferred_element_type=F32)

    def consume(c, st_ref):
        start = pl.multiple_of(c * tk, tk)
        v = v_ref[pl.ds(start, tk), :]
        st = st_ref[...]
        m_prev = m_scr[...]
        m_next = jnp.maximum(m_prev, jnp.max(st, axis=0, keepdims=True))
        p = jnp.exp2(st - m_next).astype(BF16)
        alpha = jnp.exp2(m_prev - m_next)
        vt = jnp.concatenate([v.astype(F32).T.astype(BF16), ones], axis=0)
        acc_scr[...] = alpha * acc_scr[...] + jnp.dot(vt, p, preferred_element_type=F32)
        m_scr[...] = m_next

    scores(0, sa_scr)

    def body(i, carry):
        c = 2 * i
        scores(c + 1, sb_scr)
        consume(c, sa_scr)
        scores(c + 2, sa_scr)
        consume(c + 1, sb_scr)
        return carry

    lax.fori_loop(0, n_chunks // 2 - 1, body, 0)
    scores(n_chunks - 1, sb_scr)
    consume(n_chunks - 2, sa_scr)
    consume(n_chunks - 1, sb_scr)
    acc = acc_scr[...]
    out_t = acc[:HEAD_DIM] / acc[HEAD_DIM:HEAD_DIM + 1]
    for g in range(GQA_GROUP):
        o_ref[:, g * HEAD_DIM:(g + 1) * HEAD_DIM] = out_t[:, g * tq:(g + 1) * tq].T.astype(BF16)


def _attention(proj, nb, seq, att_w, kv_w, mm_dtype):
    t = proj.shape[0]
    tq = 512
    tk = min(1024, seq // 2)
    assert seq % (2 * tk) == 0
    n_kv = kv_w // HEAD_DIM
    per_b = seq // tq
    k_off = att_w // HEAD_DIM
    v_off = (att_w + kv_w) // HEAD_DIM
    cols = GQA_GROUP * tq
    return pl.pallas_call(
        functools.partial(_attn_kernel, tk=tk),
        grid=(nb, n_kv, per_b),
        in_specs=[pl.BlockSpec((tq, Q_PER_TILE), lambda b, g, i: (b * per_b + i, g)),
                  pl.BlockSpec((seq, HEAD_DIM), lambda b, g, i: (b, k_off + g)),
                  pl.BlockSpec((seq, HEAD_DIM), lambda b, g, i: (b, v_off + g))],
        out_specs=pl.BlockSpec((tq, Q_PER_TILE), lambda b, g, i: (b * per_b + i, g)),
        out_shape=jax.ShapeDtypeStruct((t, att_w), BF16),
        scratch_shapes=[pltpu.VMEM((1, cols), F32),
                        pltpu.VMEM((HEAD_DIM + ONES_ROWS, cols), F32),
                        pltpu.VMEM((HEAD_DIM, cols), mm_dtype),
                        pltpu.VMEM((tk, cols), F32),
                        pltpu.VMEM((tk, cols), F32)],
        compiler_params=_params(("arbitrary", "arbitrary", "arbitrary"), 48),
        name="attn",
    )(proj, proj, proj)


def _fft_a_kernel(m_ref, p_ref, q_ref, twc_ref, tws_ref, ar_ref, ai_ref, *, fw):
    n1 = p_ref.shape[1]
    rhs = jnp.concatenate([p_ref[0], q_ref[0]], axis=0)
    a = jnp.dot(m_ref[...], rhs, preferred_element_type=F32)
    rep = fw // LANES
    for u in range(p_ref.shape[2] // fw):
        ar = a[:n1, u * fw:(u + 1) * fw]
        ai = a[n1:, u * fw:(u + 1) * fw]
        c = jnp.tile(twc_ref[u], (1, rep))
        s = jnp.tile(tws_ref[u], (1, rep))
        ar_ref[0, :, u * fw:(u + 1) * fw] = (ar * c + ai * s).astype(BF16)
        ai_ref[0, :, u * fw:(u + 1) * fw] = (ai * c - ar * s).astype(BF16)


def _fft_c_kernel(cs_ref, ar_ref, ai_ref, o_ref, *, scale):
    rhs = jnp.concatenate([ar_ref[0, 0], ai_ref[0, 0]], axis=0)
    o_ref[0] = (jnp.dot(cs_ref[...], rhs, preferred_element_type=F32) * scale).astype(BF16)


def _dft_tables(n):
    idx = (np.arange(n)[:, None] * np.arange(n)[None, :]) % n
    ang = 2.0 * np.pi * idx / n
    return np.cos(ang), np.sin(ang)


def _seq_dft(p, q, nb, seq, gd):
    fw = p.shape[1]
    n1 = 64 if seq >= 4096 else 16
    n2 = seq // n1
    c1, s1 = _dft_tables(n1)
    c2, s2 = _dft_tables(n2)
    m_a = jnp.asarray(np.block([[c1, -s1], [-s1, -c1]]), BF16)
    cs_c = jnp.asarray(np.concatenate([c2, s2], axis=1), BF16)
    tw = 2.0 * np.pi * (np.arange(n2)[:, None] * np.arange(n1)[None, :]) / seq
    twc = jnp.asarray(np.broadcast_to(np.cos(tw)[:, :, None], (n2, n1, LANES)), F32)
    tws = jnp.asarray(np.broadcast_to(np.sin(tw)[:, :, None], (n2, n1, LANES)), F32)

    nt2 = 4
    tc = nt2 * fw
    p3 = p.reshape(nb, n1, n2 * fw)
    q3 = q.reshape(nb, n1, n2 * fw)
    a_shape = jax.ShapeDtypeStruct((nb, n1, n2 * fw), BF16)
    blk = pl.BlockSpec((1, n1, tc), lambda b, j: (b, 0, j))
    tw_blk = pl.BlockSpec((nt2, n1, LANES), lambda b, j: (j, 0, 0))
    ar, ai = pl.pallas_call(
        functools.partial(_fft_a_kernel, fw=fw),
        grid=(nb, n2 // nt2),
        in_specs=[pl.BlockSpec((2 * n1, 2 * n1), lambda b, j: (0, 0)), blk, blk, tw_blk, tw_blk],
        out_specs=[blk, blk],
        out_shape=[a_shape, a_shape],
        compiler_params=_params(("arbitrary", "arbitrary"), 40),
        name="fft_a",
    )(m_a, p3, q3, twc, tws)

    in_blk = pl.BlockSpec((1, 1, n2, fw), lambda b, k: (b, k, 0, 0))
    out = pl.pallas_call(
        functools.partial(_fft_c_kernel, scale=1.0 / math.sqrt(seq * gd)),
        grid=(nb, n1),
        in_specs=[pl.BlockSpec((n2, 2 * n2), lambda b, k: (0, 0)), in_blk, in_blk],
        out_specs=pl.BlockSpec((1, n2, fw), lambda b, k: (b, 0, k)),
        out_shape=jax.ShapeDtypeStruct((nb, n2, n1 * fw), BF16),
        compiler_params=_params(("arbitrary", "arbitrary"), 40),
        name="fft_c",
    )(cs_c, ar.reshape(nb, n1, n2, fw), ai.reshape(nb, n1, n2, fw))
    return out.reshape(nb * seq, fw)


def _merge_kernel(a_ref, f_ref, wa_ref, wf_ref, ga_ref, gf_ref, o_ref):
    a_br = jnp.dot(a_ref[...], wa_ref[...], preferred_element_type=F32)
    f_br = jnp.dot(f_ref[...], wf_ref[...], preferred_element_type=F32)
    o_ref[...] = (ga_ref[...].astype(F32) * a_br + gf_ref[...].astype(F32) * f_br).astype(BF16)


def _merge(att, four, wa, wf, proj, gate_off):
    t, aw = att.shape
    fw = four.shape[1]
    d = wa.shape[1]
    tm, tn = 512, 512
    ga0 = gate_off // tn
    gf0 = (gate_off + d) // tn
    return pl.pallas_call(
        _merge_kernel,
        grid=(t // tm, d // tn),
        in_specs=[pl.BlockSpec((tm, aw), lambda i, j: (i, 0)),
                  pl.BlockSpec((tm, fw), lambda i, j: (i, 0)),
                  pl.BlockSpec((aw, tn), lambda i, j: (0, j)),
                  pl.BlockSpec((fw, tn), lambda i, j: (0, j)),
                  pl.BlockSpec((tm, tn), lambda i, j: (i, ga0 + j)),
                  pl.BlockSpec((tm, tn), lambda i, j: (i, gf0 + j))],
        out_specs=pl.BlockSpec((tm, tn), lambda i, j: (i, j)),
        out_shape=jax.ShapeDtypeStruct((t, d), BF16),
        compiler_params=_params(("arbitrary", "arbitrary"), 48),
        name="merge",
    )(att, four, wa, wf, proj, proj)


def _outproj_kernel(m_ref, w_ref, x_ref, g_ref, o_ref):
    acc = jnp.dot(m_ref[...], w_ref[...], preferred_element_type=F32)
    o_ref[...] = x_ref[...] + g_ref[0] * acc


def _outproj(merged, w_out, x2d, g1, seq):
    t, d = merged.shape
    tm, tn = 512, 1024
    per_b = seq // tm
    return pl.pallas_call(
        _outproj_kernel,
        grid=(t // tm, d // tn),
        in_specs=[pl.BlockSpec((tm, d), lambda i, j: (i, 0)),
                  pl.BlockSpec((d, tn), lambda i, j: (0, j)),
                  pl.BlockSpec((tm, tn), lambda i, j: (i, j)),
                  pl.BlockSpec((1, 1, tn), lambda i, j: (i // per_b, 0, j))],
        out_specs=pl.BlockSpec((tm, tn), lambda i, j: (i, j)),
        out_shape=jax.ShapeDtypeStruct((t, d), F32),
        compiler_params=_params(("arbitrary", "arbitrary"), 48),
        name="outproj",
    )(merged, w_out, x2d, g1)


def _top_values(s, k, store):
    n = s.shape[0]
    iota = lax.broadcasted_iota(jnp.int32, s.shape, 0)
    cur = s
    for r in range(k):
        m = jnp.max(cur, axis=0, keepdims=True)
        store(r, m)
        first = jnp.min(jnp.where(cur == m, iota, n), axis=0, keepdims=True)
        cur = jnp.where(iota == first, NEG_INF, cur)


def _top_values_distinct(s, k, store):
    cur = s
    for r in range(k):
        m = jnp.max(cur, axis=0, keepdims=True)
        store(r, m)
        cur = jnp.where(cur == m, NEG_INF, cur)
    return jnp.sum(jnp.where(cur == NEG_INF, 1.0, 0.0), axis=0, keepdims=True)


def _peer_score_kernel(h_ref, wq_ref, keys_ref, g0_ref, phi_ref, e1_ref, top_scr, best_scr):
    k1 = PEER_TOPK + 1
    qt = jnp.dot(wq_ref[...], h_ref[...], preferred_element_type=F32).astype(BF16)
    top_scr[...] = jnp.full(top_scr.shape, NEG_INF, F32)
    pad_rows = top_scr.shape[1] - k1
    s = [jnp.dot(keys_ref[0, p], qt[p * PEER_HALF:(p + 1) * PEER_HALF], preferred_element_type=F32)
         for p in range(2)]

    def store_best(r, row):
        best_scr[r:r + 1, :] = row

    def extract(top_fn):
        results = []
        for p in range(2):
            def store(r, row, p=p):
                top_scr[p, r:r + 1, :] = row
            results.append(top_fn(s[p], k1, store))
        cand = jnp.concatenate([top_scr[0, 0:1, :] + top_scr[1]]
                               + [top_scr[0, i:i + 1, :] + top_scr[1, 0:8, :] for i in range(1, k1)], axis=0)
        results.append(top_fn(cand, k1, store_best))
        return results

    n0, n1, nc = extract(_top_values_distinct)
    suspect = jnp.where((n0 != k1) | (n1 != k1) | (nc != k1 + pad_rows), 1.0, 0.0)

    @pl.when(jnp.max(suspect) > 0.0)
    def _():
        extract(_top_values)

    best = best_scr[0:PEER_TOPK]
    z = jnp.sum(jnp.exp(best - best[0:1]), axis=0, keepdims=True)
    tau = 0.5 * (best_scr[PEER_TOPK - 1:PEER_TOPK] + best_scr[PEER_TOPK:k1])
    max0 = top_scr[0, 0:1, :]
    max1 = top_scr[1, 0:1, :]
    g0_ref[0] = jnp.exp(s[0] - max0) / z
    phi_ref[0] = jnp.exp((tau - max1) - s[0])
    e1_ref[0] = jnp.exp(s[1] - max1).astype(BF16)


def _peer_scores(h2t, wq_t, keys):
    d, t = h2t.shape
    tm = 512
    top_rows = 24
    s_shape = jax.ShapeDtypeStruct((PEER_HEADS, PEER_NKEYS, t), F32)
    s_blk = pl.BlockSpec((1, PEER_NKEYS, tm), lambda i, h: (h, 0, i))
    return pl.pallas_call(
        _peer_score_kernel,
        grid=(t // tm, PEER_HEADS),
        in_specs=[pl.BlockSpec((d, tm), lambda i, h: (0, i)),
                  pl.BlockSpec((2 * PEER_HALF, d), lambda i, h: (h, 0)),
                  pl.BlockSpec((1, 2, PEER_NKEYS, PEER_HALF), lambda i, h: (h, 0, 0, 0))],
        out_specs=[s_blk, s_blk, s_blk],
        out_shape=[s_shape, s_shape, jax.ShapeDtypeStruct(s_shape.shape, BF16)],
        scratch_shapes=[pltpu.VMEM((2, top_rows, tm), F32), pltpu.VMEM((top_rows, tm), F32)],
        compiler_params=_params(("arbitrary", "arbitrary"), 40),
        name="peer_scores",
    )(h2t, wq_t, keys)


def _peer_w_kernel(h_ref, hs_ref, u_ref, g0_ref, phi_ref, e1_ref, w_ref, wmax_ref, st_scr, *, n_tiles):
    j = pl.program_id(1)
    tm, te = w_ref.shape
    rows = te // PEER_NKEYS

    @pl.when(j == 0)
    def _():
        st_scr[1] = jnp.zeros(st_scr.shape[1:], F32)
        wmax_ref[...] = jnp.zeros(wmax_ref.shape, F32)

    gate_tile = jnp.clip(j - 1, 0, n_tiles - 1)

    half = tm // 2

    def step(par):
        def body(a, carry):
            ts = pl.multiple_of(a * half, half)
            st_scr[par, :, pl.ds(ts, half)] = jnp.dot(u_ref[...], h_ref[:, pl.ds(ts, half)],
                                                      preferred_element_type=F32)
            inv_scale = hs_ref[:, pl.ds(ts, half)]
            for r in range(rows):
                n0 = gate_tile * rows + r
                ks = slice(r * PEER_NKEYS, (r + 1) * PEER_NKEYS)
                st = st_scr[1 - par, ks, pl.ds(ts, half)] * inv_scale
                act = 0.5 * st * (1.0 + lax.erf(st * math.sqrt(0.5)))
                gate = None
                for hd in range(PEER_HEADS):
                    e1 = e1_ref[hd, :, pl.ds(ts, half)]
                    phi = phi_ref[hd, pl.ds(n0, 1), pl.ds(ts, half)].astype(BF16)
                    g0 = g0_ref[hd, pl.ds(n0, 1), pl.ds(ts, half)].astype(BF16)
                    g = jnp.where(e1 >= phi, e1 * g0, jnp.zeros_like(e1))
                    gate = g if gate is None else gate + g
                wt = act * gate.astype(F32)
                w_ref[pl.ds(ts, half), ks] = wt.T.astype(BF16)
                wmax_ref[:, pl.ds(ts, half)] = jnp.maximum(wmax_ref[:, pl.ds(ts, half)],
                                                           jnp.max(jnp.abs(wt), axis=0, keepdims=True))
            return carry

        lax.fori_loop(0, 2, body, 0)

    @pl.when(j % 2 == 0)
    def _():
        step(0)

    @pl.when(j % 2 == 1)
    def _():
        step(1)


def _peer_weights(h2t, hs, u_q, g0, phi, e1):
    d, t = h2t.shape
    n_exp = u_q.shape[0]
    tm, te = 512, 512
    n_tiles = n_exp // te
    once = pl.Buffered(1)
    sc_blk = pl.BlockSpec((PEER_HEADS, PEER_NKEYS, tm), lambda i, j: (0, 0, i), pipeline_mode=once)
    return pl.pallas_call(
        functools.partial(_peer_w_kernel, n_tiles=n_tiles),
        grid=(t // tm, n_tiles + 1),
        in_specs=[pl.BlockSpec((d, tm), lambda i, j: (0, i), pipeline_mode=once),
                  pl.BlockSpec((1, tm), lambda i, j: (0, i)),
                  pl.BlockSpec((te, d), lambda i, j: (jnp.minimum(j, n_tiles - 1), 0)),
                  sc_blk, sc_blk, sc_blk],
        out_specs=[pl.BlockSpec((tm, te), lambda i, j: (i, jnp.maximum(j - 1, 0))),
                   pl.BlockSpec((1, tm), lambda i, j: (0, i))],
        out_shape=[jax.ShapeDtypeStruct((t, n_exp), BF16), jax.ShapeDtypeStruct((1, t), F32)],
        scratch_shapes=[pltpu.VMEM((2, te, tm), F32)],
        compiler_params=_params(("arbitrary", "arbitrary"), 48),
        name="peer_w",
    )(h2t, hs, u_q, g0, phi, e1)


def _peer_v_kernel(w_ref, sc_ref, dq_ref, v_ref, o_ref, wq_scr, *, chunk):
    @pl.when(pl.program_id(1) == 0)
    def _():
        for c in range(w_ref.shape[1] // chunk):
            cols = slice(c * chunk, (c + 1) * chunk)
            wq_scr[:, cols] = (w_ref[:, cols].astype(F32) * sc_ref[...]).astype(FP8)

    o_ref[...] = jnp.dot(wq_scr[...], v_ref[...], preferred_element_type=F32) * dq_ref[...]


def _peer_values(w, wmax, v_q, v_inv):
    t, n_exp = w.shape
    d = v_q.shape[1]
    tm, tn = 512, min(256, d)
    wmax_col = jnp.maximum(wmax.reshape(t, 1), FP8_TINY)
    scale = FP8_TARGET / wmax_col
    dequant = wmax_col * (v_inv / FP8_TARGET)
    col = pl.BlockSpec((tm, 1), lambda i, j: (i, 0))
    return pl.pallas_call(
        functools.partial(_peer_v_kernel, chunk=min(2048, n_exp)),
        grid=(t // tm, d // tn),
        in_specs=[pl.BlockSpec((tm, n_exp), lambda i, j: (i, 0), pipeline_mode=pl.Buffered(1)),
                  col, col,
                  pl.BlockSpec((n_exp, tn), lambda i, j: (0, j))],
        out_specs=pl.BlockSpec((tm, tn), lambda i, j: (i, j)),
        out_shape=jax.ShapeDtypeStruct((t, d), F32),
        scratch_shapes=[pltpu.VMEM((tm, n_exp), FP8)],
        compiler_params=_params(("arbitrary", "arbitrary"), 48),
        name="peer_v",
    )(w, scale, dequant, v_q)


def _final_kernel(x_ref, p_ref, g2_ref, fg_ref, o_ref):
    x = x_ref[...] + g2_ref[0] * p_ref[...]
    ms = jnp.mean(x * x, axis=-1, keepdims=True)
    o_ref[...] = x * lax.rsqrt(ms + EPS) * fg_ref[...]


def _final(x1, peer, g2, final_g, seq):
    t, d = x1.shape
    tm = 256
    per_b = seq // tm
    row = pl.BlockSpec((tm, d), lambda i: (i, 0))
    return pl.pallas_call(
        _final_kernel,
        grid=(t // tm,),
        in_specs=[row, row,
                  pl.BlockSpec((1, 1, d), lambda i: (i // per_b, 0, 0)),
                  pl.BlockSpec((1, d), lambda i: (0, 0))],
        out_specs=row,
        out_shape=jax.ShapeDtypeStruct((t, d), F32),
        compiler_params=_params(("arbitrary",), 48),
        name="final",
    )(x1, peer, g2, final_g.reshape(1, d))


def _rope_tables(seq):
    rows = seq // GRID_W
    row_ids = jnp.repeat(jnp.arange(rows), GRID_W).astype(F32)
    col_ids = jnp.tile(jnp.arange(GRID_W), rows).astype(F32)
    axis_dim = HEAD_DIM // 2
    inv_freq = ROPE_THETA ** (-jnp.arange(0, axis_dim, 2, dtype=F32) / axis_dim)
    ang = jnp.concatenate([row_ids[:, None] * inv_freq, col_ids[:, None] * inv_freq], axis=-1)
    cos, sin = jnp.cos(ang), jnp.sin(ang)
    return jnp.concatenate([cos, cos], axis=-1), jnp.concatenate([-sin, sin], axis=-1)


def _trunk(x, mod, wts, dims):
    nb, seq, d = x.shape
    att_w, kv_w, f_w, gd = dims
    x2d = x.reshape(nb * seq, d)
    sh1, sc1, g1, sh2, sc2, g2 = [mod[:, k].reshape(nb, 1, d) for k in range(6)]
    cc, ss = _rope_tables(seq)

    h = _ln_mod(x2d, wts["norm1_g"], sc1, sh1, seq, transpose=False)
    proj = _inproj(h, wts["w_main"], wts["qg"], wts["kg"], cc, ss, seq, att_w, kv_w)
    p, q = _fproj(h, wts["w_f"], wts["cs_ch"], gd)
    att = lax.cond(wts["qk_fp8_ok"],
                   lambda pr: _attention(pr, nb, seq, att_w, kv_w, FP8),
                   lambda pr: _attention(pr, nb, seq, att_w, kv_w, BF16), proj)
    four = _seq_dft(p, q, nb, seq, gd)
    merged = _merge(att, four, wts["wa"], wts["wf"], proj, att_w + 2 * kv_w)
    x1 = _outproj(merged, wts["w_out"], x2d, g1, seq)

    h2t, h2t_q, h2_inv = _ln_mod(x1, wts["norm2_g"], sc2, sh2, seq, transpose=True)
    g0, phi, e1 = _peer_scores(h2t, wts["wq_t"], wts["keys"])
    w, wmax = _peer_weights(h2t_q, h2_inv * wts["u_inv"], wts["u_q"], g0, phi, e1)
    peer = _peer_values(w, wmax, wts["v_q"], wts["v_inv"])
    y = _final(x1, peer, g2, wts["final_g"], seq)
    return y.reshape(nb, seq, d)


def kernel(x_prompt, x_sample, c_prompt, c_sample, w_ada, b_ada, norm1_g, norm2_g, w_in, q_norm_g, k_norm_g, w_attn_br, w_four_br, w_out, w_peer_q, peer_keys, peer_u, peer_v, final_g):
    d = x_prompt.shape[-1]
    assert w_ada.shape[0] == 1, "single-layer trunk"
    att_w = w_attn_br.shape[1]
    f_w = w_four_br.shape[1]
    kv_w = (w_in.shape[2] - att_w - f_w - 2 * d) // 2
    gd = f_w // N_FGROUPS
    dims = (att_w, kv_w, f_w, gd)

    perm = np.concatenate([np.arange(0, HEAD_DIM, 2), np.arange(1, HEAD_DIM, 2)])
    w = w_in[0]

    def deinterleave(cols):
        return cols.reshape(d, -1, HEAD_DIM)[:, :, perm].reshape(d, -1)

    s_q, s_k, s_v, s_f = att_w, att_w + kv_w, att_w + 2 * kv_w, att_w + 2 * kv_w + f_w
    w_main = jnp.concatenate([deinterleave(w[:, :s_q]), deinterleave(w[:, s_q:s_k]), w[:, s_k:s_v],
                              w[:, s_f:]], axis=1).astype(BF16)
    cch, sch = _dft_tables(gd)
    q_bound = jnp.maximum(jnp.max(jnp.abs(q_norm_g[0])) * LOG2E, FP8_TINY)
    k_bound = jnp.maximum(jnp.max(jnp.abs(k_norm_g[0])) * math.sqrt(HEAD_DIM), FP8_TINY)
    qk_fp8_ok = jnp.sqrt(2.0 * q_bound * k_bound) < FP8_TARGET
    qk_bal = jnp.where(qk_fp8_ok, jnp.exp2(jnp.round(0.5 * jnp.log2(k_bound / q_bound))), 1.0)
    u_max = jnp.maximum(jnp.max(jnp.abs(peer_u[0])), FP8_TINY)
    v_max = jnp.maximum(jnp.max(jnp.abs(peer_v[0])), FP8_TINY)
    wts = {
        "norm1_g": norm1_g[0], "norm2_g": norm2_g[0], "final_g": final_g,
        "w_main": w_main,
        "w_f": w[:, s_v:s_f].astype(BF16),
        "cs_ch": jnp.asarray(np.concatenate([cch, sch], axis=1), BF16),
        "qg": (q_norm_g[0][perm] * (HEAD_DIM ** -0.5 * LOG2E * qk_bal)).reshape(1, HEAD_DIM),
        "kg": (k_norm_g[0][perm] / qk_bal).reshape(1, HEAD_DIM),
        "qk_fp8_ok": qk_fp8_ok,
        "wa": w_attn_br[0].astype(BF16), "wf": w_four_br[0].astype(BF16),
        "w_out": w_out[0].astype(BF16),
        "wq_t": w_peer_q[0].T.astype(BF16),
        "keys": peer_keys[0].astype(BF16),
        "u_q": (peer_u[0] * (FP8_TARGET / u_max)).astype(FP8), "u_inv": u_max * (1.0 / FP8_TARGET),
        "v_q": (peer_v[0] * (FP8_TARGET / v_max)).astype(FP8), "v_inv": v_max * (1.0 / FP8_TARGET),
    }

    nbp, nbs = c_prompt.shape[0], c_sample.shape[0]
    c_all = jnp.concatenate([c_prompt, c_sample], axis=0)
    c_pad = jnp.pad(c_all, ((0, (-c_all.shape[0]) % 8), (0, 0)))
    mod = _modulation(c_pad, w_ada[0], b_ada[0]).reshape(c_pad.shape[0], 6, d)

    y_prompt = _trunk(x_prompt, mod[:nbp], wts, dims)
    y_sample = _trunk(x_sample, mod[nbp:nbp + nbs], wts, dims)
    return (y_prompt, y_sample)
```

```python
import functools
import math

import jax
import jax.numpy as jnp
import numpy as np
from jax import lax
from jax.experimental import pallas as pl
from jax.experimental.pallas import tpu as pltpu

F32 = jnp.float32
BF16 = jnp.bfloat16
FP8 = jnp.float8_e4m3fn
FP8_TARGET = 256.0
FP8_TINY = 1e-30

HEAD_DIM = 128
GQA_GROUP = 4
Q_PER_TILE = GQA_GROUP * HEAD_DIM
ROPE_THETA = 10000.0
GRID_W = 64
N_FGROUPS = 8
PEER_HEADS = 8
PEER_NKEYS = 128
PEER_HALF = 128
PEER_TOPK = 16
SCORE_COLS = 128
EPS = 1e-6
LANES = 128
NEG_INF = float("-inf")
ONES_ROWS = 16
LOG2E = math.log2(math.e)
MIB = 1024 * 1024


def _params(sem, vmem_mib):
    return pltpu.CompilerParams(dimension_semantics=sem, vmem_limit_bytes=vmem_mib * MIB)


def _mod_kernel(c_ref, w_ref, b_ref, o_ref):
    c = c_ref[...]
    a = c * jax.nn.sigmoid(c)
    o_ref[...] = jnp.dot(a, w_ref[...], preferred_element_type=F32,
                         precision=lax.Precision.HIGHEST) + b_ref[...]


def _modulation(c_pad, w_ada, b_ada):
    rows, d = c_pad.shape
    n = w_ada.shape[1]
    tn = 512
    return pl.pallas_call(
        _mod_kernel,
        grid=(n // tn,),
        in_specs=[pl.BlockSpec((rows, d), lambda j: (0, 0)),
                  pl.BlockSpec((d, tn), lambda j: (0, j)),
                  pl.BlockSpec((1, tn), lambda j: (0, j))],
        out_specs=pl.BlockSpec((rows, tn), lambda j: (0, j)),
        out_shape=jax.ShapeDtypeStruct((rows, n), F32),
        compiler_params=_params(("arbitrary",), 40),
        name="mod",
    )(c_pad, w_ada, b_ada.reshape(1, n))


def _ln_rows(x_ref, g_ref, sc_ref, sh_ref):
    x = x_ref[...]
    ms = jnp.mean(x * x, axis=-1, keepdims=True)
    y = x * lax.rsqrt(ms + EPS) * g_ref[...]
    return y * (1.0 + sc_ref[0]) + sh_ref[0]


def _ln_kernel(x_ref, g_ref, sc_ref, sh_ref, o_ref):
    o_ref[...] = _ln_rows(x_ref, g_ref, sc_ref, sh_ref).astype(BF16)


def _ln_t_kernel(x_ref, g_ref, sc_ref, sh_ref, o_ref, q_ref, s_ref, ht_scr, *, chunk):
    h = _ln_rows(x_ref, g_ref, sc_ref, sh_ref)
    amax = jnp.full(s_ref.shape, FP8_TINY, F32)
    for c in range(h.shape[1] // chunk):
        rows = slice(c * chunk, (c + 1) * chunk)
        blk = h[:, rows].T
        ht_scr[rows, :] = blk
        o_ref[rows, :] = blk.astype(BF16)
        amax = jnp.maximum(amax, jnp.max(jnp.abs(blk), axis=0, keepdims=True))
    q_ref[...] = (ht_scr[...] * (FP8_TARGET / amax)).astype(FP8)
    s_ref[...] = amax * (1.0 / FP8_TARGET)


def _ln_mod(x2d, g, sc, sh, seq, *, transpose):
    t, d = x2d.shape
    tm = 256
    per_b = seq // tm
    in_specs = [pl.BlockSpec((tm, d), lambda i: (i, 0)),
                pl.BlockSpec((1, d), lambda i: (0, 0)),
                pl.BlockSpec((1, 1, d), lambda i: (i // per_b, 0, 0)),
                pl.BlockSpec((1, 1, d), lambda i: (i // per_b, 0, 0))]
    args = (x2d, g.reshape(1, d), sc, sh)
    if not transpose:
        return pl.pallas_call(
            _ln_kernel, grid=(t // tm,), in_specs=in_specs,
            out_specs=pl.BlockSpec((tm, d), lambda i: (i, 0)),
            out_shape=jax.ShapeDtypeStruct((t, d), BF16),
            compiler_params=_params(("arbitrary",), 40), name="ln",
        )(*args)
    col_blk = pl.BlockSpec((d, tm), lambda i: (0, i))
    return pl.pallas_call(
        functools.partial(_ln_t_kernel, chunk=min(512, d)),
        grid=(t // tm,), in_specs=in_specs,
        out_specs=[col_blk, col_blk, pl.BlockSpec((1, tm), lambda i: (0, i))],
        out_shape=[jax.ShapeDtypeStruct((d, t), BF16), jax.ShapeDtypeStruct((d, t), FP8),
                   jax.ShapeDtypeStruct((1, t), F32)],
        scratch_shapes=[pltpu.VMEM((d, tm), F32)],
        compiler_params=_params(("arbitrary",), 40), name="ln_t",
    )(*args)


def _norm_rope(acc, g, cc, ss, scale, o_ref):
    for hd in range(acc.shape[1] // HEAD_DIM):
        xh = acc[:, hd * HEAD_DIM:(hd + 1) * HEAD_DIM]
        ms = jnp.mean(xh * xh, axis=-1, keepdims=True)
        y = xh * lax.rsqrt(ms + EPS) * g
        y = y * cc + pltpu.roll(y, HEAD_DIM // 2, axis=1) * ss
        if scale != 1.0:
            y = y * scale
        o_ref[:, hd * HEAD_DIM:(hd + 1) * HEAD_DIM] = y.astype(BF16)


def _inproj_kernel(h_ref, w_ref, qg_ref, kg_ref, cc_ref, ss_ref, o_ref, *, nq, nk, nv, q_scale):
    j = pl.program_id(1)
    acc = jnp.dot(h_ref[...], w_ref[...], preferred_element_type=F32)

    @pl.when(j < nq)
    def _():
        _norm_rope(acc, qg_ref[...], cc_ref[...], ss_ref[...], q_scale, o_ref)

    @pl.when(jnp.logical_and(j >= nq, j < nq + nk))
    def _():
        _norm_rope(acc, kg_ref[...], cc_ref[...], ss_ref[...], 1.0, o_ref)

    @pl.when(jnp.logical_and(j >= nq + nk, j < nq + nk + nv))
    def _():
        o_ref[...] = acc.astype(BF16)

    @pl.when(j >= nq + nk + nv)
    def _():
        o_ref[...] = (0.5 * jnp.tanh(0.5 * acc) + 0.5).astype(BF16)


def _inproj(h, w_main, qg, kg, cc, ss, seq, att_w, kv_w):
    t, d = h.shape
    wtot = w_main.shape[1]
    tm = 512
    tn = min(1024, kv_w)
    per_b = seq // tm
    kern = functools.partial(_inproj_kernel, nq=att_w // tn, nk=kv_w // tn, nv=kv_w // tn,
                             q_scale=1.0)
    return pl.pallas_call(
        kern,
        grid=(t // tm, wtot // tn),
        in_specs=[pl.BlockSpec((tm, d), lambda i, j: (i, 0)),
                  pl.BlockSpec((d, tn), lambda i, j: (0, j)),
                  pl.BlockSpec((1, HEAD_DIM), lambda i, j: (0, 0)),
                  pl.BlockSpec((1, HEAD_DIM), lambda i, j: (0, 0)),
                  pl.BlockSpec((tm, HEAD_DIM), lambda i, j: (i % per_b, 0)),
                  pl.BlockSpec((tm, HEAD_DIM), lambda i, j: (i % per_b, 0))],
        out_specs=pl.BlockSpec((tm, tn), lambda i, j: (i, j)),
        out_shape=jax.ShapeDtypeStruct((t, wtot), BF16),
        compiler_params=_params(("arbitrary", "arbitrary"), 48),
        name="inproj",
    )(h, w_main, qg, kg, cc, ss)


def _fproj_kernel(h_ref, w_ref, cs_ref, p_ref, q_ref, *, gd):
    acc = jnp.dot(h_ref[...], w_ref[...], preferred_element_type=F32)
    for g in range(acc.shape[1] // gd):
        u = acc[:, g * gd:(g + 1) * gd].astype(BF16)
        pq = jnp.dot(u, cs_ref[...], preferred_element_type=F32)
        p_ref[:, g * gd:(g + 1) * gd] = pq[:, :gd].astype(BF16)
        q_ref[:, g * gd:(g + 1) * gd] = pq[:, gd:].astype(BF16)


def _fproj(h, w_f, cs, gd):
    t, d = h.shape
    fw = w_f.shape[1]
    tm = 512
    tn = min(1024, fw)
    out = jax.ShapeDtypeStruct((t, fw), BF16)
    return pl.pallas_call(
        functools.partial(_fproj_kernel, gd=gd),
        grid=(t // tm, fw // tn),
        in_specs=[pl.BlockSpec((tm, d), lambda i, j: (i, 0)),
                  pl.BlockSpec((d, tn), lambda i, j: (0, j)),
                  pl.BlockSpec((gd, 2 * gd), lambda i, j: (0, 0))],
        out_specs=[pl.BlockSpec((tm, tn), lambda i, j: (i, j)),
                   pl.BlockSpec((tm, tn), lambda i, j: (i, j))],
        out_shape=[out, out],
        compiler_params=_params(("arbitrary", "arbitrary"), 48),
        name="fproj",
    )(h, w_f, cs)


def _attn_kernel(q_ref, k_ref, v_ref, o_ref, m_scr, acc_scr, qt_scr, sa_scr, sb_scr, *, tk):
    tq = q_ref.shape[0]
    seq = k_ref.shape[0]
    n_chunks = seq // tk
    mm_dtype = qt_scr.dtype
    for g in range(GQA_GROUP):
        qt_scr[:, g * tq:(g + 1) * tq] = q_ref[:, g * HEAD_DIM:(g + 1) * HEAD_DIM].astype(F32).T.astype(mm_dtype)
    m_scr[...] = jnp.full(m_scr.shape, NEG_INF, F32)
    acc_scr[...] = jnp.zeros(acc_scr.shape, F32)
    ones = jnp.ones((ONES_ROWS, tk), BF16)

    def scores(c, st_ref):
        start = pl.multiple_of(c * tk, tk)
        st_ref[...] = jnp.dot(k_ref[pl.ds(start, tk), :].astype(mm_dtype), qt_scr[...],
                              preferred_element_type=F32)

    def consume(c, st_ref):
        start = pl.multiple_of(c * tk, tk)
        v = v_ref[pl.ds(start, tk), :]
        st = st_ref[...]
        m_prev = m_scr[...]
        m_next = jnp.maximum(m_prev, jnp.max(st, axis=0, keepdims=True))
        p = jnp.exp2(st - m_next).astype(BF16)
        alpha = jnp.exp2(m_prev - m_next)
        vt = jnp.concatenate([v.astype(F32).T.astype(BF16), ones], axis=0)
        acc_scr[...] = alpha * acc_scr[...] + jnp.dot(vt, p, preferred_element_type=F32)
        m_scr[...] = m_next

    scores(0, sa_scr)

    def body(i, carry):
        c = 2 * i
        scores(c + 1, sb_scr)
        consume(c, sa_scr)
        scores(c + 2, sa_scr)
        consume(c + 1, sb_scr)
        return carry

    lax.fori_loop(0, n_chunks // 2 - 1, body, 0)
    scores(n_chunks - 1, sb_scr)
    consume(n_chunks - 2, sa_scr)
    consume(n_chunks - 1, sb_scr)
    acc = acc_scr[...]
    out_t = acc[:HEAD_DIM] / acc[HEAD_DIM:HEAD_DIM + 1]
    for g in range(GQA_GROUP):
        o_ref[:, g * HEAD_DIM:(g + 1) * HEAD_DIM] = out_t[:, g * tq:(g + 1) * tq].T.astype(BF16)


def _attention(proj, nb, seq, att_w, kv_w, mm_dtype):
    t = proj.shape[0]
    tq = 512
    tk = min(1024, seq // 2)
    assert seq % (2 * tk) == 0
    n_kv = kv_w // HEAD_DIM
    per_b = seq // tq
    k_off = att_w // HEAD_DIM
    v_off = (att_w + kv_w) // HEAD_DIM
    cols = GQA_GROUP * tq
    return pl.pallas_call(
        functools.partial(_attn_kernel, tk=tk),
        grid=(nb, n_kv, per_b),
        in_specs=[pl.BlockSpec((tq, Q_PER_TILE), lambda b, g, i: (b * per_b + i, g)),
                  pl.BlockSpec((seq, HEAD_DIM), lambda b, g, i: (b, k_off + g)),
                  pl.BlockSpec((seq, HEAD_DIM), lambda b, g, i: (b, v_off + g))],
        out_specs=pl.BlockSpec((tq, Q_PER_TILE), lambda b, g, i: (b * per_b + i, g)),
        out_shape=jax.ShapeDtypeStruct((t, att_w), BF16),
        scratch_shapes=[pltpu.VMEM((1, cols), F32),
                        pltpu.VMEM((HEAD_DIM + ONES_ROWS, cols), F32),
                        pltpu.VMEM((HEAD_DIM, cols), mm_dtype),
                        pltpu.VMEM((tk, cols), F32),
                        pltpu.VMEM((tk, cols), F32)],
        compiler_params=_params(("arbitrary", "arbitrary", "arbitrary"), 48),
        name="attn",
    )(proj, proj, proj)


def _fft_a_kernel(m_ref, p_ref, q_ref, twc_ref, tws_ref, ar_ref, ai_ref, *, fw):
    n1 = p_ref.shape[1]
    rhs = jnp.concatenate([p_ref[0], q_ref[0]], axis=0)
    a = jnp.dot(m_ref[...], rhs, preferred_element_type=F32)
    rep = fw // LANES
    for u in range(p_ref.shape[2] // fw):
        ar = a[:n1, u * fw:(u + 1) * fw]
        ai = a[n1:, u * fw:(u + 1) * fw]
        c = jnp.tile(twc_ref[u], (1, rep))
        s = jnp.tile(tws_ref[u], (1, rep))
        ar_ref[0, :, u * fw:(u + 1) * fw] = (ar * c + ai * s).astype(BF16)
        ai_ref[0, :, u * fw:(u + 1) * fw] = (ai * c - ar * s).astype(BF16)


def _fft_c_kernel(cs_ref, ar_ref, ai_ref, o_ref, *, scale):
    rhs = jnp.concatenate([ar_ref[0, 0], ai_ref[0, 0]], axis=0)
    o_ref[0] = (jnp.dot(cs_ref[...], rhs, preferred_element_type=F32) * scale).astype(BF16)


def _dft_tables(n):
    idx = (np.arange(n)[:, None] * np.arange(n)[None, :]) % n
    ang = 2.0 * np.pi * idx / n
    return np.cos(ang), np.sin(ang)


def _seq_dft(p, q, nb, seq, gd):
    fw = p.shape[1]
    n1 = 64 if seq >= 4096 else 16
    n2 = seq // n1
    c1, s1 = _dft_tables(n1)
    c2, s2 = _dft_tables(n2)
    m_a = jnp.asarray(np.block([[c1, -s1], [-s1, -c1]]), BF16)
    cs_c = jnp.asarray(np.concatenate([c2, s2], axis=1), BF16)
    tw = 2.0 * np.pi * (np.arange(n2)[:, None] * np.arange(n1)[None, :]) / seq
    twc = jnp.asarray(np.broadcast_to(np.cos(tw)[:, :, None], (n2, n1, LANES)), F32)
    tws = jnp.asarray(np.broadcast_to(np.sin(tw)[:, :, None], (n2, n1, LANES)), F32)

    nt2 = 4
    tc = nt2 * fw
    p3 = p.reshape(nb, n1, n2 * fw)
    q3 = q.reshape(nb, n1, n2 * fw)
    a_shape = jax.ShapeDtypeStruct((nb, n1, n2 * fw), BF16)
    blk = pl.BlockSpec((1, n1, tc), lambda b, j: (b, 0, j))
    tw_blk = pl.BlockSpec((nt2, n1, LANES), lambda b, j: (j, 0, 0))
    ar, ai = pl.pallas_call(
        functools.partial(_fft_a_kernel, fw=fw),
        grid=(nb, n2 // nt2),
        in_specs=[pl.BlockSpec((2 * n1, 2 * n1), lambda b, j: (0, 0)), blk, blk, tw_blk, tw_blk],
        out_specs=[blk, blk],
        out_shape=[a_shape, a_shape],
        compiler_params=_params(("arbitrary", "arbitrary"), 40),
        name="fft_a",
    )(m_a, p3, q3, twc, tws)

    in_blk = pl.BlockSpec((1, 1, n2, fw), lambda b, k: (b, k, 0, 0))
    out = pl.pallas_call(
        functools.partial(_fft_c_kernel, scale=1.0 / math.sqrt(seq * gd)),
        grid=(nb, n1),
        in_specs=[pl.BlockSpec((n2, 2 * n2), lambda b, k: (0, 0)), in_blk, in_blk],
        out_specs=pl.BlockSpec((1, n2, fw), lambda b, k: (b, 0, k)),
        out_shape=jax.ShapeDtypeStruct((nb, n2, n1 * fw), BF16),
        compiler_params=_params(("arbitrary", "arbitrary"), 40),
        name="fft_c",
    )(cs_c, ar.reshape(nb, n1, n2, fw), ai.reshape(nb, n1, n2, fw))
    return out.reshape(nb * seq, fw)


def _merge_kernel(a_ref, f_ref, wa_ref, wf_ref, ga_ref, gf_ref, o_ref):
    a_br = jnp.dot(a_ref[...], wa_ref[...], preferred_element_type=F32)
    f_br = jnp.dot(f_ref[...], wf_ref[...], preferred_element_type=F32)
    o_ref[...] = (ga_ref[...].astype(F32) * a_br + gf_ref[...].astype(F32) * f_br).astype(BF16)


def _merge(att, four, wa, wf, proj, gate_off):
    t, aw = att.shape
    fw = four.shape[1]
    d = wa.shape[1]
    tm, tn = 512, 512
    ga0 = gate_off // tn
    gf0 = (gate_off + d) // tn
    return pl.pallas_call(
        _merge_kernel,
        grid=(t // tm, d // tn),
        in_specs=[pl.BlockSpec((tm, aw), lambda i, j: (i, 0)),
                  pl.BlockSpec((tm, fw), lambda i, j: (i, 0)),
                  pl.BlockSpec((aw, tn), lambda i, j: (0, j)),
                  pl.BlockSpec((fw, tn), lambda i, j: (0, j)),
                  pl.BlockSpec((tm, tn), lambda i, j: (i, ga0 + j)),
                  pl.BlockSpec((tm, tn), lambda i, j: (i, gf0 + j))],
        out_specs=pl.BlockSpec((tm, tn), lambda i, j: (i, j)),
        out_shape=jax.ShapeDtypeStruct((t, d), BF16),
        compiler_params=_params(("arbitrary", "arbitrary"), 48),
        name="merge",
    )(att, four, wa, wf, proj, proj)


def _outproj_kernel(m_ref, w_ref, x_ref, g_ref, o_ref):
    acc = jnp.dot(m_ref[...], w_ref[...], preferred_element_type=F32)
    o_ref[...] = x_ref[...] + g_ref[0] * acc


def _outproj(merged, w_out, x2d, g1, seq):
    t, d = merged.shape
    tm, tn = 512, 1024
    per_b = seq // tm
    return pl.pallas_call(
        _outproj_kernel,
        grid=(t // tm, d // tn),
        in_specs=[pl.BlockSpec((tm, d), lambda i, j: (i, 0)),
                  pl.BlockSpec((d, tn), lambda i, j: (0, j)),
                  pl.BlockSpec((tm, tn), lambda i, j: (i, j)),
                  pl.BlockSpec((1, 1, tn), lambda i, j: (i // per_b, 0, j))],
        out_specs=pl.BlockSpec((tm, tn), lambda i, j: (i, j)),
        out_shape=jax.ShapeDtypeStruct((t, d), F32),
        compiler_params=_params(("arbitrary", "arbitrary"), 48),
        name="outproj",
    )(merged, w_out, x2d, g1)


def _top_values(s, k, store):
    n = s.shape[0]
    iota = lax.broadcasted_iota(jnp.int32, s.shape, 0)
    cur = s
    for r in range(k):
        m = jnp.max(cur, axis=0, keepdims=True)
        store(r, m)
        first = jnp.min(jnp.where(cur == m, iota, n), axis=0, keepdims=True)
        cur = jnp.where(iota == first, NEG_INF, cur)


def _top_values_distinct(s, k, store):
    cur = s
    for r in range(k):
        m = jnp.max(cur, axis=0, keepdims=True)
        store(r, m)
        cur = jnp.where(cur == m, NEG_INF, cur)
    return jnp.sum(jnp.where(cur == NEG_INF, 1.0, 0.0), axis=0, keepdims=True)


def _peer_score_kernel(h_ref, wq_ref, keys_ref, g0_ref, phi_ref, e1_ref, top_scr, best_scr, s_scr):
    k1 = PEER_TOPK + 1
    qt = jnp.dot(wq_ref[...], h_ref[...], preferred_element_type=F32).astype(BF16)
    top_scr[...] = jnp.full(top_scr.shape, NEG_INF, F32)
    pad_rows = top_scr.shape[1] - k1
    s = [jnp.dot(keys_ref[0, p], qt[p * PEER_HALF:(p + 1) * PEER_HALF], preferred_element_type=F32)
         for p in range(2)]
    for p in range(2):
        s_scr[p] = s[p]

    def extract(top_fn):
        checks = []
        for c in range(s_scr.shape[2] // SCORE_COLS):
            cols = slice(c * SCORE_COLS, (c + 1) * SCORE_COLS)
            for p in range(2):
                def store(r, row, p=p):
                    top_scr[p, r:r + 1, cols] = row
                checks.append((top_fn(s_scr[p, :, cols], k1, store), k1))
            cand = jnp.concatenate(
                [top_scr[0, 0:1, cols] + top_scr[1, :, cols]]
                + [top_scr[0, i:i + 1, cols] + top_scr[1, 0:8, cols] for i in range(1, k1)], axis=0)

            def store_best(r, row):
                best_scr[r:r + 1, cols] = row
            checks.append((top_fn(cand, k1, store_best), k1 + pad_rows))
        return checks

    suspect = None
    for removed, expected in extract(_top_values_distinct):
        bad = jnp.where(removed != expected, 1.0, 0.0)
        suspect = bad if suspect is None else jnp.maximum(suspect, bad)

    @pl.when(jnp.max(suspect) > 0.0)
    def _():
        extract(_top_values)

    best = best_scr[0:PEER_TOPK]
    z = jnp.sum(jnp.exp(best - best[0:1]), axis=0, keepdims=True)
    tau = 0.5 * (best_scr[PEER_TOPK - 1:PEER_TOPK] + best_scr[PEER_TOPK:k1])
    max0 = top_scr[0, 0:1, :]
    max1 = top_scr[1, 0:1, :]
    g0_ref[0] = jnp.exp(s[0] - max0) / z
    phi_ref[0] = jnp.exp((tau - max1) - s[0])
    e1_ref[0] = jnp.exp(s[1] - max1).astype(BF16)


def _peer_scores(h2t, wq_t, keys):
    d, t = h2t.shape
    tm = 512
    top_rows = 24
    s_shape = jax.ShapeDtypeStruct((PEER_HEADS, PEER_NKEYS, t), F32)
    s_blk = pl.BlockSpec((1, PEER_NKEYS, tm), lambda i, h: (h, 0, i))
    return pl.pallas_call(
        _peer_score_kernel,
        grid=(t // tm, PEER_HEADS),
        in_specs=[pl.BlockSpec((d, tm), lambda i, h: (0, i)),
                  pl.BlockSpec((2 * PEER_HALF, d), lambda i, h: (h, 0)),
                  pl.BlockSpec((1, 2, PEER_NKEYS, PEER_HALF), lambda i, h: (h, 0, 0, 0))],
        out_specs=[s_blk, s_blk, s_blk],
        out_shape=[s_shape, s_shape, jax.ShapeDtypeStruct(s_shape.shape, BF16)],
        scratch_shapes=[pltpu.VMEM((2, top_rows, tm), F32), pltpu.VMEM((top_rows, tm), F32),
                        pltpu.VMEM((2, PEER_NKEYS, tm), F32)],
        compiler_params=_params(("arbitrary", "arbitrary"), 40),
        name="peer_scores",
    )(h2t, wq_t, keys)


def _peer_w_kernel(h_ref, hs_ref, u_ref, g0_ref, phi_ref, e1_ref, w_ref, wmax_ref, st_scr, *, n_tiles):
    j = pl.program_id(1)
    tm, te = w_ref.shape
    rows = te // PEER_NKEYS

    @pl.when(j == 0)
    def _():
        st_scr[1] = jnp.zeros(st_scr.shape[1:], F32)
        wmax_ref[...] = jnp.zeros(wmax_ref.shape, F32)

    gate_tile = jnp.clip(j - 1, 0, n_tiles - 1)

    half = tm // 2

    def step(par):
        def body(a, carry):
            ts = pl.multiple_of(a * half, half)
            st_scr[par, :, pl.ds(ts, half)] = jnp.dot(u_ref[...], h_ref[:, pl.ds(ts, half)],
                                                      preferred_element_type=F32)
            inv_scale = hs_ref[:, pl.ds(ts, half)]
            for r in range(rows):
                n0 = gate_tile * rows + r
                ks = slice(r * PEER_NKEYS, (r + 1) * PEER_NKEYS)
                st = st_scr[1 - par, ks, pl.ds(ts, half)] * inv_scale
                act = 0.5 * st * (1.0 + lax.erf(st * math.sqrt(0.5)))
                gate = None
                for hd in range(PEER_HEADS):
                    e1 = e1_ref[hd, :, pl.ds(ts, half)]
                    phi = phi_ref[hd, pl.ds(n0, 1), pl.ds(ts, half)].astype(BF16)
                    g0 = g0_ref[hd, pl.ds(n0, 1), pl.ds(ts, half)].astype(BF16)
                    g = jnp.where(e1 >= phi, e1 * g0, jnp.zeros_like(e1))
                    gate = g if gate is None else gate + g
                wt = act * gate.astype(F32)
                w_ref[pl.ds(ts, half), ks] = wt.T.astype(BF16)
                wmax_ref[:, pl.ds(ts, half)] = jnp.maximum(wmax_ref[:, pl.ds(ts, half)],
                                                           jnp.max(jnp.abs(wt), axis=0, keepdims=True))
            return carry

        lax.fori_loop(0, 2, body, 0)

    @pl.when(j % 2 == 0)
    def _():
        step(0)

    @pl.when(j % 2 == 1)
    def _():
        step(1)


def _peer_weights(h2t, hs, u_q, g0, phi, e1):
    d, t = h2t.shape
    n_exp = u_q.shape[0]
    tm, te = 512, 512
    n_tiles = n_exp // te
    once = pl.Buffered(1)
    sc_blk = pl.BlockSpec((PEER_HEADS, PEER_NKEYS, tm), lambda i, j: (0, 0, i), pipeline_mode=once)
    return pl.pallas_call(
        functools.partial(_peer_w_kernel, n_tiles=n_tiles),
        grid=(t // tm, n_tiles + 1),
        in_specs=[pl.BlockSpec((d, tm), lambda i, j: (0, i), pipeline_mode=once),
                  pl.BlockSpec((1, tm), lambda i, j: (0, i)),
                  pl.BlockSpec((te, d), lambda i, j: (jnp.minimum(j, n_tiles - 1), 0)),
                  sc_blk, sc_blk, sc_blk],
        out_specs=[pl.BlockSpec((tm, te), lambda i, j: (i, jnp.maximum(j - 1, 0))),
                   pl.BlockSpec((1, tm), lambda i, j: (0, i))],
        out_shape=[jax.ShapeDtypeStruct((t, n_exp), BF16), jax.ShapeDtypeStruct((1, t), F32)],
        scratch_shapes=[pltpu.VMEM((2, te, tm), F32)],
        compiler_params=_params(("arbitrary", "arbitrary"), 48),
        name="peer_w",
    )(h2t, hs, u_q, g0, phi, e1)


def _peer_v_kernel(w_ref, sc_ref, dq_ref, v_ref, o_ref, wq_scr, *, chunk):
    @pl.when(pl.program_id(1) == 0)
    def _():
        for c in range(w_ref.shape[1] // chunk):
            cols = slice(c * chunk, (c + 1) * chunk)
            wq_scr[:, cols] = (w_ref[:, cols].astype(F32) * sc_ref[...]).astype(FP8)

    o_ref[...] = jnp.dot(wq_scr[...], v_ref[...], preferred_element_type=F32) * dq_ref[...]


def _peer_values(w, wmax, v_q, v_inv):
    t, n_exp = w.shape
    d = v_q.shape[1]
    tm, tn = 512, min(512, d)
    wmax_col = jnp.maximum(wmax.reshape(t, 1), FP8_TINY)
    scale = FP8_TARGET / wmax_col
    dequant = wmax_col * (v_inv / FP8_TARGET)
    col = pl.BlockSpec((tm, 1), lambda i, j: (i, 0))
    return pl.pallas_call(
        functools.partial(_peer_v_kernel, chunk=min(2048, n_exp)),
        grid=(t // tm, d // tn),
        in_specs=[pl.BlockSpec((tm, n_exp), lambda i, j: (i, 0), pipeline_mode=pl.Buffered(1)),
                  col, col,
                  pl.BlockSpec((n_exp, tn), lambda i, j: (0, j))],
        out_specs=pl.BlockSpec((tm, tn), lambda i, j: (i, j)),
        out_shape=jax.ShapeDtypeStruct((t, d), F32),
        scratch_shapes=[pltpu.VMEM((tm, n_exp), FP8)],
        compiler_params=_params(("arbitrary", "arbitrary"), 48),
        name="peer_v",
    )(w, scale, dequant, v_q)


def _final_kernel(x_ref, p_ref, g2_ref, fg_ref, o_ref):
    x = x_ref[...] + g2_ref[0] * p_ref[...]
    ms = jnp.mean(x * x, axis=-1, keepdims=True)
    o_ref[...] = x * lax.rsqrt(ms + EPS) * fg_ref[...]


def _final(x1, peer, g2, final_g, seq):
    t, d = x1.shape
    tm = 256
    per_b = seq // tm
    row = pl.BlockSpec((tm, d), lambda i: (i, 0))
    return pl.pallas_call(
        _final_kernel,
        grid=(t // tm,),
        in_specs=[row, row,
                  pl.BlockSpec((1, 1, d), lambda i: (i // per_b, 0, 0)),
                  pl.BlockSpec((1, d), lambda i: (0, 0))],
        out_specs=row,
        out_shape=jax.ShapeDtypeStruct((t, d), F32),
        compiler_params=_params(("arbitrary",), 48),
        name="final",
    )(x1, peer, g2, final_g.reshape(1, d))


def _rope_tables(seq):
    rows = seq // GRID_W
    row_ids = jnp.repeat(jnp.arange(rows), GRID_W).astype(F32)
    col_ids = jnp.tile(jnp.arange(GRID_W), rows).astype(F32)
    axis_dim = HEAD_DIM // 2
    inv_freq = ROPE_THETA ** (-jnp.arange(0, axis_dim, 2, dtype=F32) / axis_dim)
    ang = jnp.concatenate([row_ids[:, None] * inv_freq, col_ids[:, None] * inv_freq], axis=-1)
    cos, sin = jnp.cos(ang), jnp.sin(ang)
    return jnp.concatenate([cos, cos], axis=-1), jnp.concatenate([-sin, sin], axis=-1)


def _trunk(x, mod, wts, dims):
    nb, seq, d = x.shape
    att_w, kv_w, f_w, gd = dims
    x2d = x.reshape(nb * seq, d)
    sh1, sc1, g1, sh2, sc2, g2 = [mod[:, k].reshape(nb, 1, d) for k in range(6)]
    cc, ss = _rope_tables(seq)

    h = _ln_mod(x2d, wts["norm1_g"], sc1, sh1, seq, transpose=False)
    proj = _inproj(h, wts["w_main"], wts["qg"], wts["kg"], cc, ss, seq, att_w, kv_w)
    p, q = _fproj(h, wts["w_f"], wts["cs_ch"], gd)
    att = lax.cond(wts["qk_fp8_ok"],
                   lambda pr: _attention(pr, nb, seq, att_w, kv_w, FP8),
                   lambda pr: _attention(pr, nb, seq, att_w, kv_w, BF16), proj)
    four = _seq_dft(p, q, nb, seq, gd)
    merged = _merge(att, four, wts["wa"], wts["wf"], proj, att_w + 2 * kv_w)
    x1 = _outproj(merged, wts["w_out"], x2d, g1, seq)

    h2t, h2t_q, h2_inv = _ln_mod(x1, wts["norm2_g"], sc2, sh2, seq, transpose=True)
    g0, phi, e1 = _peer_scores(h2t, wts["wq_t"], wts["keys"])
    w, wmax = _peer_weights(h2t_q, h2_inv * wts["u_inv"], wts["u_q"], g0, phi, e1)
    peer = _peer_values(w, wmax, wts["v_q"], wts["v_inv"])
    y = _final(x1, peer, g2, wts["final_g"], seq)
    return y.reshape(nb, seq, d)


def kernel(x_prompt, x_sample, c_prompt, c_sample, w_ada, b_ada, norm1_g, norm2_g, w_in, q_norm_g, k_norm_g, w_attn_br, w_four_br, w_out, w_peer_q, peer_keys, peer_u, peer_v, final_g):
    d = x_prompt.shape[-1]
    assert w_ada.shape[0] == 1, "single-layer trunk"
    att_w = w_attn_br.shape[1]
    f_w = w_four_br.shape[1]
    kv_w = (w_in.shape[2] - att_w - f_w - 2 * d) // 2
    gd = f_w // N_FGROUPS
    dims = (att_w, kv_w, f_w, gd)

    perm = np.concatenate([np.arange(0, HEAD_DIM, 2), np.arange(1, HEAD_DIM, 2)])
    w = w_in[0]

    def deinterleave(cols):
        return cols.reshape(d, -1, HEAD_DIM)[:, :, perm].reshape(d, -1)

    s_q, s_k, s_v, s_f = att_w, att_w + kv_w, att_w + 2 * kv_w, att_w + 2 * kv_w + f_w
    w_main = jnp.concatenate([deinterleave(w[:, :s_q]), deinterleave(w[:, s_q:s_k]), w[:, s_k:s_v],
                              w[:, s_f:]], axis=1).astype(BF16)
    cch, sch = _dft_tables(gd)
    q_bound = jnp.maximum(jnp.max(jnp.abs(q_norm_g[0])) * LOG2E, FP8_TINY)
    k_bound = jnp.maximum(jnp.max(jnp.abs(k_norm_g[0])) * math.sqrt(HEAD_DIM), FP8_TINY)
    qk_fp8_ok = jnp.sqrt(2.0 * q_bound * k_bound) < FP8_TARGET
    qk_bal = jnp.where(qk_fp8_ok, jnp.exp2(jnp.round(0.5 * jnp.log2(k_bound / q_bound))), 1.0)
    u_max = jnp.maximum(jnp.max(jnp.abs(peer_u[0])), FP8_TINY)
    v_max = jnp.maximum(jnp.max(jnp.abs(peer_v[0])), FP8_TINY)
    wts = {
        "norm1_g": norm1_g[0], "norm2_g": norm2_g[0], "final_g": final_g,
        "w_main": w_main,
        "w_f": w[:, s_v:s_f].astype(BF16),
        "cs_ch": jnp.asarray(np.concatenate([cch, sch], axis=1), BF16),
        "qg": (q_norm_g[0][perm] * (HEAD_DIM ** -0.5 * LOG2E * qk_bal)).reshape(1, HEAD_DIM),
        "kg": (k_norm_g[0][perm] / qk_bal).reshape(1, HEAD_DIM),
        "qk_fp8_ok": qk_fp8_ok,
        "wa": w_attn_br[0].astype(BF16), "wf": w_four_br[0].astype(BF16),
        "w_out": w_out[0].astype(BF16),
        "wq_t": w_peer_q[0].T.astype(BF16),
        "keys": peer_keys[0].astype(BF16),
        "u_q": (peer_u[0] * (FP8_TARGET / u_max)).astype(FP8), "u_inv": u_max * (1.0 / FP8_TARGET),
        "v_q": (peer_v[0] * (FP8_TARGET / v_max)).astype(FP8), "v_inv": v_max * (1.0 / FP8_TARGET),
    }

    nbp, nbs = c_prompt.shape[0], c_sample.shape[0]
    c_all = jnp.concatenate([c_prompt, c_sample], axis=0)
    c_pad = jnp.pad(c_all, ((0, (-c_all.shape[0]) % 8), (0, 0)))
    mod = _modulation(c_pad, w_ada[0], b_ada[0]).reshape(c_pad.shape[0], 6, d)

    y_prompt = _trunk(x_prompt, mod[:nbp], wts, dims)
    y_sample = _trunk(x_sample, mod[nbp:nbp + nbs], wts, dims)
    return (y_prompt, y_sample)
```

```python
import functools
import math

import jax
import jax.numpy as jnp
import numpy as np
from jax import lax
from jax.experimental import pallas as pl
from jax.experimental.pallas import tpu as pltpu

F32 = jnp.float32
BF16 = jnp.bfloat16
FP8 = jnp.float8_e4m3fn
FP8_TARGET = 256.0
FP8_TINY = 1e-30

HEAD_DIM = 128
GQA_GROUP = 4
Q_PER_TILE = GQA_GROUP * HEAD_DIM
ROPE_THETA = 10000.0
GRID_W = 64
N_FGROUPS = 8
PEER_HEADS = 8
PEER_NKEYS = 128
PEER_HALF = 128
PEER_TOPK = 16
SCORE_COLS = 128
EPS = 1e-6
LANES = 128
NEG_INF = float("-inf")
ONES_ROWS = 16
LOG2E = math.log2(math.e)
MIB = 1024 * 1024


def _params(sem, vmem_mib):
    return pltpu.CompilerParams(dimension_semantics=sem, vmem_limit_bytes=vmem_mib * MIB)


def _mod_kernel(c_ref, w_ref, b_ref, o_ref):
    c = c_ref[...]
    a = c * jax.nn.sigmoid(c)
    o_ref[...] = jnp.dot(a, w_ref[...], preferred_element_type=F32,
                         precision=lax.Precision.HIGHEST) + b_ref[...]


def _modulation(c_pad, w_ada, b_ada):
    rows, d = c_pad.shape
    n = w_ada.shape[1]
    tn = 512
    return pl.pallas_call(
        _mod_kernel,
        grid=(n // tn,),
        in_specs=[pl.BlockSpec((rows, d), lambda j: (0, 0)),
                  pl.BlockSpec((d, tn), lambda j: (0, j)),
                  pl.BlockSpec((1, tn), lambda j: (0, j))],
        out_specs=pl.BlockSpec((rows, tn), lambda j: (0, j)),
        out_shape=jax.ShapeDtypeStruct((rows, n), F32),
        compiler_params=_params(("arbitrary",), 40),
        name="mod",
    )(c_pad, w_ada, b_ada.reshape(1, n))


def _ln_rows(x_ref, g_ref, sc_ref, sh_ref):
    x = x_ref[...]
    ms = jnp.mean(x * x, axis=-1, keepdims=True)
    y = x * lax.rsqrt(ms + EPS) * g_ref[...]
    return y * (1.0 + sc_ref[0]) + sh_ref[0]


def _ln_kernel(x_ref, g_ref, sc_ref, sh_ref, o_ref):
    o_ref[...] = _ln_rows(x_ref, g_ref, sc_ref, sh_ref).astype(BF16)


def _ln_t_kernel(x_ref, g_ref, sc_ref, sh_ref, o_ref, q_ref, s_ref, ht_scr, *, chunk):
    h = _ln_rows(x_ref, g_ref, sc_ref, sh_ref)
    amax = jnp.full(s_ref.shape, FP8_TINY, F32)
    for c in range(h.shape[1] // chunk):
        rows = slice(c * chunk, (c + 1) * chunk)
        blk = h[:, rows].T
        ht_scr[rows, :] = blk
        o_ref[rows, :] = blk.astype(BF16)
        amax = jnp.maximum(amax, jnp.max(jnp.abs(blk), axis=0, keepdims=True))
    q_ref[...] = (ht_scr[...] * (FP8_TARGET / amax)).astype(FP8)
    s_ref[...] = amax * (1.0 / FP8_TARGET)


def _ln_mod(x2d, g, sc, sh, seq, *, transpose):
    t, d = x2d.shape
    tm = 256
    per_b = seq // tm
    in_specs = [pl.BlockSpec((tm, d), lambda i: (i, 0)),
                pl.BlockSpec((1, d), lambda i: (0, 0)),
                pl.BlockSpec((1, 1, d), lambda i: (i // per_b, 0, 0)),
                pl.BlockSpec((1, 1, d), lambda i: (i // per_b, 0, 0))]
    args = (x2d, g.reshape(1, d), sc, sh)
    if not transpose:
        return pl.pallas_call(
            _ln_kernel, grid=(t // tm,), in_specs=in_specs,
            out_specs=pl.BlockSpec((tm, d), lambda i: (i, 0)),
            out_shape=jax.ShapeDtypeStruct((t, d), BF16),
            compiler_params=_params(("arbitrary",), 40), name="ln",
        )(*args)
    col_blk = pl.BlockSpec((d, tm), lambda i: (0, i))
    return pl.pallas_call(
        functools.partial(_ln_t_kernel, chunk=min(512, d)),
        grid=(t // tm,), in_specs=in_specs,
        out_specs=[col_blk, col_blk, pl.BlockSpec((1, tm), lambda i: (0, i))],
        out_shape=[jax.ShapeDtypeStruct((d, t), BF16), jax.ShapeDtypeStruct((d, t), FP8),
                   jax.ShapeDtypeStruct((1, t), F32)],
        scratch_shapes=[pltpu.VMEM((d, tm), F32)],
        compiler_params=_params(("arbitrary",), 40), name="ln_t",
    )(*args)


def _norm_rope(acc, g, cc, ss, o_ref):
    for hd in range(acc.shape[1] // HEAD_DIM):
        xh = acc[:, hd * HEAD_DIM:(hd + 1) * HEAD_DIM]
        ms = jnp.mean(xh * xh, axis=-1, keepdims=True)
        y = xh * lax.rsqrt(ms + EPS) * g
        y = y * cc + pltpu.roll(y, HEAD_DIM // 2, axis=1) * ss
        o_ref[:, hd * HEAD_DIM:(hd + 1) * HEAD_DIM] = y.astype(BF16)


def _inproj_kernel(h_ref, w_ref, qg_ref, kg_ref, cc_ref, ss_ref, o_ref, *, nq, nk, nv):
    j = pl.program_id(1)
    acc = jnp.dot(h_ref[...], w_ref[...], preferred_element_type=F32)

    @pl.when(j < nq)
    def _():
        _norm_rope(acc, qg_ref[...], cc_ref[...], ss_ref[...], o_ref)

    @pl.when(jnp.logical_and(j >= nq, j < nq + nk))
    def _():
        _norm_rope(acc, kg_ref[...], cc_ref[...], ss_ref[...], o_ref)

    @pl.when(jnp.logical_and(j >= nq + nk, j < nq + nk + nv))
    def _():
        o_ref[...] = acc.astype(BF16)

    @pl.when(j >= nq + nk + nv)
    def _():
        o_ref[...] = (0.5 * jnp.tanh(0.5 * acc) + 0.5).astype(BF16)


def _inproj(h, w_main, qg, kg, cc, ss, seq, att_w, kv_w):
    t, d = h.shape
    wtot = w_main.shape[1]
    tm = 1024
    tn = min(1024, kv_w)
    per_b = seq // tm
    kern = functools.partial(_inproj_kernel, nq=att_w // tn, nk=kv_w // tn, nv=kv_w // tn)
    return pl.pallas_call(
        kern,
        grid=(t // tm, wtot // tn),
        in_specs=[pl.BlockSpec((tm, d), lambda i, j: (i, 0)),
                  pl.BlockSpec((d, tn), lambda i, j: (0, j)),
                  pl.BlockSpec((1, HEAD_DIM), lambda i, j: (0, 0)),
                  pl.BlockSpec((1, HEAD_DIM), lambda i, j: (0, 0)),
                  pl.BlockSpec((tm, HEAD_DIM), lambda i, j: (i % per_b, 0)),
                  pl.BlockSpec((tm, HEAD_DIM), lambda i, j: (i % per_b, 0))],
        out_specs=pl.BlockSpec((tm, tn), lambda i, j: (i, j)),
        out_shape=jax.ShapeDtypeStruct((t, wtot), BF16),
        compiler_params=_params(("arbitrary", "arbitrary"), 48),
        name="inproj",
    )(h, w_main, qg, kg, cc, ss)


def _fproj_kernel(h_ref, w_ref, cs_ref, p_ref, q_ref, *, gd):
    acc = jnp.dot(h_ref[...], w_ref[...], preferred_element_type=F32)
    for g in range(acc.shape[1] // gd):
        u = acc[:, g * gd:(g + 1) * gd].astype(BF16)
        pq = jnp.dot(u, cs_ref[...], preferred_element_type=F32)
        p_ref[:, g * gd:(g + 1) * gd] = pq[:, :gd].astype(BF16)
        q_ref[:, g * gd:(g + 1) * gd] = pq[:, gd:].astype(BF16)


def _fproj(h, w_f, cs, gd):
    t, d = h.shape
    fw = w_f.shape[1]
    tm = 512
    tn = min(1024, fw)
    out = jax.ShapeDtypeStruct((t, fw), BF16)
    return pl.pallas_call(
        functools.partial(_fproj_kernel, gd=gd),
        grid=(t // tm, fw // tn),
        in_specs=[pl.BlockSpec((tm, d), lambda i, j: (i, 0)),
                  pl.BlockSpec((d, tn), lambda i, j: (0, j)),
                  pl.BlockSpec((gd, 2 * gd), lambda i, j: (0, 0))],
        out_specs=[pl.BlockSpec((tm, tn), lambda i, j: (i, j)),
                   pl.BlockSpec((tm, tn), lambda i, j: (i, j))],
        out_shape=[out, out],
        compiler_params=_params(("arbitrary", "arbitrary"), 48),
        name="fproj",
    )(h, w_f, cs)


def _attn_kernel(q_ref, k_ref, v_ref, o_ref, m_scr, acc_scr, qt_scr, sa_scr, sb_scr, *, tk):
    tq = q_ref.shape[0]
    seq = k_ref.shape[0]
    n_chunks = seq // tk
    mm_dtype = qt_scr.dtype
    for g in range(GQA_GROUP):
        qt_scr[:, g * tq:(g + 1) * tq] = q_ref[:, g * HEAD_DIM:(g + 1) * HEAD_DIM].astype(F32).T.astype(mm_dtype)
    m_scr[...] = jnp.full(m_scr.shape, NEG_INF, F32)
    acc_scr[...] = jnp.zeros(acc_scr.shape, F32)
    ones = jnp.ones((ONES_ROWS, tk), BF16)

    def scores(c, st_ref):
        start = pl.multiple_of(c * tk, tk)
        st_ref[...] = jnp.dot(k_ref[pl.ds(start, tk), :].astype(mm_dtype), qt_scr[...],
                              preferred_element_type=F32)

    def consume(c, st_ref):
        start = pl.multiple_of(c * tk, tk)
        v = v_ref[pl.ds(start, tk), :]
        st = st_ref[...]
        m_prev = m_scr[...]
        m_next = jnp.maximum(m_prev, jnp.max(st, axis=0, keepdims=True))
        p = jnp.exp2(st - m_next).astype(BF16)
        alpha = jnp.exp2(m_prev - m_next)
        vt = jnp.concatenate([v.astype(F32).T.astype(BF16), ones], axis=0)
        acc_scr[...] = alpha * acc_scr[...] + jnp.dot(vt, p, preferred_element_type=F32)
        m_scr[...] = m_next

    scores(0, sa_scr)

    def body(i, carry):
        c = 2 * i
        scores(c + 1, sb_scr)
        consume(c, sa_scr)
        scores(c + 2, sa_scr)
        consume(c + 1, sb_scr)
        return carry

    lax.fori_loop(0, n_chunks // 2 - 1, body, 0)
    scores(n_chunks - 1, sb_scr)
    consume(n_chunks - 2, sa_scr)
    consume(n_chunks - 1, sb_scr)
    acc = acc_scr[...]
    out_t = acc[:HEAD_DIM] / acc[HEAD_DIM:HEAD_DIM + 1]
    for g in range(GQA_GROUP):
        o_ref[:, g * HEAD_DIM:(g + 1) * HEAD_DIM] = out_t[:, g * tq:(g + 1) * tq].T.astype(BF16)


def _attention(proj, nb, seq, att_w, kv_w, mm_dtype):
    t = proj.shape[0]
    tq = 512
    tk = min(1024, seq // 2)
    assert seq % (2 * tk) == 0
    n_kv = kv_w // HEAD_DIM
    per_b = seq // tq
    k_off = att_w // HEAD_DIM
    v_off = (att_w + kv_w) // HEAD_DIM
    cols = GQA_GROUP * tq
    return pl.pallas_call(
        functools.partial(_attn_kernel, tk=tk),
        grid=(nb, n_kv, per_b),
        in_specs=[pl.BlockSpec((tq, Q_PER_TILE), lambda b, g, i: (b * per_b + i, g)),
                  pl.BlockSpec((seq, HEAD_DIM), lambda b, g, i: (b, k_off + g)),
                  pl.BlockSpec((seq, HEAD_DIM), lambda b, g, i: (b, v_off + g))],
        out_specs=pl.BlockSpec((tq, Q_PER_TILE), lambda b, g, i: (b * per_b + i, g)),
        out_shape=jax.ShapeDtypeStruct((t, att_w), BF16),
        scratch_shapes=[pltpu.VMEM((1, cols), F32),
                        pltpu.VMEM((HEAD_DIM + ONES_ROWS, cols), F32),
                        pltpu.VMEM((HEAD_DIM, cols), mm_dtype),
                        pltpu.VMEM((tk, cols), F32),
                        pltpu.VMEM((tk, cols), F32)],
        compiler_params=_params(("arbitrary", "arbitrary", "arbitrary"), 48),
        name="attn",
    )(proj, proj, proj)


def _fft_a_kernel(m_ref, p_ref, q_ref, twc_ref, tws_ref, ar_ref, ai_ref, *, fw):
    n1 = p_ref.shape[1]
    rhs = jnp.concatenate([p_ref[0], q_ref[0]], axis=0)
    a = jnp.dot(m_ref[...], rhs, preferred_element_type=F32)
    rep = fw // LANES
    for u in range(p_ref.shape[2] // fw):
        ar = a[:n1, u * fw:(u + 1) * fw]
        ai = a[n1:, u * fw:(u + 1) * fw]
        c = jnp.tile(twc_ref[u], (1, rep))
        s = jnp.tile(tws_ref[u], (1, rep))
        ar_ref[0, :, u * fw:(u + 1) * fw] = (ar * c + ai * s).astype(BF16)
        ai_ref[0, :, u * fw:(u + 1) * fw] = (ai * c - ar * s).astype(BF16)


def _fft_c_kernel(cs_ref, ar_ref, ai_ref, o_ref, *, scale):
    rhs = jnp.concatenate([ar_ref[0, 0], ai_ref[0, 0]], axis=0)
    o_ref[0] = (jnp.dot(cs_ref[...], rhs, preferred_element_type=F32) * scale).astype(BF16)


def _dft_tables(n):
    idx = (np.arange(n)[:, None] * np.arange(n)[None, :]) % n
    ang = 2.0 * np.pi * idx / n
    return np.cos(ang), np.sin(ang)


def _seq_dft(p, q, nb, seq, gd):
    fw = p.shape[1]
    n1 = 64 if seq >= 4096 else 16
    n2 = seq // n1
    c1, s1 = _dft_tables(n1)
    c2, s2 = _dft_tables(n2)
    m_a = jnp.asarray(np.block([[c1, -s1], [-s1, -c1]]), BF16)
    cs_c = jnp.asarray(np.concatenate([c2, s2], axis=1), BF16)
    tw = 2.0 * np.pi * (np.arange(n2)[:, None] * np.arange(n1)[None, :]) / seq
    twc = jnp.asarray(np.broadcast_to(np.cos(tw)[:, :, None], (n2, n1, LANES)), F32)
    tws = jnp.asarray(np.broadcast_to(np.sin(tw)[:, :, None], (n2, n1, LANES)), F32)

    nt2 = 4
    tc = nt2 * fw
    p3 = p.reshape(nb, n1, n2 * fw)
    q3 = q.reshape(nb, n1, n2 * fw)
    a_shape = jax.ShapeDtypeStruct((nb, n1, n2 * fw), BF16)
    blk = pl.BlockSpec((1, n1, tc), lambda b, j: (b, 0, j))
    tw_blk = pl.BlockSpec((nt2, n1, LANES), lambda b, j: (j, 0, 0))
    ar, ai = pl.pallas_call(
        functools.partial(_fft_a_kernel, fw=fw),
        grid=(nb, n2 // nt2),
        in_specs=[pl.BlockSpec((2 * n1, 2 * n1), lambda b, j: (0, 0)), blk, blk, tw_blk, tw_blk],
        out_specs=[blk, blk],
        out_shape=[a_shape, a_shape],
        compiler_params=_params(("arbitrary", "arbitrary"), 40),
        name="fft_a",
    )(m_a, p3, q3, twc, tws)

    in_blk = pl.BlockSpec((1, 1, n2, fw), lambda b, k: (b, k, 0, 0))
    out = pl.pallas_call(
        functools.partial(_fft_c_kernel, scale=1.0 / math.sqrt(seq * gd)),
        grid=(nb, n1),
        in_specs=[pl.BlockSpec((n2, 2 * n2), lambda b, k: (0, 0)), in_blk, in_blk],
        out_specs=pl.BlockSpec((1, n2, fw), lambda b, k: (b, 0, k)),
        out_shape=jax.ShapeDtypeStruct((nb, n2, n1 * fw), BF16),
        compiler_params=_params(("arbitrary", "arbitrary"), 40),
        name="fft_c",
    )(cs_c, ar.reshape(nb, n1, n2, fw), ai.reshape(nb, n1, n2, fw))
    return out.reshape(nb * seq, fw)


def _merge_kernel(a_ref, f_ref, wa_ref, wf_ref, ga_ref, gf_ref, o_ref):
    a_br = jnp.dot(a_ref[...], wa_ref[...], preferred_element_type=F32)
    f_br = jnp.dot(f_ref[...], wf_ref[...], preferred_element_type=F32)
    o_ref[...] = (ga_ref[...].astype(F32) * a_br + gf_ref[...].astype(F32) * f_br).astype(BF16)


def _merge(att, four, wa, wf, proj, gate_off):
    t, aw = att.shape
    fw = four.shape[1]
    d = wa.shape[1]
    tm, tn = 512, 512
    ga0 = gate_off // tn
    gf0 = (gate_off + d) // tn
    return pl.pallas_call(
        _merge_kernel,
        grid=(t // tm, d // tn),
        in_specs=[pl.BlockSpec((tm, aw), lambda i, j: (i, 0)),
                  pl.BlockSpec((tm, fw), lambda i, j: (i, 0)),
                  pl.BlockSpec((aw, tn), lambda i, j: (0, j)),
                  pl.BlockSpec((fw, tn), lambda i, j: (0, j)),
                  pl.BlockSpec((tm, tn), lambda i, j: (i, ga0 + j)),
                  pl.BlockSpec((tm, tn), lambda i, j: (i, gf0 + j))],
        out_specs=pl.BlockSpec((tm, tn), lambda i, j: (i, j)),
        out_shape=jax.ShapeDtypeStruct((t, d), BF16),
        compiler_params=_params(("arbitrary", "arbitrary"), 48),
        name="merge",
    )(att, four, wa, wf, proj, proj)


def _outproj_kernel(m_ref, w_ref, x_ref, g_ref, o_ref):
    acc = jnp.dot(m_ref[...], w_ref[...], preferred_element_type=F32)
    o_ref[...] = x_ref[...] + g_ref[0] * acc


def _outproj(merged, w_out, x2d, g1, seq):
    t, d = merged.shape
    tm, tn = 512, 1024
    per_b = seq // tm
    return pl.pallas_call(
        _outproj_kernel,
        grid=(t // tm, d // tn),
        in_specs=[pl.BlockSpec((tm, d), lambda i, j: (i, 0)),
                  pl.BlockSpec((d, tn), lambda i, j: (0, j)),
                  pl.BlockSpec((tm, tn), lambda i, j: (i, j)),
                  pl.BlockSpec((1, 1, tn), lambda i, j: (i // per_b, 0, j))],
        out_specs=pl.BlockSpec((tm, tn), lambda i, j: (i, j)),
        out_shape=jax.ShapeDtypeStruct((t, d), F32),
        compiler_params=_params(("arbitrary", "arbitrary"), 48),
        name="outproj",
    )(merged, w_out, x2d, g1)


def _top_values(s, k, store):
    n = s.shape[0]
    iota = lax.broadcasted_iota(jnp.int32, s.shape, 0)
    cur = s
    for r in range(k):
        m = jnp.max(cur, axis=0, keepdims=True)
        store(r, m)
        first = jnp.min(jnp.where(cur == m, iota, n), axis=0, keepdims=True)
        cur = jnp.where(iota == first, NEG_INF, cur)


def _top_values_distinct(s, k, store):
    cur = s
    for r in range(k):
        m = jnp.max(cur, axis=0, keepdims=True)
        store(r, m)
        cur = jnp.where(cur == m, NEG_INF, cur)
    return jnp.sum(jnp.where(cur == NEG_INF, 1.0, 0.0), axis=0, keepdims=True)


def _peer_score_kernel(h_ref, wq_ref, keys_ref, g0_ref, phi_ref, e1_ref, top_scr, best_scr, s_scr):
    k1 = PEER_TOPK + 1
    qt = jnp.dot(wq_ref[...], h_ref[...], preferred_element_type=F32).astype(BF16)
    top_scr[...] = jnp.full(top_scr.shape, NEG_INF, F32)
    pad_rows = top_scr.shape[1] - k1
    s = [jnp.dot(keys_ref[0, p], qt[p * PEER_HALF:(p + 1) * PEER_HALF], preferred_element_type=F32)
         for p in range(2)]
    for p in range(2):
        s_scr[p] = s[p]

    def extract(top_fn):
        checks = []
        for c in range(s_scr.shape[2] // SCORE_COLS):
            cols = slice(c * SCORE_COLS, (c + 1) * SCORE_COLS)
            for p in range(2):
                def store(r, row, p=p):
                    top_scr[p, r:r + 1, cols] = row
                checks.append((top_fn(s_scr[p, :, cols], k1, store), k1))
            cand = jnp.concatenate(
                [top_scr[0, 0:1, cols] + top_scr[1, :, cols]]
                + [top_scr[0, i:i + 1, cols] + top_scr[1, 0:8, cols] for i in range(1, k1)], axis=0)

            def store_best(r, row):
                best_scr[r:r + 1, cols] = row
            checks.append((top_fn(cand, k1, store_best), k1 + pad_rows))
        return checks

    suspect = None
    for removed, expected in extract(_top_values_distinct):
        bad = jnp.where(removed != expected, 1.0, 0.0)
        suspect = bad if suspect is None else jnp.maximum(suspect, bad)

    @pl.when(jnp.max(suspect) > 0.0)
    def _():
        extract(_top_values)

    best = best_scr[0:PEER_TOPK]
    z = jnp.sum(jnp.exp(best - best[0:1]), axis=0, keepdims=True)
    tau = 0.5 * (best_scr[PEER_TOPK - 1:PEER_TOPK] + best_scr[PEER_TOPK:k1])
    max0 = top_scr[0, 0:1, :]
    max1 = top_scr[1, 0:1, :]
    g0_ref[0] = jnp.exp(s[0] - max0) / z
    phi_ref[0] = jnp.exp((tau - max1) - s[0])
    e1_ref[0] = jnp.exp(s[1] - max1).astype(BF16)


def _peer_scores(h2t, wq_t, keys):
    d, t = h2t.shape
    tm = 512
    top_rows = 24
    s_shape = jax.ShapeDtypeStruct((PEER_HEADS, PEER_NKEYS, t), F32)
    s_blk = pl.BlockSpec((1, PEER_NKEYS, tm), lambda i, h: (h, 0, i))
    return pl.pallas_call(
        _peer_score_kernel,
        grid=(t // tm, PEER_HEADS),
        in_specs=[pl.BlockSpec((d, tm), lambda i, h: (0, i)),
                  pl.BlockSpec((2 * PEER_HALF, d), lambda i, h: (h, 0)),
                  pl.BlockSpec((1, 2, PEER_NKEYS, PEER_HALF), lambda i, h: (h, 0, 0, 0))],
        out_specs=[s_blk, s_blk, s_blk],
        out_shape=[s_shape, s_shape, jax.ShapeDtypeStruct(s_shape.shape, BF16)],
        scratch_shapes=[pltpu.VMEM((2, top_rows, tm), F32), pltpu.VMEM((top_rows, tm), F32),
                        pltpu.VMEM((2, PEER_NKEYS, tm), F32)],
        compiler_params=_params(("arbitrary", "arbitrary"), 40),
        name="peer_scores",
    )(h2t, wq_t, keys)


def _peer_w_kernel(h_ref, hs_ref, u_ref, g0_ref, phi_ref, e1_ref, w_ref, wmax_ref, st_scr, *, n_tiles):
    j = pl.program_id(1)
    tm, te = w_ref.shape
    rows = te // PEER_NKEYS

    @pl.when(j == 0)
    def _():
        st_scr[1] = jnp.zeros(st_scr.shape[1:], F32)
        wmax_ref[...] = jnp.zeros(wmax_ref.shape, F32)

    gate_tile = jnp.clip(j - 1, 0, n_tiles - 1)

    half = tm // 2

    def step(par):
        def body(a, carry):
            ts = pl.multiple_of(a * half, half)
            st_scr[par, :, pl.ds(ts, half)] = jnp.dot(u_ref[...], h_ref[:, pl.ds(ts, half)],
                                                      preferred_element_type=F32)
            inv_scale = hs_ref[:, pl.ds(ts, half)]
            for r in range(rows):
                n0 = gate_tile * rows + r
                ks = slice(r * PEER_NKEYS, (r + 1) * PEER_NKEYS)
                st = st_scr[1 - par, ks, pl.ds(ts, half)] * inv_scale
                act = 0.5 * st * (1.0 + lax.erf(st * math.sqrt(0.5)))
                gate = None
                for hd in range(PEER_HEADS):
                    e1 = e1_ref[hd, :, pl.ds(ts, half)]
                    phi = phi_ref[hd, pl.ds(n0, 1), pl.ds(ts, half)].astype(BF16)
                    g0 = g0_ref[hd, pl.ds(n0, 1), pl.ds(ts, half)].astype(BF16)
                    g = jnp.where(e1 >= phi, e1 * g0, jnp.zeros_like(e1))
                    gate = g if gate is None else gate + g
                wt = act * gate.astype(F32)
                w_ref[pl.ds(ts, half), ks] = wt.T.astype(BF16)
                wmax_ref[:, pl.ds(ts, half)] = jnp.maximum(wmax_ref[:, pl.ds(ts, half)],
                                                           jnp.max(jnp.abs(wt), axis=0, keepdims=True))
            return carry

        lax.fori_loop(0, 2, body, 0)

    @pl.when(j % 2 == 0)
    def _():
        step(0)

    @pl.when(j % 2 == 1)
    def _():
        step(1)


def _peer_weights(h2t, hs, u_q, g0, phi, e1):
    d, t = h2t.shape
    n_exp = u_q.shape[0]
    tm, te = 512, 1024
    n_tiles = n_exp // te
    once = pl.Buffered(1)
    sc_blk = pl.BlockSpec((PEER_HEADS, PEER_NKEYS, tm), lambda i, j: (0, 0, i), pipeline_mode=once)
    return pl.pallas_call(
        functools.partial(_peer_w_kernel, n_tiles=n_tiles),
        grid=(t // tm, n_tiles + 1),
        in_specs=[pl.BlockSpec((d, tm), lambda i, j: (0, i), pipeline_mode=once),
                  pl.BlockSpec((1, tm), lambda i, j: (0, i)),
                  pl.BlockSpec((te, d), lambda i, j: (jnp.minimum(j, n_tiles - 1), 0)),
                  sc_blk, sc_blk, sc_blk],
        out_specs=[pl.BlockSpec((tm, te), lambda i, j: (i, jnp.maximum(j - 1, 0))),
                   pl.BlockSpec((1, tm), lambda i, j: (0, i))],
        out_shape=[jax.ShapeDtypeStruct((t, n_exp), BF16), jax.ShapeDtypeStruct((1, t), F32)],
        scratch_shapes=[pltpu.VMEM((2, te, tm), F32)],
        compiler_params=_params(("arbitrary", "arbitrary"), 48),
        name="peer_w",
    )(h2t, hs, u_q, g0, phi, e1)


def _peer_v_kernel(w_ref, sc_ref, dq_ref, v_ref, o_ref, wq_scr, *, chunk):
    @pl.when(pl.program_id(1) == 0)
    def _():
        for c in range(w_ref.shape[1] // chunk):
            cols = slice(c * chunk, (c + 1) * chunk)
            wq_scr[:, cols] = (w_ref[:, cols].astype(F32) * sc_ref[...]).astype(FP8)

    o_ref[...] = jnp.dot(wq_scr[...], v_ref[...], preferred_element_type=F32) * dq_ref[...]


def _peer_values(w, wmax, v_q, v_inv):
    t, n_exp = w.shape
    d = v_q.shape[1]
    tm, tn = 512, min(512, d)
    wmax_col = jnp.maximum(wmax.reshape(t, 1), FP8_TINY)
    scale = FP8_TARGET / wmax_col
    dequant = wmax_col * (v_inv / FP8_TARGET)
    col = pl.BlockSpec((tm, 1), lambda i, j: (i, 0))
    return pl.pallas_call(
        functools.partial(_peer_v_kernel, chunk=min(2048, n_exp)),
        grid=(t // tm, d // tn),
        in_specs=[pl.BlockSpec((tm, n_exp), lambda i, j: (i, 0), pipeline_mode=pl.Buffered(1)),
                  col, col,
                  pl.BlockSpec((n_exp, tn), lambda i, j: (0, j))],
        out_specs=pl.BlockSpec((tm, tn), lambda i, j: (i, j)),
        out_shape=jax.ShapeDtypeStruct((t, d), F32),
        scratch_shapes=[pltpu.VMEM((tm, n_exp), FP8)],
        compiler_params=_params(("arbitrary", "arbitrary"), 48),
        name="peer_v",
    )(w, scale, dequant, v_q)


def _final_kernel(x_ref, p_ref, g2_ref, fg_ref, o_ref):
    x = x_ref[...] + g2_ref[0] * p_ref[...]
    ms = jnp.mean(x * x, axis=-1, keepdims=True)
    o_ref[...] = x * lax.rsqrt(ms + EPS) * fg_ref[...]


def _final(x1, peer, g2, final_g, seq):
    t, d = x1.shape
    tm = 256
    per_b = seq // tm
    row = pl.BlockSpec((tm, d), lambda i: (i, 0))
    return pl.pallas_call(
        _final_kernel,
        grid=(t // tm,),
        in_specs=[row, row,
                  pl.BlockSpec((1, 1, d), lambda i: (i // per_b, 0, 0)),
                  pl.BlockSpec((1, d), lambda i: (0, 0))],
        out_specs=row,
        out_shape=jax.ShapeDtypeStruct((t, d), F32),
        compiler_params=_params(("arbitrary",), 48),
        name="final",
    )(x1, peer, g2, final_g.reshape(1, d))


def _rope_tables(seq):
    rows = seq // GRID_W
    row_ids = jnp.repeat(jnp.arange(rows), GRID_W).astype(F32)
    col_ids = jnp.tile(jnp.arange(GRID_W), rows).astype(F32)
    axis_dim = HEAD_DIM // 2
    inv_freq = ROPE_THETA ** (-jnp.arange(0, axis_dim, 2, dtype=F32) / axis_dim)
    ang = jnp.concatenate([row_ids[:, None] * inv_freq, col_ids[:, None] * inv_freq], axis=-1)
    cos, sin = jnp.cos(ang), jnp.sin(ang)
    return jnp.concatenate([cos, cos], axis=-1), jnp.concatenate([-sin, sin], axis=-1)


def _trunk(x, mod, wts, dims):
    nb, seq, d = x.shape
    att_w, kv_w, f_w, gd = dims
    x2d = x.reshape(nb * seq, d)
    sh1, sc1, g1, sh2, sc2, g2 = [mod[:, k].reshape(nb, 1, d) for k in range(6)]
    cc, ss = _rope_tables(seq)

    h = _ln_mod(x2d, wts["norm1_g"], sc1, sh1, seq, transpose=False)
    proj = _inproj(h, wts["w_main"], wts["qg"], wts["kg"], cc, ss, seq, att_w, kv_w)
    p, q = _fproj(h, wts["w_f"], wts["cs_ch"], gd)
    att = lax.cond(wts["qk_fp8_ok"],
                   lambda pr: _attention(pr, nb, seq, att_w, kv_w, FP8),
                   lambda pr: _attention(pr, nb, seq, att_w, kv_w, BF16), proj)
    four = _seq_dft(p, q, nb, seq, gd)
    merged = _merge(att, four, wts["wa"], wts["wf"], proj, att_w + 2 * kv_w)
    x1 = _outproj(merged, wts["w_out"], x2d, g1, seq)

    h2t, h2t_q, h2_inv = _ln_mod(x1, wts["norm2_g"], sc2, sh2, seq, transpose=True)
    g0, phi, e1 = _peer_scores(h2t, wts["wq_t"], wts["keys"])
    w, wmax = _peer_weights(h2t_q, h2_inv * wts["u_inv"], wts["u_q"], g0, phi, e1)
    peer = _peer_values(w, wmax, wts["v_q"], wts["v_inv"])
    y = _final(x1, peer, g2, wts["final_g"], seq)
    return y.reshape(nb, seq, d)


def kernel(x_prompt, x_sample, c_prompt, c_sample, w_ada, b_ada, norm1_g, norm2_g, w_in, q_norm_g, k_norm_g, w_attn_br, w_four_br, w_out, w_peer_q, peer_keys, peer_u, peer_v, final_g):
    d = x_prompt.shape[-1]
    assert w_ada.shape[0] == 1, "single-layer trunk"
    att_w = w_attn_br.shape[1]
    f_w = w_four_br.shape[1]
    kv_w = (w_in.shape[2] - att_w - f_w - 2 * d) // 2
    gd = f_w // N_FGROUPS
    dims = (att_w, kv_w, f_w, gd)

    perm = np.concatenate([np.arange(0, HEAD_DIM, 2), np.arange(1, HEAD_DIM, 2)])
    w = w_in[0]

    def deinterleave(cols):
        return cols.reshape(d, -1, HEAD_DIM)[:, :, perm].reshape(d, -1)

    s_q, s_k, s_v, s_f = att_w, att_w + kv_w, att_w + 2 * kv_w, att_w + 2 * kv_w + f_w
    w_main = jnp.concatenate([deinterleave(w[:, :s_q]), deinterleave(w[:, s_q:s_k]), w[:, s_k:s_v],
                              w[:, s_f:]], axis=1).astype(BF16)
    cch, sch = _dft_tables(gd)
    q_bound = jnp.maximum(jnp.max(jnp.abs(q_norm_g[0])) * LOG2E, FP8_TINY)
    k_bound = jnp.maximum(jnp.max(jnp.abs(k_norm_g[0])) * math.sqrt(HEAD_DIM), FP8_TINY)
    qk_fp8_ok = jnp.sqrt(2.0 * q_bound * k_bound) < FP8_TARGET
    qk_bal = jnp.where(qk_fp8_ok, jnp.exp2(jnp.round(0.5 * jnp.log2(k_bound / q_bound))), 1.0)
    u_max = jnp.maximum(jnp.max(jnp.abs(peer_u[0])), FP8_TINY)
    v_max = jnp.maximum(jnp.max(jnp.abs(peer_v[0])), FP8_TINY)
    wts = {
        "norm1_g": norm1_g[0], "norm2_g": norm2_g[0], "final_g": final_g,
        "w_main": w_main,
        "w_f": w[:, s_v:s_f].astype(BF16),
        "cs_ch": jnp.asarray(np.concatenate([cch, sch], axis=1), BF16),
        "qg": (q_norm_g[0][perm] * (HEAD_DIM ** -0.5 * LOG2E * qk_bal)).reshape(1, HEAD_DIM),
        "kg": (k_norm_g[0][perm] / qk_bal).reshape(1, HEAD_DIM),
        "qk_fp8_ok": qk_fp8_ok,
        "wa": w_attn_br[0].astype(BF16), "wf": w_four_br[0].astype(BF16),
        "w_out": w_out[0].astype(BF16),
        "wq_t": w_peer_q[0].T.astype(BF16),
        "keys": peer_keys[0].astype(BF16),
        "u_q": (peer_u[0] * (FP8_TARGET / u_max)).astype(FP8), "u_inv": u_max * (1.0 / FP8_TARGET),
        "v_q": (peer_v[0] * (FP8_TARGET / v_max)).astype(FP8), "v_inv": v_max * (1.0 / FP8_TARGET),
    }

    nbp, nbs = c_prompt.shape[0], c_sample.shape[0]
    c_all = jnp.concatenate([c_prompt, c_sample], axis=0)
    c_pad = jnp.pad(c_all, ((0, (-c_all.shape[0]) % 8), (0, 0)))
    mod = _modulation(c_pad, w_ada[0], b_ada[0]).reshape(c_pad.shape[0], 6, d)

    y_prompt = _trunk(x_prompt, mod[:nbp], wts, dims)
    y_sample = _trunk(x_sample, mod[nbp:nbp + nbs], wts, dims)
    return (y_prompt, y_sample)
```

```python
import functools
import math

import jax
import jax.numpy as jnp
import numpy as np
from jax import lax
from jax.experimental import pallas as pl
from jax.experimental.pallas import tpu as pltpu

F32 = jnp.float32
BF16 = jnp.bfloat16
FP8 = jnp.float8_e4m3fn
FP8_TARGET = 256.0
FP8_TINY = 1e-30

HEAD_DIM = 128
GQA_GROUP = 4
Q_PER_TILE = GQA_GROUP * HEAD_DIM
ROPE_THETA = 10000.0
GRID_W = 64
N_FGROUPS = 8
PEER_HEADS = 8
PEER_NKEYS = 128
PEER_HALF = 128
PEER_TOPK = 16
SCORE_COLS = 128
EPS = 1e-6
LANES = 128
NEG_INF = float("-inf")
ONES_ROWS = 16
LOG2E = math.log2(math.e)
MIB = 1024 * 1024


def _params(sem, vmem_mib):
    return pltpu.CompilerParams(dimension_semantics=sem, vmem_limit_bytes=vmem_mib * MIB)


def _mod_kernel(c_ref, w_ref, b_ref, o_ref):
    c = c_ref[...]
    a = c * jax.nn.sigmoid(c)
    o_ref[...] = jnp.dot(a, w_ref[...], preferred_element_type=F32,
                         precision=lax.Precision.HIGHEST) + b_ref[...]


def _modulation(c_pad, w_ada, b_ada):
    rows, d = c_pad.shape
    n = w_ada.shape[1]
    tn = 512
    return pl.pallas_call(
        _mod_kernel,
        grid=(n // tn,),
        in_specs=[pl.BlockSpec((rows, d), lambda j: (0, 0)),
                  pl.BlockSpec((d, tn), lambda j: (0, j)),
                  pl.BlockSpec((1, tn), lambda j: (0, j))],
        out_specs=pl.BlockSpec((rows, tn), lambda j: (0, j)),
        out_shape=jax.ShapeDtypeStruct((rows, n), F32),
        compiler_params=_params(("arbitrary",), 40),
        name="mod",
    )(c_pad, w_ada, b_ada.reshape(1, n))


def _ln_rows(x_ref, g_ref, sc_ref, sh_ref):
    x = x_ref[...]
    ms = jnp.mean(x * x, axis=-1, keepdims=True)
    y = x * lax.rsqrt(ms + EPS) * g_ref[...]
    return y * (1.0 + sc_ref[0]) + sh_ref[0]


def _ln_kernel(x_ref, g_ref, sc_ref, sh_ref, o_ref):
    o_ref[...] = _ln_rows(x_ref, g_ref, sc_ref, sh_ref).astype(BF16)


def _ln_t_kernel(x_ref, g_ref, sc_ref, sh_ref, o_ref, q_ref, s_ref, ht_scr, *, chunk):
    h = _ln_rows(x_ref, g_ref, sc_ref, sh_ref)
    amax = jnp.full(s_ref.shape, FP8_TINY, F32)
    for c in range(h.shape[1] // chunk):
        rows = slice(c * chunk, (c + 1) * chunk)
        blk = h[:, rows].T
        ht_scr[rows, :] = blk
        o_ref[rows, :] = blk.astype(BF16)
        amax = jnp.maximum(amax, jnp.max(jnp.abs(blk), axis=0, keepdims=True))
    q_ref[...] = (ht_scr[...] * (FP8_TARGET / amax)).astype(FP8)
    s_ref[...] = amax * (1.0 / FP8_TARGET)


def _ln_mod(x2d, g, sc, sh, seq, *, transpose):
    t, d = x2d.shape
    tm = 256
    per_b = seq // tm
    in_specs = [pl.BlockSpec((tm, d), lambda i: (i, 0)),
                pl.BlockSpec((1, d), lambda i: (0, 0)),
                pl.BlockSpec((1, 1, d), lambda i: (i // per_b, 0, 0)),
                pl.BlockSpec((1, 1, d), lambda i: (i // per_b, 0, 0))]
    args = (x2d, g.reshape(1, d), sc, sh)
    if not transpose:
        return pl.pallas_call(
            _ln_kernel, grid=(t // tm,), in_specs=in_specs,
            out_specs=pl.BlockSpec((tm, d), lambda i: (i, 0)),
            out_shape=jax.ShapeDtypeStruct((t, d), BF16),
            compiler_params=_params(("arbitrary",), 40), name="ln",
        )(*args)
    col_blk = pl.BlockSpec((d, tm), lambda i: (0, i))
    return pl.pallas_call(
        functools.partial(_ln_t_kernel, chunk=min(512, d)),
        grid=(t // tm,), in_specs=in_specs,
        out_specs=[col_blk, col_blk, pl.BlockSpec((1, tm), lambda i: (0, i))],
        out_shape=[jax.ShapeDtypeStruct((d, t), BF16), jax.ShapeDtypeStruct((d, t), FP8),
                   jax.ShapeDtypeStruct((1, t), F32)],
        scratch_shapes=[pltpu.VMEM((d, tm), F32)],
        compiler_params=_params(("arbitrary",), 40), name="ln_t",
    )(*args)


def _norm_rope(acc, g, cc, ss, o_ref):
    for hd in range(acc.shape[1] // HEAD_DIM):
        xh = acc[:, hd * HEAD_DIM:(hd + 1) * HEAD_DIM]
        ms = jnp.mean(xh * xh, axis=-1, keepdims=True)
        y = xh * lax.rsqrt(ms + EPS) * g
        y = y * cc + pltpu.roll(y, HEAD_DIM // 2, axis=1) * ss
        o_ref[:, hd * HEAD_DIM:(hd + 1) * HEAD_DIM] = y.astype(BF16)


def _inproj_kernel(h_ref, w_ref, qg_ref, kg_ref, cc_ref, ss_ref, o_ref, *, nq, nk, nv):
    j = pl.program_id(1)
    acc = jnp.dot(h_ref[...], w_ref[...], preferred_element_type=F32)

    @pl.when(j < nq)
    def _():
        _norm_rope(acc, qg_ref[...], cc_ref[...], ss_ref[...], o_ref)

    @pl.when(jnp.logical_and(j >= nq, j < nq + nk))
    def _():
        _norm_rope(acc, kg_ref[...], cc_ref[...], ss_ref[...], o_ref)

    @pl.when(jnp.logical_and(j >= nq + nk, j < nq + nk + nv))
    def _():
        o_ref[...] = acc.astype(BF16)

    @pl.when(j >= nq + nk + nv)
    def _():
        o_ref[...] = (0.5 * jnp.tanh(0.5 * acc) + 0.5).astype(BF16)


def _inproj(h, w_main, qg, kg, cc, ss, seq, att_w, kv_w):
    t, d = h.shape
    wtot = w_main.shape[1]
    tm = 1024
    tn = min(1024, kv_w)
    per_b = seq // tm
    kern = functools.partial(_inproj_kernel, nq=att_w // tn, nk=kv_w // tn, nv=kv_w // tn)
    return pl.pallas_call(
        kern,
        grid=(t // tm, wtot // tn),
        in_specs=[pl.BlockSpec((tm, d), lambda i, j: (i, 0)),
                  pl.BlockSpec((d, tn), lambda i, j: (0, j)),
                  pl.BlockSpec((1, HEAD_DIM), lambda i, j: (0, 0)),
                  pl.BlockSpec((1, HEAD_DIM), lambda i, j: (0, 0)),
                  pl.BlockSpec((tm, HEAD_DIM), lambda i, j: (i % per_b, 0)),
                  pl.BlockSpec((tm, HEAD_DIM), lambda i, j: (i % per_b, 0))],
        out_specs=pl.BlockSpec((tm, tn), lambda i, j: (i, j)),
        out_shape=jax.ShapeDtypeStruct((t, wtot), BF16),
        compiler_params=_params(("arbitrary", "arbitrary"), 48),
        name="inproj",
    )(h, w_main, qg, kg, cc, ss)


def _fproj_kernel(h_ref, w_ref, cs_ref, p_ref, q_ref, *, gd):
    acc = jnp.dot(h_ref[...], w_ref[...], preferred_element_type=F32)
    for g in range(acc.shape[1] // gd):
        u = acc[:, g * gd:(g + 1) * gd].astype(BF16)
        pq = jnp.dot(u, cs_ref[...], preferred_element_type=F32)
        p_ref[:, g * gd:(g + 1) * gd] = pq[:, :gd].astype(BF16)
        q_ref[:, g * gd:(g + 1) * gd] = pq[:, gd:].astype(BF16)


def _fproj(h, w_f, cs, gd):
    t, d = h.shape
    fw = w_f.shape[1]
    tm = 512
    tn = min(1024, fw)
    out = jax.ShapeDtypeStruct((t, fw), BF16)
    return pl.pallas_call(
        functools.partial(_fproj_kernel, gd=gd),
        grid=(t // tm, fw // tn),
        in_specs=[pl.BlockSpec((tm, d), lambda i, j: (i, 0)),
                  pl.BlockSpec((d, tn), lambda i, j: (0, j)),
                  pl.BlockSpec((gd, 2 * gd), lambda i, j: (0, 0))],
        out_specs=[pl.BlockSpec((tm, tn), lambda i, j: (i, j)),
                   pl.BlockSpec((tm, tn), lambda i, j: (i, j))],
        out_shape=[out, out],
        compiler_params=_params(("arbitrary", "arbitrary"), 48),
        name="fproj",
    )(h, w_f, cs)


def _attn_kernel(q_ref, k_ref, v_ref, o_ref, m_scr, acc_scr, qt_scr, sa_scr, sb_scr, *, tk):
    tq = q_ref.shape[0]
    seq = k_ref.shape[0]
    n_chunks = seq // tk
    mm_dtype = qt_scr.dtype
    for g in range(GQA_GROUP):
        qt_scr[:, g * tq:(g + 1) * tq] = q_ref[:, g * HEAD_DIM:(g + 1) * HEAD_DIM].astype(F32).T.astype(mm_dtype)
    m_scr[...] = jnp.full(m_scr.shape, NEG_INF, F32)
    acc_scr[...] = jnp.zeros(acc_scr.shape, F32)
    ones = jnp.ones((ONES_ROWS, tk), BF16)

    def scores(c, st_ref):
        start = pl.multiple_of(c * tk, tk)
        st_ref[...] = jnp.dot(k_ref[pl.ds(start, tk), :].astype(mm_dtype), qt_scr[...],
                              preferred_element_type=F32)

    def consume(c, st_ref):
        start = pl.multiple_of(c * tk, tk)
        v = v_ref[pl.ds(start, tk), :]
        st = st_ref[...]
        m_prev = m_scr[...]
        m_next = jnp.maximum(m_prev, jnp.max(st, axis=0, keepdims=True))
        p = jnp.exp2(st - m_next).astype(BF16)
        alpha = jnp.exp2(m_prev - m_next)
        vt = jnp.concatenate([v.astype(F32).T.astype(BF16), ones], axis=0)
        acc_scr[...] = alpha * acc_scr[...] + jnp.dot(vt, p, preferred_element_type=F32)
        m_scr[...] = m_next

    scores(0, sa_scr)

    def body(i, carry):
        c = 2 * i
        scores(c + 1, sb_scr)
        consume(c, sa_scr)
        scores(c + 2, sa_scr)
        consume(c + 1, sb_scr)
        return carry

    lax.fori_loop(0, n_chunks // 2 - 1, body, 0)
    scores(n_chunks - 1, sb_scr)
    consume(n_chunks - 2, sa_scr)
    consume(n_chunks - 1, sb_scr)
    acc = acc_scr[...]
    out_t = acc[:HEAD_DIM] / acc[HEAD_DIM:HEAD_DIM + 1]
    for g in range(GQA_GROUP):
        o_ref[:, g * HEAD_DIM:(g + 1) * HEAD_DIM] = out_t[:, g * tq:(g + 1) * tq].T.astype(BF16)


def _attention(proj, nb, seq, att_w, kv_w, mm_dtype):
    t = proj.shape[0]
    tq = 512
    tk = min(1024, seq // 2)
    assert seq % (2 * tk) == 0
    n_kv = kv_w // HEAD_DIM
    per_b = seq // tq
    k_off = att_w // HEAD_DIM
    v_off = (att_w + kv_w) // HEAD_DIM
    cols = GQA_GROUP * tq
    return pl.pallas_call(
        functools.partial(_attn_kernel, tk=tk),
        grid=(nb, n_kv, per_b),
        in_specs=[pl.BlockSpec((tq, Q_PER_TILE), lambda b, g, i: (b * per_b + i, g)),
                  pl.BlockSpec((seq, HEAD_DIM), lambda b, g, i: (b, k_off + g)),
                  pl.BlockSpec((seq, HEAD_DIM), lambda b, g, i: (b, v_off + g))],
        out_specs=pl.BlockSpec((tq, Q_PER_TILE), lambda b, g, i: (b * per_b + i, g)),
        out_shape=jax.ShapeDtypeStruct((t, att_w), BF16),
        scratch_shapes=[pltpu.VMEM((1, cols), F32),
                        pltpu.VMEM((HEAD_DIM + ONES_ROWS, cols), F32),
                        pltpu.VMEM((HEAD_DIM, cols), mm_dtype),
                        pltpu.VMEM((tk, cols), F32),
                        pltpu.VMEM((tk, cols), F32)],
        compiler_params=_params(("arbitrary", "arbitrary", "arbitrary"), 48),
        name="attn",
    )(proj, proj, proj)


def _fft_a_kernel(m_ref, p_ref, q_ref, twc_ref, tws_ref, ar_ref, ai_ref, *, fw):
    n1 = p_ref.shape[1]
    rhs = jnp.concatenate([p_ref[0], q_ref[0]], axis=0)
    a = jnp.dot(m_ref[...], rhs, preferred_element_type=F32)
    rep = fw // LANES
    for u in range(p_ref.shape[2] // fw):
        ar = a[:n1, u * fw:(u + 1) * fw]
        ai = a[n1:, u * fw:(u + 1) * fw]
        c = jnp.tile(twc_ref[u], (1, rep))
        s = jnp.tile(tws_ref[u], (1, rep))
        ar_ref[0, :, u * fw:(u + 1) * fw] = (ar * c + ai * s).astype(BF16)
        ai_ref[0, :, u * fw:(u + 1) * fw] = (ai * c - ar * s).astype(BF16)


def _fft_c_kernel(cs_ref, ar_ref, ai_ref, o_ref, *, scale):
    rhs = jnp.concatenate([ar_ref[0, 0], ai_ref[0, 0]], axis=0)
    o_ref[0] = (jnp.dot(cs_ref[...], rhs, preferred_element_type=F32) * scale).astype(BF16)


def _dft_tables(n):
    idx = (np.arange(n)[:, None] * np.arange(n)[None, :]) % n
    ang = 2.0 * np.pi * idx / n
    return np.cos(ang), np.sin(ang)


def _seq_dft(p, q, nb, seq, gd):
    fw = p.shape[1]
    n1 = 64 if seq >= 4096 else 16
    n2 = seq // n1
    c1, s1 = _dft_tables(n1)
    c2, s2 = _dft_tables(n2)
    m_a = jnp.asarray(np.block([[c1, -s1], [-s1, -c1]]), BF16)
    cs_c = jnp.asarray(np.concatenate([c2, s2], axis=1), BF16)
    tw = 2.0 * np.pi * (np.arange(n2)[:, None] * np.arange(n1)[None, :]) / seq
    twc = jnp.asarray(np.broadcast_to(np.cos(tw)[:, :, None], (n2, n1, LANES)), F32)
    tws = jnp.asarray(np.broadcast_to(np.sin(tw)[:, :, None], (n2, n1, LANES)), F32)

    nt2 = 4
    tc = nt2 * fw
    p3 = p.reshape(nb, n1, n2 * fw)
    q3 = q.reshape(nb, n1, n2 * fw)
    a_shape = jax.ShapeDtypeStruct((nb, n1, n2 * fw), BF16)
    blk = pl.BlockSpec((1, n1, tc), lambda b, j: (b, 0, j))
    tw_blk = pl.BlockSpec((nt2, n1, LANES), lambda b, j: (j, 0, 0))
    ar, ai = pl.pallas_call(
        functools.partial(_fft_a_kernel, fw=fw),
        grid=(nb, n2 // nt2),
        in_specs=[pl.BlockSpec((2 * n1, 2 * n1), lambda b, j: (0, 0)), blk, blk, tw_blk, tw_blk],
        out_specs=[blk, blk],
        out_shape=[a_shape, a_shape],
        compiler_params=_params(("arbitrary", "arbitrary"), 40),
        name="fft_a",
    )(m_a, p3, q3, twc, tws)

    in_blk = pl.BlockSpec((1, 1, n2, fw), lambda b, k: (b, k, 0, 0))
    out = pl.pallas_call(
        functools.partial(_fft_c_kernel, scale=1.0 / math.sqrt(seq * gd)),
        grid=(nb, n1),
        in_specs=[pl.BlockSpec((n2, 2 * n2), lambda b, k: (0, 0)), in_blk, in_blk],
        out_specs=pl.BlockSpec((1, n2, fw), lambda b, k: (b, 0, k)),
        out_shape=jax.ShapeDtypeStruct((nb, n2, n1 * fw), BF16),
        compiler_params=_params(("arbitrary", "arbitrary"), 40),
        name="fft_c",
    )(cs_c, ar.reshape(nb, n1, n2, fw), ai.reshape(nb, n1, n2, fw))
    return out.reshape(nb * seq, fw)


def _merge_kernel(a_ref, f_ref, wa_ref, wf_ref, ga_ref, gf_ref, o_ref):
    a_br = jnp.dot(a_ref[...], wa_ref[...], preferred_element_type=F32)
    f_br = jnp.dot(f_ref[...], wf_ref[...], preferred_element_type=F32)
    o_ref[...] = (ga_ref[...].astype(F32) * a_br + gf_ref[...].astype(F32) * f_br).astype(BF16)


def _merge(att, four, wa, wf, proj, gate_off):
    t, aw = att.shape
    fw = four.shape[1]
    d = wa.shape[1]
    tm, tn = 1024, 512
    ga0 = gate_off // tn
    gf0 = (gate_off + d) // tn
    return pl.pallas_call(
        _merge_kernel,
        grid=(t // tm, d // tn),
        in_specs=[pl.BlockSpec((tm, aw), lambda i, j: (i, 0)),
                  pl.BlockSpec((tm, fw), lambda i, j: (i, 0)),
                  pl.BlockSpec((aw, tn), lambda i, j: (0, j)),
                  pl.BlockSpec((fw, tn), lambda i, j: (0, j)),
                  pl.BlockSpec((tm, tn), lambda i, j: (i, ga0 + j)),
                  pl.BlockSpec((tm, tn), lambda i, j: (i, gf0 + j))],
        out_specs=pl.BlockSpec((tm, tn), lambda i, j: (i, j)),
        out_shape=jax.ShapeDtypeStruct((t, d), BF16),
        compiler_params=_params(("arbitrary", "arbitrary"), 48),
        name="merge",
    )(att, four, wa, wf, proj, proj)


def _outproj_kernel(m_ref, w_ref, x_ref, g_ref, o_ref):
    acc = jnp.dot(m_ref[...], w_ref[...], preferred_element_type=F32)
    o_ref[...] = x_ref[...] + g_ref[0] * acc


def _outproj(merged, w_out, x2d, g1, seq):
    t, d = merged.shape
    tm, tn = 1024, 512
    per_b = seq // tm
    return pl.pallas_call(
        _outproj_kernel,
        grid=(t // tm, d // tn),
        in_specs=[pl.BlockSpec((tm, d), lambda i, j: (i, 0)),
                  pl.BlockSpec((d, tn), lambda i, j: (0, j)),
                  pl.BlockSpec((tm, tn), lambda i, j: (i, j)),
                  pl.BlockSpec((1, 1, tn), lambda i, j: (i // per_b, 0, j))],
        out_specs=pl.BlockSpec((tm, tn), lambda i, j: (i, j)),
        out_shape=jax.ShapeDtypeStruct((t, d), F32),
        compiler_params=_params(("arbitrary", "arbitrary"), 48),
        name="outproj",
    )(merged, w_out, x2d, g1)


def _top_values(s, k, store):
    n = s.shape[0]
    iota = lax.broadcasted_iota(jnp.int32, s.shape, 0)
    cur = s
    for r in range(k):
        m = jnp.max(cur, axis=0, keepdims=True)
        store(r, m)
        first = jnp.min(jnp.where(cur == m, iota, n), axis=0, keepdims=True)
        cur = jnp.where(iota == first, NEG_INF, cur)


def _top_values_distinct(s, k, store):
    cur = s
    for r in range(k):
        m = jnp.max(cur, axis=0, keepdims=True)
        store(r, m)
        cur = jnp.where(cur == m, NEG_INF, cur)
    return jnp.sum(jnp.where(cur == NEG_INF, 1.0, 0.0), axis=0, keepdims=True)


def _peer_score_kernel(h_ref, wq_ref, keys_ref, g0_ref, phi_ref, e1_ref, top_scr, best_scr, s_scr):
    k1 = PEER_TOPK + 1
    qt = jnp.dot(wq_ref[...], h_ref[...], preferred_element_type=F32).astype(BF16)
    top_scr[...] = jnp.full(top_scr.shape, NEG_INF, F32)
    pad_rows = top_scr.shape[1] - k1
    s = [jnp.dot(keys_ref[0, p], qt[p * PEER_HALF:(p + 1) * PEER_HALF], preferred_element_type=F32)
         for p in range(2)]
    for p in range(2):
        s_scr[p] = s[p]

    def extract(top_fn):
        checks = []
        for c in range(s_scr.shape[2] // SCORE_COLS):
            cols = slice(c * SCORE_COLS, (c + 1) * SCORE_COLS)
            for p in range(2):
                def store(r, row, p=p):
                    top_scr[p, r:r + 1, cols] = row
                checks.append((top_fn(s_scr[p, :, cols], k1, store), k1))
            cand = jnp.concatenate(
                [top_scr[0, 0:1, cols] + top_scr[1, :, cols]]
                + [top_scr[0, i:i + 1, cols] + top_scr[1, 0:8, cols] for i in range(1, k1)], axis=0)

            def store_best(r, row):
                best_scr[r:r + 1, cols] = row
            checks.append((top_fn(cand, k1, store_best), k1 + pad_rows))
        return checks

    suspect = None
    for removed, expected in extract(_top_values_distinct):
        bad = jnp.where(removed != expected, 1.0, 0.0)
        suspect = bad if suspect is None else jnp.maximum(suspect, bad)

    @pl.when(jnp.max(suspect) > 0.0)
    def _():
        extract(_top_values)

    best = best_scr[0:PEER_TOPK]
    z = jnp.sum(jnp.exp(best - best[0:1]), axis=0, keepdims=True)
    tau = 0.5 * (best_scr[PEER_TOPK - 1:PEER_TOPK] + best_scr[PEER_TOPK:k1])
    max0 = top_scr[0, 0:1, :]
    max1 = top_scr[1, 0:1, :]
    g0_ref[0] = jnp.exp(s[0] - max0) / z
    phi_ref[0] = jnp.exp((tau - max1) - s[0])
    e1_ref[0] = jnp.exp(s[1] - max1).astype(BF16)


def _peer_scores(h2t, wq_t, keys):
    d, t = h2t.shape
    tm = 512
    top_rows = 24
    s_shape = jax.ShapeDtypeStruct((PEER_HEADS, PEER_NKEYS, t), F32)
    s_blk = pl.BlockSpec((1, PEER_NKEYS, tm), lambda i, h: (h, 0, i))
    return pl.pallas_call(
        _peer_score_kernel,
        grid=(t // tm, PEER_HEADS),
        in_specs=[pl.BlockSpec((d, tm), lambda i, h: (0, i)),
                  pl.BlockSpec((2 * PEER_HALF, d), lambda i, h: (h, 0)),
                  pl.BlockSpec((1, 2, PEER_NKEYS, PEER_HALF), lambda i, h: (h, 0, 0, 0))],
        out_specs=[s_blk, s_blk, s_blk],
        out_shape=[s_shape, s_shape, jax.ShapeDtypeStruct(s_shape.shape, BF16)],
        scratch_shapes=[pltpu.VMEM((2, top_rows, tm), F32), pltpu.VMEM((top_rows, tm), F32),
                        pltpu.VMEM((2, PEER_NKEYS, tm), F32)],
        compiler_params=_params(("arbitrary", "arbitrary"), 40),
        name="peer_scores",
    )(h2t, wq_t, keys)


def _peer_w_kernel(h_ref, hs_ref, u_ref, g0_ref, phi_ref, e1_ref, w_ref, wmax_ref, st_scr, *, n_tiles):
    j = pl.program_id(1)
    tm, te = w_ref.shape
    rows = te // PEER_NKEYS

    @pl.when(j == 0)
    def _():
        st_scr[1] = jnp.zeros(st_scr.shape[1:], F32)
        wmax_ref[...] = jnp.zeros(wmax_ref.shape, F32)

    gate_tile = jnp.clip(j - 1, 0, n_tiles - 1)

    half = tm // 2

    def step(par):
        def body(a, carry):
            ts = pl.multiple_of(a * half, half)
            st_scr[par, :, pl.ds(ts, half)] = jnp.dot(u_ref[...], h_ref[:, pl.ds(ts, half)],
                                                      preferred_element_type=F32)
            inv_scale = hs_ref[:, pl.ds(ts, half)]
            for r in range(rows):
                n0 = gate_tile * rows + r
                ks = slice(r * PEER_NKEYS, (r + 1) * PEER_NKEYS)
                st = st_scr[1 - par, ks, pl.ds(ts, half)] * inv_scale
                act = 0.5 * st * (1.0 + lax.erf(st * math.sqrt(0.5)))
                gate = None
                for hd in range(PEER_HEADS):
                    e1 = e1_ref[hd, :, pl.ds(ts, half)]
                    phi = phi_ref[hd, pl.ds(n0, 1), pl.ds(ts, half)].astype(BF16)
                    g0 = g0_ref[hd, pl.ds(n0, 1), pl.ds(ts, half)].astype(BF16)
                    g = jnp.where(e1 >= phi, e1 * g0, jnp.zeros_like(e1))
                    gate = g if gate is None else gate + g
                wt = act * gate.astype(F32)
                w_ref[pl.ds(ts, half), ks] = wt.T.astype(BF16)
                wmax_ref[:, pl.ds(ts, half)] = jnp.maximum(wmax_ref[:, pl.ds(ts, half)],
                                                           jnp.max(jnp.abs(wt), axis=0, keepdims=True))
            return carry

        lax.fori_loop(0, 2, body, 0)

    @pl.when(j % 2 == 0)
    def _():
        step(0)

    @pl.when(j % 2 == 1)
    def _():
        step(1)


def _peer_weights(h2t, hs, u_q, g0, phi, e1):
    d, t = h2t.shape
    n_exp = u_q.shape[0]
    tm, te = 512, 1024
    n_tiles = n_exp // te
    once = pl.Buffered(1)
    sc_blk = pl.BlockSpec((PEER_HEADS, PEER_NKEYS, tm), lambda i, j: (0, 0, i), pipeline_mode=once)
    return pl.pallas_call(
        functools.partial(_peer_w_kernel, n_tiles=n_tiles),
        grid=(t // tm, n_tiles + 1),
        in_specs=[pl.BlockSpec((d, tm), lambda i, j: (0, i), pipeline_mode=once),
                  pl.BlockSpec((1, tm), lambda i, j: (0, i)),
                  pl.BlockSpec((te, d), lambda i, j: (jnp.minimum(j, n_tiles - 1), 0)),
                  sc_blk, sc_blk, sc_blk],
        out_specs=[pl.BlockSpec((tm, te), lambda i, j: (i, jnp.maximum(j - 1, 0))),
                   pl.BlockSpec((1, tm), lambda i, j: (0, i))],
        out_shape=[jax.ShapeDtypeStruct((t, n_exp), BF16), jax.ShapeDtypeStruct((1, t), F32)],
        scratch_shapes=[pltpu.VMEM((2, te, tm), F32)],
        compiler_params=_params(("arbitrary", "arbitrary"), 48),
        name="peer_w",
    )(h2t, hs, u_q, g0, phi, e1)


def _peer_v_kernel(w_ref, sc_ref, dq_ref, v_ref, o_ref, wq_scr, *, chunk):
    @pl.when(pl.program_id(1) == 0)
    def _():
        for c in range(w_ref.shape[1] // chunk):
            cols = slice(c * chunk, (c + 1) * chunk)
            wq_scr[:, cols] = (w_ref[:, cols].astype(F32) * sc_ref[...]).astype(FP8)

    o_ref[...] = jnp.dot(wq_scr[...], v_ref[...], preferred_element_type=F32) * dq_ref[...]


def _peer_values(w, wmax, v_q, v_inv):
    t, n_exp = w.shape
    d = v_q.shape[1]
    tm, tn = 512, min(512, d)
    wmax_col = jnp.maximum(wmax.reshape(t, 1), FP8_TINY)
    scale = FP8_TARGET / wmax_col
    dequant = wmax_col * (v_inv / FP8_TARGET)
    col = pl.BlockSpec((tm, 1), lambda i, j: (i, 0))
    return pl.pallas_call(
        functools.partial(_peer_v_kernel, chunk=min(2048, n_exp)),
        grid=(t // tm, d // tn),
        in_specs=[pl.BlockSpec((tm, n_exp), lambda i, j: (i, 0), pipeline_mode=pl.Buffered(1)),
                  col, col,
                  pl.BlockSpec((n_exp, tn), lambda i, j: (0, j))],
        out_specs=pl.BlockSpec((tm, tn), lambda i, j: (i, j)),
        out_shape=jax.ShapeDtypeStruct((t, d), F32),
        scratch_shapes=[pltpu.VMEM((tm, n_exp), FP8)],
        compiler_params=_params(("arbitrary", "arbitrary"), 48),
        name="peer_v",
    )(w, scale, dequant, v_q)


def _final_kernel(x_ref, p_ref, g2_ref, fg_ref, o_ref):
    x = x_ref[...] + g2_ref[0] * p_ref[...]
    ms = jnp.mean(x * x, axis=-1, keepdims=True)
    o_ref[...] = x * lax.rsqrt(ms + EPS) * fg_ref[...]


def _final(x1, peer, g2, final_g, seq):
    t, d = x1.shape
    tm = 256
    per_b = seq // tm
    row = pl.BlockSpec((tm, d), lambda i: (i, 0))
    return pl.pallas_call(
        _final_kernel,
        grid=(t // tm,),
        in_specs=[row, row,
                  pl.BlockSpec((1, 1, d), lambda i: (i // per_b, 0, 0)),
                  pl.BlockSpec((1, d), lambda i: (0, 0))],
        out_specs=row,
        out_shape=jax.ShapeDtypeStruct((t, d), F32),
        compiler_params=_params(("arbitrary",), 48),
        name="final",
    )(x1, peer, g2, final_g.reshape(1, d))


def _rope_tables(seq):
    rows = seq // GRID_W
    row_ids = jnp.repeat(jnp.arange(rows), GRID_W).astype(F32)
    col_ids = jnp.tile(jnp.arange(GRID_W), rows).astype(F32)
    axis_dim = HEAD_DIM // 2
    inv_freq = ROPE_THETA ** (-jnp.arange(0, axis_dim, 2, dtype=F32) / axis_dim)
    ang = jnp.concatenate([row_ids[:, None] * inv_freq, col_ids[:, None] * inv_freq], axis=-1)
    cos, sin = jnp.cos(ang), jnp.sin(ang)
    return jnp.concatenate([cos, cos], axis=-1), jnp.concatenate([-sin, sin], axis=-1)


def _trunk(x, mod, wts, dims):
    nb, seq, d = x.shape
    att_w, kv_w, f_w, gd = dims
    x2d = x.reshape(nb * seq, d)
    sh1, sc1, g1, sh2, sc2, g2 = [mod[:, k].reshape(nb, 1, d) for k in range(6)]
    cc, ss = _rope_tables(seq)

    h = _ln_mod(x2d, wts["norm1_g"], sc1, sh1, seq, transpose=False)
    proj = _inproj(h, wts["w_main"], wts["qg"], wts["kg"], cc, ss, seq, att_w, kv_w)
    p, q = _fproj(h, wts["w_f"], wts["cs_ch"], gd)
    att = lax.cond(wts["qk_fp8_ok"],
                   lambda pr: _attention(pr, nb, seq, att_w, kv_w, FP8),
                   lambda pr: _attention(pr, nb, seq, att_w, kv_w, BF16), proj)
    four = _seq_dft(p, q, nb, seq, gd)
    merged = _merge(att, four, wts["wa"], wts["wf"], proj, att_w + 2 * kv_w)
    x1 = _outproj(merged, wts["w_out"], x2d, g1, seq)

    h2t, h2t_q, h2_inv = _ln_mod(x1, wts["norm2_g"], sc2, sh2, seq, transpose=True)
    g0, phi, e1 = _peer_scores(h2t, wts["wq_t"], wts["keys"])
    w, wmax = _peer_weights(h2t_q, h2_inv * wts["u_inv"], wts["u_q"], g0, phi, e1)
    peer = _peer_values(w, wmax, wts["v_q"], wts["v_inv"])
    y = _final(x1, peer, g2, wts["final_g"], seq)
    return y.reshape(nb, seq, d)


def kernel(x_prompt, x_sample, c_prompt, c_sample, w_ada, b_ada, norm1_g, norm2_g, w_in, q_norm_g, k_norm_g, w_attn_br, w_four_br, w_out, w_peer_q, peer_keys, peer_u, peer_v, final_g):
    d = x_prompt.shape[-1]
    assert w_ada.shape[0] == 1, "single-layer trunk"
    att_w = w_attn_br.shape[1]
    f_w = w_four_br.shape[1]
    kv_w = (w_in.shape[2] - att_w - f_w - 2 * d) // 2
    gd = f_w // N_FGROUPS
    dims = (att_w, kv_w, f_w, gd)

    perm = np.concatenate([np.arange(0, HEAD_DIM, 2), np.arange(1, HEAD_DIM, 2)])
    w = w_in[0]

    def deinterleave(cols):
        return cols.reshape(d, -1, HEAD_DIM)[:, :, perm].reshape(d, -1)

    s_q, s_k, s_v, s_f = att_w, att_w + kv_w, att_w + 2 * kv_w, att_w + 2 * kv_w + f_w
    w_main = jnp.concatenate([deinterleave(w[:, :s_q]), deinterleave(w[:, s_q:s_k]), w[:, s_k:s_v],
                              w[:, s_f:]], axis=1).astype(BF16)
    cch, sch = _dft_tables(gd)
    q_bound = jnp.maximum(jnp.max(jnp.abs(q_norm_g[0])) * LOG2E, FP8_TINY)
    k_bound = jnp.maximum(jnp.max(jnp.abs(k_norm_g[0])) * math.sqrt(HEAD_DIM), FP8_TINY)
    qk_fp8_ok = jnp.sqrt(2.0 * q_bound * k_bound) < FP8_TARGET
    qk_bal = jnp.where(qk_fp8_ok, jnp.exp2(jnp.round(0.5 * jnp.log2(k_bound / q_bound))), 1.0)
    u_max = jnp.maximum(jnp.max(jnp.abs(peer_u[0])), FP8_TINY)
    v_max = jnp.maximum(jnp.max(jnp.abs(peer_v[0])), FP8_TINY)
    wts = {
        "norm1_g": norm1_g[0], "norm2_g": norm2_g[0], "final_g": final_g,
        "w_main": w_main,
        "w_f": w[:, s_v:s_f].astype(BF16),
        "cs_ch": jnp.asarray(np.concatenate([cch, sch], axis=1), BF16),
        "qg": (q_norm_g[0][perm] * (HEAD_DIM ** -0.5 * LOG2E * qk_bal)).reshape(1, HEAD_DIM),
        "kg": (k_norm_g[0][perm] / qk_bal).reshape(1, HEAD_DIM),
        "qk_fp8_ok": qk_fp8_ok,
        "wa": w_attn_br[0].astype(BF16), "wf": w_four_br[0].astype(BF16),
        "w_out": w_out[0].astype(BF16),
        "wq_t": w_peer_q[0].T.astype(BF16),
        "keys": peer_keys[0].astype(BF16),
        "u_q": (peer_u[0] * (FP8_TARGET / u_max)).astype(FP8), "u_inv": u_max * (1.0 / FP8_TARGET),
        "v_q": (peer_v[0] * (FP8_TARGET / v_max)).astype(FP8), "v_inv": v_max * (1.0 / FP8_TARGET),
    }

    nbp, nbs = c_prompt.shape[0], c_sample.shape[0]
    c_all = jnp.concatenate([c_prompt, c_sample], axis=0)
    c_pad = jnp.pad(c_all, ((0, (-c_all.shape[0]) % 8), (0, 0)))
    mod = _modulation(c_pad, w_ada[0], b_ada[0]).reshape(c_pad.shape[0], 6, d)

    y_prompt = _trunk(x_prompt, mod[:nbp], wts, dims)
    y_sample = _trunk(x_sample, mod[nbp:nbp + nbs], wts, dims)
    return (y_prompt, y_sample)
```
